```python
import jax, jax.numpy as jnp
from jax import lax
import numpy as np

D_MODEL = 1024
BATCH = 2
SEQ = 8192
DEPTH = 4
DEC_BATCH = 128
DEC_SEQ = 8
PAST_LEN = 2048
PAGE_SIZE = 128

D_MIX = D_MODEL
CONV_W = 4
HEAD_DIM = 64
ATT_WIDTH = 3 * D_MIX // 8
ATT_HEADS = ATT_WIDTH // HEAD_DIM
DILATED = ((128, 1), (512, 4), (2048, 16))
MAX_WINDOW = 2048
ATT_BLK = 128
ATT_SCALE = HEAD_DIM ** -0.5
SSD_WIDTH = 3 * D_MIX // 8
SSD_HEAD_DIM = 64
SSD_HEADS = SSD_WIDTH // SSD_HEAD_DIM
SSD_GROUPS = 2
SSD_STATE = 128
SSD_CHUNK = 128
SSD_CONV_CH = SSD_WIDTH + 2 * SSD_GROUPS * SSD_STATE
LRU_WIDTH = D_MIX - ATT_WIDTH - SSD_WIDTH
LRU_BLOCKS = 4
LRU_BLOCK = LRU_WIDTH // LRU_BLOCKS
LRU_C = 8.0
D_FF = -(-8 * D_MODEL // (3 * 256)) * 256
N_IN = 2 * LRU_WIDTH + 3 * ATT_WIDTH + SSD_WIDTH + SSD_CONV_CH + SSD_HEADS
EPS = 1e-6

kernel_name = "hymba_lru_dilatedswa_ssd_decode_step"


def _rmsnorm(x, g):
    xf = x.astype(jnp.float32)
    y = xf * lax.rsqrt(jnp.mean(xf * xf, axis=-1, keepdims=True) + EPS)
    return (y * g.astype(jnp.float32)).astype(x.dtype)


def _split_cols(p):
    sizes = (LRU_WIDTH, LRU_WIDTH, ATT_WIDTH, ATT_WIDTH, ATT_WIDTH, SSD_WIDTH, SSD_CONV_CH, SSD_HEADS)
    offs = np.cumsum(sizes)[:-1].tolist()
    return jnp.split(p, offs, axis=-1)


def _causal_conv(x, buf, w, b):
    T = x.shape[1]
    xp = jnp.concatenate([buf, x], axis=1)
    y = b + xp[:, 0:T] * w[0]
    for j in range(1, CONV_W):
        y = y + xp[:, j:j + T] * w[j]
    return y, xp[:, -(CONV_W - 1):]


def _rg_lru(x, h0, wa, ba, wx, bx, lam):
    b, T, _ = x.shape
    xb = x.reshape(b, T, LRU_BLOCKS, LRU_BLOCK)
    r = jax.nn.sigmoid((jnp.einsum('btki,kij->btkj', xb, wa).reshape(b, T, LRU_WIDTH) + ba).astype(jnp.float32))
    i = jax.nn.sigmoid((jnp.einsum('btki,kij->btkj', xb, wx).reshape(b, T, LRU_WIDTH) + bx).astype(jnp.float32))
    log_a = -LRU_C * r * jax.nn.softplus(-lam.astype(jnp.float32))
    a = jnp.exp(log_a)
    u = jnp.sqrt(-jnp.expm1(2.0 * log_a)) * (i * x.astype(jnp.float32))
    u = u.at[:, 0].add(a[:, 0] * h0.astype(jnp.float32))

    def comb(left, right):
        return (left[0] * right[0], right[0] * left[1] + right[1])

    _, h = lax.associative_scan(comb, (a, u), axis=1)
    return h.astype(x.dtype), h[:, -1].astype(x.dtype)


def _dilated_branch_prompt(q, k, v, window, dil):
    b, S, H, Dh = q.shape
    span = window // dil
    L = S // dil
    n = b * dil

    def to_res(t):
        return t.reshape(b, L, dil, H, Dh).transpose(0, 2, 1, 3, 4).reshape(n, L, H, Dh)

    nb = -(-L // ATT_BLK)
    Lp = nb * ATT_BLK
    pad = ((0, 0), (0, Lp - L), (0, 0), (0, 0))
    qb, kb, vb = [jnp.pad(to_res(t), pad).reshape(n, nb, ATT_BLK, H, Dh) for t in (q, k, v)]
    zblk = ((0, 0), (1, 0), (0, 0), (0, 0), (0, 0))
    kk = jnp.concatenate([jnp.pad(kb, zblk)[:, :-1], kb], axis=2)
    vv = jnp.concatenate([jnp.pad(vb, zblk)[:, :-1], vb], axis=2)
    s = jnp.einsum('nbqhd,nbkhd->nbhqk', qb, kk, preferred_element_type=jnp.float32) * ATT_SCALE
    qi = jnp.arange(ATT_BLK)[:, None]
    ki = jnp.arange(2 * ATT_BLK)[None, :]
    dist = ATT_BLK + qi - ki
    kpos = (jnp.arange(nb)[:, None, None] - 1) * ATT_BLK + ki[None]
    mask = ((dist >= 0) & (dist <= span))[None] & (kpos >= 0)
    s = jnp.where(mask[None, :, None], s, -jnp.inf)
    lse = jax.nn.logsumexp(s, axis=-1)
    p = jnp.exp(s - lse[..., None])
    o = jnp.einsum('nbhqk,nbkhd->nbqhd', p.astype(vv.dtype), vv, preferred_element_type=jnp.float32)
    o = o.reshape(n, Lp, H, Dh)[:, :L].reshape(b, dil, L, H, Dh).transpose(0, 2, 1, 3, 4).reshape(b, S, H, Dh)
    lse = lse.transpose(0, 1, 3, 2).reshape(n, Lp, H)[:, :L].reshape(b, dil, L, H).transpose(0, 2, 1, 3).reshape(b, S, H)
    return o, lse


def _dilated_branch_decode(q, kc, vc, window, dil):
    T = q.shape[1]
    Lb = kc.shape[1] - T
    idx = Lb + jnp.arange(T)[:, None] - dil * jnp.arange(window // dil + 1)[None, :]
    valid = idx >= 0
    idx = jnp.maximum(idx, 0)
    kg = kc[:, idx]
    vg = vc[:, idx]
    s = jnp.einsum('bthd,btkhd->bthk', q, kg, preferred_element_type=jnp.float32) * ATT_SCALE
    s = jnp.where(valid[None, :, None, :], s, -jnp.inf)
    lse = jax.nn.logsumexp(s, axis=-1)
    p = jnp.exp(s - lse[..., None])
    o = jnp.einsum('bthk,btkhd->bthd', p.astype(vg.dtype), vg, preferred_element_type=jnp.float32)
    return o, lse


def _combine_branches(branches):
    o = jnp.stack([br[0] for br in branches])
    w = jax.nn.softmax(jnp.stack([br[1] for br in branches]), axis=0)
    return jnp.sum(w[..., None] * o, axis=0)


def _ssd(x, dt, A, Bm, Cm, h0, chunk):
    b, T, h, p = x.shape
    c = T // chunk
    hpg = SSD_HEADS // SSD_GROUPS
    Br = jnp.repeat(Bm.astype(jnp.float32), hpg, axis=2).reshape(b, c, chunk, h, SSD_STATE)
    Cr = jnp.repeat(Cm.astype(jnp.float32), hpg, axis=2).reshape(b, c, chunk, h, SSD_STATE)
    xr = (x.astype(jnp.float32) * dt[..., None]).reshape(b, c, chunk, h, p)
    acum = jnp.cumsum((dt * A).reshape(b, c, chunk, h), axis=2)
    causal = jnp.tril(jnp.ones((chunk, chunk), dtype=bool))[None, None, :, :, None]
    diff = acum[:, :, :, None, :] - acum[:, :, None, :, :]
    Lmat = jnp.exp(jnp.where(causal, diff, -jnp.inf))
    G = jnp.einsum('bclhn,bcshn->bclsh', Cr, Br) * Lmat
    y_diag = jnp.einsum('bclsh,bcshp->bclhp', G, xr)
    decay = jnp.exp(acum[:, :, -1:, :] - acum)
    st = jnp.einsum('bclhn,bclh,bclhp->bchpn', Br, decay, xr)
    tot = jnp.exp(acum[:, :, -1, :])

    def step(hc, inp):
        st_c, tot_c = inp
        return hc * tot_c[:, :, None, None] + st_c, hc

    hT, prev = lax.scan(step, h0.astype(jnp.float32), (st.transpose(1, 0, 2, 3, 4), tot.transpose(1, 0, 2)))
    prev = prev.transpose(1, 0, 2, 3, 4)
    y_off = jnp.einsum('bclhn,bchpn,bclh->bclhp', Cr, prev, jnp.exp(acum))
    return (y_diag + y_off).reshape(b, T, h, p), hT


def _layer(x, prm, st, is_prompt):
    (g_mix_in, g_mix_out, w_in, conv_a_w, conv_a_b, lru_wa, lru_ba, lru_wx, lru_bx, lru_lambda,
     conv_c_w, conv_c_b, dt_bias, a_log, d_skip, ssm_norm, w_out, g_ffn_in, g_ffn_out, w_gate_up, w_down) = prm
    lru_h0, lru_buf, k_buf, v_buf, ssd_h0, ssd_buf = st
    b, T, _ = x.shape
    h = _rmsnorm(x, g_mix_in)
    gate_a, x_a, q, k, v, z_c, xbc, dt_raw = _split_cols(h @ w_in)

    x_a, lru_buf_new = _causal_conv(x_a, lru_buf, conv_a_w, conv_a_b)
    h_a, lru_h_new = _rg_lru(x_a, lru_h0, lru_wa, lru_ba, lru_wx, lru_bx, lru_lambda)
    out_a = h_a * jax.nn.gelu(gate_a)

    q = q.reshape(b, T, ATT_HEADS, HEAD_DIM)
    k = k.reshape(b, T, ATT_HEADS, HEAD_DIM)
    v = v.reshape(b, T, ATT_HEADS, HEAD_DIM)
    if is_prompt:
        out_b = _combine_branches([_dilated_branch_prompt(q, k, v, w, d) for (w, d) in DILATED])
        keep = min(MAX_WINDOW, T)
        k_new, v_new = k[:, T - keep:], v[:, T - keep:]
    else:
        kc = jnp.concatenate([k_buf, k], axis=1)
        vc = jnp.concatenate([v_buf, v], axis=1)
        out_b = _combine_branches([_dilated_branch_decode(q, kc, vc, w, d) for (w, d) in DILATED])
        k_new, v_new = k, v
    out_b = out_b.reshape(b, T, ATT_WIDTH).astype(x.dtype)

    xbc, ssd_buf_new = _causal_conv(xbc, ssd_buf, conv_c_w, conv_c_b)
    xbc = jax.nn.silu(xbc)
    xs = xbc[..., :SSD_WIDTH].reshape(b, T, SSD_HEADS, SSD_HEAD_DIM)
    Bm = xbc[..., SSD_WIDTH:SSD_WIDTH + SSD_GROUPS * SSD_STATE].reshape(b, T, SSD_GROUPS, SSD_STATE)
    Cm = xbc[..., SSD_WIDTH + SSD_GROUPS * SSD_STATE:].reshape(b, T, SSD_GROUPS, SSD_STATE)
    dt = jax.nn.softplus(dt_raw.astype(jnp.float32) + dt_bias.astype(jnp.float32))
    A = -jnp.exp(a_log.astype(jnp.float32))
    chunk = SSD_CHUNK if is_prompt else T
    y_c, ssd_h_new = _ssd(xs, dt, A, Bm, Cm, ssd_h0, chunk)
    y_c = y_c + d_skip.astype(jnp.float32)[:, None] * xs.astype(jnp.float32)
    y_c = y_c.reshape(b, T, SSD_WIDTH) * jax.nn.silu(z_c.astype(jnp.float32))
    out_c = _rmsnorm(y_c, ssm_norm).astype(x.dtype)

    mix = jnp.concatenate([out_a, out_b, out_c], axis=-1) @ w_out
    x = x + _rmsnorm(mix, g_mix_out)

    hf = _rmsnorm(x, g_ffn_in)
    g_f, u_f = jnp.split(hf @ w_gate_up, 2, axis=-1)
    x = x + _rmsnorm((jax.nn.silu(g_f) * u_f) @ w_down, g_ffn_out)
    new_state = (lru_h_new, lru_buf_new, k_new, v_new, ssd_h_new.astype(x.dtype), ssd_buf_new)
    return x, new_state


def setup_inputs(seed: int = 0) -> dict:
    key = jax.random.key(seed)
    ks = iter(jax.random.split(key, 40))
    f32 = jnp.float32

    def nrm(shape, scale):
        return scale * jax.random.normal(next(ks), shape, f32)

    def gain(shape):
        return 1.0 + nrm(shape, 0.02)

    LB = min(MAX_WINDOW, PAST_LEN)
    a0 = jax.random.uniform(next(ks), (DEPTH, LRU_WIDTH), f32, minval=0.9, maxval=0.999)
    s0 = a0 ** (1.0 / LRU_C)
    lru_lambda = jnp.log(s0) - jnp.log1p(-s0)
    dt0 = jnp.exp(jax.random.uniform(next(ks), (DEPTH, SSD_HEADS), f32, minval=float(np.log(1e-3)), maxval=float(np.log(1e-1))))
    dt_bias = dt0 + jnp.log(-jnp.expm1(-dt0))
    a_log = jnp.log(jax.random.uniform(next(ks), (DEPTH, SSD_HEADS), f32, minval=1.0, maxval=16.0))
    return {
        "x_prompt": nrm((BATCH, SEQ, D_MODEL), 1.0),
        "x_sample": nrm((DEC_BATCH, DEC_SEQ, D_MODEL), 1.0),
        "state_lru_h": nrm((DEPTH, DEC_BATCH, LRU_WIDTH), 0.5),
        "state_lru_conv": nrm((DEPTH, DEC_BATCH, CONV_W - 1, LRU_WIDTH), 1.0),
        "cache_swa_k": nrm((DEPTH, DEC_BATCH, LB, ATT_HEADS, HEAD_DIM), 1.0),
        "cache_swa_v": nrm((DEPTH, DEC_BATCH, LB, ATT_HEADS, HEAD_DIM), 1.0),
        "state_ssd": nrm((DEPTH, DEC_BATCH, SSD_HEADS, SSD_HEAD_DIM, SSD_STATE), 0.1),
        "state_ssd_conv": nrm((DEPTH, DEC_BATCH, CONV_W - 1, SSD_CONV_CH), 1.0),
        "norm_mix_in": gain((DEPTH, D_MODEL)),
        "norm_mix_out": gain((DEPTH, D_MODEL)),
        "w_in": nrm((DEPTH, D_MODEL, N_IN), D_MODEL ** -0.5),
        "conv_a_w": nrm((DEPTH, CONV_W, LRU_WIDTH), 0.5),
        "conv_a_b": nrm((DEPTH, LRU_WIDTH), 0.01),
        "lru_wa": nrm((DEPTH, LRU_BLOCKS, LRU_BLOCK, LRU_BLOCK), LRU_BLOCK ** -0.5),
        "lru_ba": nrm((DEPTH, LRU_WIDTH), 0.01),
        "lru_wx": nrm((DEPTH, LRU_BLOCKS, LRU_BLOCK, LRU_BLOCK), LRU_BLOCK ** -0.5),
        "lru_bx": nrm((DEPTH, LRU_WIDTH), 0.01),
        "lru_lambda": lru_lambda,
        "conv_c_w": nrm((DEPTH, CONV_W, SSD_CONV_CH), 0.5),
        "conv_c_b": nrm((DEPTH, SSD_CONV_CH), 0.01),
        "dt_bias": dt_bias,
        "a_log": a_log,
        "d_skip": 1.0 + nrm((DEPTH, SSD_HEADS), 0.1),
        "ssm_norm": gain((DEPTH, SSD_WIDTH)),
        "w_out": nrm((DEPTH, D_MIX, D_MODEL), D_MIX ** -0.5),
        "norm_ffn_in": gain((DEPTH, D_MODEL)),
        "norm_ffn_out": gain((DEPTH, D_MODEL)),
        "w_gate_up": nrm((DEPTH, D_MODEL, 2 * D_FF), D_MODEL ** -0.5),
        "w_down": nrm((DEPTH, D_FF, D_MODEL), D_FF ** -0.5),
    }


def reference(x_prompt, x_sample, state_lru_h, state_lru_conv, cache_swa_k, cache_swa_v, state_ssd, state_ssd_conv,
              norm_mix_in, norm_mix_out, w_in, conv_a_w, conv_a_b, lru_wa, lru_ba, lru_wx, lru_bx, lru_lambda,
              conv_c_w, conv_c_b, dt_bias, a_log, d_skip, ssm_norm, w_out, norm_ffn_in, norm_ffn_out,
              w_gate_up, w_down):
    dtype = x_prompt.dtype
    bp = x_prompt.shape[0]
    yp, ys = x_prompt, x_sample
    p_new = [[] for _ in range(6)]
    s_new = [[] for _ in range(6)]
    for l in range(DEPTH):
        prm = (norm_mix_in[l], norm_mix_out[l], w_in[l], conv_a_w[l], conv_a_b[l], lru_wa[l], lru_ba[l],
               lru_wx[l], lru_bx[l], lru_lambda[l], conv_c_w[l], conv_c_b[l], dt_bias[l], a_log[l], d_skip[l],
               ssm_norm[l], w_out[l], norm_ffn_in[l], norm_ffn_out[l], w_gate_up[l], w_down[l])
        st_p = (jnp.zeros((bp, LRU_WIDTH), dtype), jnp.zeros((bp, CONV_W - 1, LRU_WIDTH), dtype), None, None,
                jnp.zeros((bp, SSD_HEADS, SSD_HEAD_DIM, SSD_STATE), dtype),
                jnp.zeros((bp, CONV_W - 1, SSD_CONV_CH), dtype))
        yp, np_l = _layer(yp, prm, st_p, True)
        st_s = (state_lru_h[l], state_lru_conv[l], cache_swa_k[l], cache_swa_v[l], state_ssd[l], state_ssd_conv[l])
        ys, ns_l = _layer(ys, prm, st_s, False)
        for j in range(6):
            p_new[j].append(np_l[j])
            s_new[j].append(ns_l[j])
    p_lru_h, p_lru_conv, p_swa_k, p_swa_v, p_ssd, p_ssd_conv = [jnp.stack(a) for a in p_new]
    s_lru_h, s_lru_conv, s_swa_k, s_swa_v, s_ssd, s_ssd_conv = [jnp.stack(a) for a in s_new]
    return (yp, ys, p_lru_h, p_lru_conv, p_swa_k, p_swa_v, p_ssd, p_ssd_conv,
            s_lru_h, s_lru_conv, s_swa_k, s_swa_v, s_ssd, s_ssd_conv)
```

```python
import functools

import numpy as np
import jax
import jax.numpy as jnp
from jax import lax
from jax.experimental import pallas as pl
from jax.experimental.pallas import tpu as pltpu

F32 = jnp.float32
BF16 = jnp.bfloat16

D_MODEL = 1024
DEPTH = 4
CONV_W = 4
HEAD_DIM = 64
ATT_WIDTH = 384
ATT_HEADS = 6
ATT_SPAN = 128
DILATIONS = (1, 4, 16)
MAX_WINDOW = 2048
ATT_SCALE = HEAD_DIM ** -0.5
SSD_WIDTH = 384
SSD_HEADS = 6
SSD_HEAD_DIM = 64
SSD_GROUPS = 2
SSD_STATE = 128
SSD_CHUNK = 128
SSD_CONV_CH = 896
LRU_WIDTH = 256
LRU_BLOCKS = 4
LRU_C = 8.0
D_FF = 2816
N_IN = 2950
EPS = 1e-6

LANES = 128
SUBLANES = 8
N_IN_PAD = 3072
VMEM_LIMIT = 56 * 1024 * 1024

_IN_GROUPS = ((0, 256), (256, 256), (512, 384), (896, 384), (1280, 384), (1664, 384), (2048, 896), (2944, 128))

NEG_INF = float("-inf")


def _cparams(sem):
    return pltpu.CompilerParams(dimension_semantics=sem, vmem_limit_bytes=VMEM_LIMIT)


def _const_spec(shape):
    nd = len(shape)
    return pl.BlockSpec(shape, lambda *_: (0,) * nd, pipeline_mode=pl.Buffered(1))


def _rms(x, g):
    ms = jnp.mean(x * x, axis=-1, keepdims=True)
    return x * lax.rsqrt(ms + EPS) * g


def _sigmoid(x):
    return jax.nn.sigmoid(x)


def _silu(x):
    return x * jax.nn.sigmoid(x)


def _softplus(x):
    return jnp.maximum(x, 0.0) + jnp.log1p(jnp.exp(-jnp.abs(x)))


def _gelu_tanh(x):
    c = np.sqrt(2.0 / np.pi).astype(np.float32)
    return 0.5 * x * (1.0 + jnp.tanh(c * (x + 0.044715 * (x * x * x))))


def _roll_rows(x, shift):
    n = x.shape[0]
    shift = shift % n
    if shift == 0:
        return x
    return pltpu.roll(x, shift, 0)


def _row_index(shape, group=None):
    r = lax.broadcasted_iota(jnp.int32, shape, 0)
    if group is not None:
        r = jnp.bitwise_and(r, group - 1)
    return r


def _lin_scan(a, u, row, steps):
    s = 1
    for _ in range(steps):
        keep = row >= s
        a_sh = jnp.where(keep, _roll_rows(a, s), 1.0)
        u_sh = jnp.where(keep, _roll_rows(u, s), 0.0)
        u = a * u_sh + u
        a = a * a_sh
        s *= 2
    return a, u


def _cumsum_rows(x, row, steps):
    s = 1
    for _ in range(steps):
        x = x + jnp.where(row >= s, _roll_rows(x, s), 0.0)
        s *= 2
    return x


def _suffix_sum_rows(x, row, group, steps):
    incl = x
    s = 1
    for _ in range(steps):
        incl = incl + jnp.where(row < group - s, _roll_rows(incl, -s), 0.0)
        s *= 2
    return incl - x


def _conv_taps(x, shifted_fn, w_ref, b_ref):
    y = b_ref[...] + w_ref[CONV_W - 1:CONV_W, :] * x
    for s in range(1, CONV_W):
        y = y + w_ref[CONV_W - 1 - s:CONV_W - s, :] * shifted_fn(s)
    return y


def _conv_block_carry(x, tail_ref, w_ref, b_ref):
    tb = x.shape[0]
    row8 = _row_index((SUBLANES, x.shape[1]))
    tail = tail_ref[...]

    def shifted(s):
        rolled = _roll_rows(x, s)
        first = jnp.where(row8 >= s, rolled[:SUBLANES], _roll_rows(tail, s - (CONV_W - 1)))
        return jnp.concatenate([first, rolled[SUBLANES:]], axis=0)

    y = _conv_taps(x, shifted, w_ref, b_ref)
    tail_ref[...] = _roll_rows(x[tb - SUBLANES:tb], CONV_W - 1)
    return y


def _conv_groups(x, bufpad, w_ref, b_ref):
    row = _row_index(x.shape, SUBLANES)
    return _conv_taps(
        x, lambda s: jnp.where(row >= s, _roll_rows(x, s), _roll_rows(bufpad, s - (CONV_W - 1))), w_ref, b_ref)


def _lru_gates(xc, wa_ref, ba_ref, wx_ref, bx_ref, lam_ref):
    xb = xc.astype(BF16)
    r = _sigmoid(jnp.dot(xb, wa_ref[...], preferred_element_type=F32) + ba_ref[...])
    ig = _sigmoid(jnp.dot(xb, wx_ref[...], preferred_element_type=F32) + bx_ref[...])
    log_a = (-LRU_C) * r * _softplus(-lam_ref[...])
    a = jnp.exp(log_a)
    t = jnp.tanh(log_a)
    u = jnp.sqrt(-2.0 * t / (1.0 - t)) * (ig * xc)
    return a, u


def _in_proj_kernel(x_ref, g_ref, w_ref, *out_refs):
    h = _rms(x_ref[...], g_ref[...]).astype(BF16)
    for ref, (off, width) in zip(out_refs, _IN_GROUPS):
        ref[...] = jnp.dot(h, w_ref[:, off:off + width], preferred_element_type=F32)


def _in_proj(x, g, w, tm):
    n = x.shape[0]
    return pl.pallas_call(
        _in_proj_kernel,
        grid=(n // tm,),
        in_specs=[pl.BlockSpec((tm, D_MODEL), lambda i: (i, 0)),
                  _const_spec((1, D_MODEL)),
                  _const_spec((D_MODEL, N_IN_PAD))],
        out_specs=[pl.BlockSpec((tm, width), lambda i: (i, 0)) for _, width in _IN_GROUPS],
        out_shape=[jax.ShapeDtypeStruct((n, width), F32) for _, width in _IN_GROUPS],
        compiler_params=_cparams(("parallel",)),
        name="in_proj",
    )(x, g, w)


def _out_proj_kernel(a_ref, b_ref, c_ref, x_ref, w_ref, g_ref, o_ref):
    mix = jnp.dot(a_ref[...].astype(BF16), w_ref[0:LRU_WIDTH, :], preferred_element_type=F32)
    mix = mix + jnp.dot(b_ref[...].astype(BF16), w_ref[LRU_WIDTH:LRU_WIDTH + ATT_WIDTH, :],
                        preferred_element_type=F32)
    mix = mix + jnp.dot(c_ref[...].astype(BF16), w_ref[LRU_WIDTH + ATT_WIDTH:D_MODEL, :],
                        preferred_element_type=F32)
    o_ref[...] = x_ref[...] + _rms(mix, g_ref[...])


def _out_proj(a, b, c, x, w, g, tm):
    n = x.shape[0]
    row = lambda width: pl.BlockSpec((tm, width), lambda i: (i, 0))
    return pl.pallas_call(
        _out_proj_kernel,
        grid=(n // tm,),
        in_specs=[row(LRU_WIDTH), row(ATT_WIDTH), row(SSD_WIDTH), row(D_MODEL),
                  _const_spec((D_MODEL, D_MODEL)), _const_spec((1, D_MODEL))],
        out_specs=row(D_MODEL),
        out_shape=jax.ShapeDtypeStruct((n, D_MODEL), F32),
        compiler_params=_cparams(("parallel",)),
        name="out_proj",
    )(a, b, c, x, w, g)


FFN_CHUNK = 704


def _ffn_kernel(x_ref, gi_ref, wgu_ref, wd_ref, go_ref, o_ref):
    x = x_ref[...]
    h = _rms(x, gi_ref[...]).astype(BF16)
    acc = jnp.zeros(x.shape, F32)
    for c in range(D_FF // FFN_CHUNK):
        lo = c * FFN_CHUNK
        g = jnp.dot(h, wgu_ref[:, lo:lo + FFN_CHUNK], preferred_element_type=F32)
        u = jnp.dot(h, wgu_ref[:, D_FF + lo:D_FF + lo + FFN_CHUNK], preferred_element_type=F32)
        act = (_silu(g) * u).astype(BF16)
        acc = acc + jnp.dot(act, wd_ref[lo:lo + FFN_CHUNK, :], preferred_element_type=F32)
    o_ref[...] = x + _rms(acc, go_ref[...])


def _ffn(x, gi, wgu, wd, go, tm):
    n = x.shape[0]
    return pl.pallas_call(
        _ffn_kernel,
        grid=(n // tm,),
        in_specs=[pl.BlockSpec((tm, D_MODEL), lambda i: (i, 0)),
                  _const_spec((1, D_MODEL)),
                  _const_spec((D_MODEL, 2 * D_FF)),
                  _const_spec((D_FF, D_MODEL)),
                  _const_spec((1, D_MODEL))],
        out_specs=pl.BlockSpec((tm, D_MODEL), lambda i: (i, 0)),
        out_shape=jax.ShapeDtypeStruct((n, D_MODEL), F32),
        compiler_params=_cparams(("parallel",)),
        name="ffn",
    )(x, gi, wgu, wd, go)


def _mix_a_prompt_kernel(xa_ref, ga_ref, cw_ref, cb_ref, wa_ref, ba_ref, wx_ref, bx_ref, lam_ref,
                         out_ref, hlast_ref, tail_ref, hc_ref):
    @pl.when(pl.program_id(1) == 0)
    def _():
        tail_ref[...] = jnp.zeros_like(tail_ref)
        hc_ref[...] = jnp.zeros_like(hc_ref)

    x = xa_ref[...]
    tb = x.shape[0]
    xc = _conv_block_carry(x, tail_ref, cw_ref, cb_ref)
    a, u = _lru_gates(xc, wa_ref, ba_ref, wx_ref, bx_ref, lam_ref)
    row = _row_index(x.shape)
    a_cum, h = _lin_scan(a, u, row, int(np.log2(tb)))
    h = h + a_cum * hc_ref[0:1, :]
    last = h[tb - 1:tb, :]
    hc_ref[...] = jnp.broadcast_to(last, hc_ref.shape)
    hlast_ref[...] = last
    out_ref[...] = h * _gelu_tanh(ga_ref[...])


def _mix_a_prompt(xa, ga, cw, cb, wa, ba, wx, bx, lam, batch, tb):
    n = xa.shape[0]
    nb = n // batch // tb
    blk = pl.BlockSpec((tb, LRU_WIDTH), lambda b, i: (b * nb + i, 0))
    vec = _const_spec((1, LRU_WIDTH))
    mat = _const_spec((LRU_WIDTH, LRU_WIDTH))
    return pl.pallas_call(
        _mix_a_prompt_kernel,
        grid=(batch, nb),
        in_specs=[blk, blk, _const_spec((CONV_W, LRU_WIDTH)), vec, mat, vec, mat, vec, vec],
        out_specs=[blk, pl.BlockSpec((None, 1, LRU_WIDTH), lambda b, i: (b, 0, 0))],
        out_shape=[jax.ShapeDtypeStruct((n, LRU_WIDTH), F32),
                   jax.ShapeDtypeStruct((batch, 1, LRU_WIDTH), F32)],
        scratch_shapes=[pltpu.VMEM((SUBLANES, LRU_WIDTH), F32), pltpu.VMEM((SUBLANES, LRU_WIDTH), F32)],
        compiler_params=_cparams(("parallel", "arbitrary")),
        name="mix_a_prompt",
    )(xa, ga, cw, cb, wa, ba, wx, bx, lam)


def _mix_a_decode_kernel(xa_ref, ga_ref, buf_ref, h0_ref, cw_ref, cb_ref, wa_ref, ba_ref, wx_ref, bx_ref,
                         lam_ref, out_ref, h_ref):
    x = xa_ref[...]
    xc = _conv_groups(x, buf_ref[...], cw_ref, cb_ref)
    a, u = _lru_gates(xc, wa_ref, ba_ref, wx_ref, bx_ref, lam_ref)
    u = u + a * h0_ref[...]
    row = _row_index(x.shape, SUBLANES)
    _, h = _lin_scan(a, u, row, 3)
    h_ref[...] = h
    out_ref[...] = h * _gelu_tanh(ga_ref[...])


def _mix_a_decode(xa, ga, bufpad, h0pad, cw, cb, wa, ba, wx, bx, lam, tm):
    n = xa.shape[0]
    blk = pl.BlockSpec((tm, LRU_WIDTH), lambda i: (i, 0))
    vec = _const_spec((1, LRU_WIDTH))
    mat = _const_spec((LRU_WIDTH, LRU_WIDTH))
    return pl.pallas_call(
        _mix_a_decode_kernel,
        grid=(n // tm,),
        in_specs=[blk, blk, blk, blk, _const_spec((CONV_W, LRU_WIDTH)), vec, mat, vec, mat, vec, vec],
        out_specs=[blk, blk],
        out_shape=[jax.ShapeDtypeStruct((n, LRU_WIDTH), F32), jax.ShapeDtypeStruct((n, LRU_WIDTH), F32)],
        compiler_params=_cparams(("parallel",)),
        name="mix_a_decode",
    )(xa, ga, bufpad, h0pad, cw, cb, wa, ba, wx, bx, lam)


ATT_BLK = 2048
ATT_UNIT = 128


def _attn_unit(q_t, kp_t, kc_t, vp_t, vc_t, bias, lane_lo):
    zero = jnp.zeros_like(q_t)
    qs = q_t * ATT_SCALE
    qq = jnp.concatenate([jnp.where(lane_lo, qs, zero), jnp.where(lane_lo, zero, qs)], axis=0).astype(BF16)
    kk = jnp.concatenate([kp_t, kc_t], axis=0).astype(BF16)
    s = lax.dot_general(qq, kk, (((1,), (1,)), ((), ())), preferred_element_type=F32) + bias
    m = jnp.max(s, axis=1, keepdims=True)
    p = jnp.exp(s - m).astype(BF16)
    pcat = jnp.concatenate([p[:ATT_UNIT], p[ATT_UNIT:]], axis=1)
    one_lo = jnp.where(lane_lo, 1.0, 0.0)
    one_hi = 1.0 - one_lo
    w = jnp.concatenate([
        jnp.concatenate([jnp.where(lane_lo, vp_t, zero), one_lo], axis=1),
        jnp.concatenate([jnp.where(lane_lo, vc_t, zero), one_lo], axis=1),
        jnp.concatenate([jnp.where(lane_lo, zero, vp_t), one_hi], axis=1),
        jnp.concatenate([jnp.where(lane_lo, zero, vc_t), one_hi], axis=1)], axis=0).astype(BF16)
    ol = jnp.dot(pcat, w, preferred_element_type=F32)
    m_t = jnp.where(lane_lo, jnp.broadcast_to(m[:ATT_UNIT], q_t.shape), jnp.broadcast_to(m[ATT_UNIT:], q_t.shape))
    return ol[:, :LANES], m_t, ol[:, LANES:]


def _attn_prompt_kernel(q_ref, kp_ref, kc_ref, vp_ref, vc_ref, o_ref, acc_ref, m_ref, l_ref):
    first_block = pl.program_id(2) == 0
    lane_lo = lax.broadcasted_iota(jnp.int32, (ATT_UNIT, LANES), 1) < HEAD_DIM
    qi = lax.broadcasted_iota(jnp.int32, (2 * ATT_UNIT, 2 * ATT_UNIT), 0) & (ATT_UNIT - 1)
    ki = lax.broadcasted_iota(jnp.int32, (2 * ATT_UNIT, 2 * ATT_UNIT), 1)
    is_prev = ki < ATT_UNIT
    dist = qi - ki + ATT_UNIT
    bias = jnp.where(dist >= 0, jnp.where(dist <= ATT_SPAN, 0.0, NEG_INF), NEG_INF)
    bias_first = bias + jnp.where(is_prev, jnp.where(first_block, NEG_INF, 0.0), 0.0)

    def rows(start, d):
        if d == 1:
            return pl.ds(start, ATT_UNIT)
        return pl.ds(start, ATT_UNIT, stride=d)

    def merge(sl, o_t, m_t, l_t, first):
        if first:
            acc_ref[sl, :] = o_t
            m_ref[sl, :] = m_t
            l_ref[sl, :] = l_t
        else:
            m_old = m_ref[sl, :]
            m_new = jnp.maximum(m_old, m_t)
            a = jnp.exp(m_old - m_new)
            b = jnp.exp(m_t - m_new)
            acc_ref[sl, :] = a * acc_ref[sl, :] + b * o_t
            l_ref[sl, :] = a * l_ref[sl, :] + b * l_t
            m_ref[sl, :] = m_new

    for bi, d in enumerate(DILATIONS):
        nj = ATT_BLK // (ATT_UNIT * d)
        first = bi == 0

        def head_unit(rho, carry, d=d, nj=nj, first=first):
            cur = rows(rho, d)
            prev = rows(rho + d * ATT_UNIT * (nj - 1), d)
            o_t, m_t, l_t = _attn_unit(q_ref[cur, :], kp_ref[prev, :], kc_ref[cur, :], vp_ref[prev, :],
                                       vc_ref[cur, :], bias_first, lane_lo)
            merge(cur, o_t, m_t, l_t, first)
            return carry

        def inner_unit(idx, carry, d=d, first=first):
            rho = idx & (d - 1)
            j = 1 + lax.shift_right_logical(idx, int(np.log2(d)))
            start = rho + d * ATT_UNIT * j
            cur = rows(start, d)
            prev = rows(start - d * ATT_UNIT, d)
            o_t, m_t, l_t = _attn_unit(q_ref[cur, :], kc_ref[prev, :], kc_ref[cur, :], vc_ref[prev, :],
                                       vc_ref[cur, :], bias, lane_lo)
            merge(cur, o_t, m_t, l_t, first)
            return carry

        lax.fori_loop(0, d, head_unit, 0)
        if nj > 1:
            lax.fori_loop(0, d * (nj - 1), inner_unit, 0)

    o_ref[...] = acc_ref[...] / l_ref[...]


def _attn_prompt(q, k, v, batch):
    n = q.shape[0]
    nb = n // batch // ATT_BLK
    npair = ATT_WIDTH // LANES
    cur = pl.BlockSpec((ATT_BLK, LANES), lambda b, hp, i: (b * nb + i, hp))
    prev = pl.BlockSpec((ATT_BLK, LANES), lambda b, hp, i: (b * nb + jnp.maximum(i - 1, 0), hp))
    scratch = pltpu.VMEM((ATT_BLK, LANES), F32)
    return pl.pallas_call(
        _attn_prompt_kernel,
        grid=(batch, npair, nb),
        in_specs=[cur, prev, cur, prev, cur],
        out_specs=cur,
        out_shape=jax.ShapeDtypeStruct((n, ATT_WIDTH), F32),
        scratch_shapes=[scratch, scratch, scratch],
        compiler_params=_cparams(("parallel", "parallel", "parallel")),
        name="attn_prompt",
    )(q, k, k, v, v)


DEC_T = 8
DEC_GATHER_ROWS = MAX_WINDOW - 4 * ATT_SPAN
DEC_GATHER_GROUPS = DEC_GATHER_ROWS // 16
DEC_TAIL_ROWS = MAX_WINDOW - DEC_GATHER_ROWS
DEC_NEW_PAD = 128
DEC_KEYS = DEC_GATHER_GROUPS * DEC_T + DEC_TAIL_ROWS + DEC_NEW_PAD
DEC_BT = 2


def _decode_key_multiplicity():
    pos = np.full((DEC_KEYS,), -10 ** 9, np.int64)
    g = np.arange(DEC_GATHER_GROUPS * DEC_T)
    pos[:g.size] = 16 * (g // DEC_T) + (g % DEC_T)
    pos[g.size:g.size + DEC_TAIL_ROWS] = DEC_GATHER_ROWS + np.arange(DEC_TAIL_ROWS)
    pos[g.size + DEC_TAIL_ROWS:g.size + DEC_TAIL_ROWS + DEC_T] = MAX_WINDOW + np.arange(DEC_T)
    t = np.arange(DEC_T)
    dist = (MAX_WINDOW + t)[None, :] - pos[:, None]
    cnt = np.zeros(dist.shape, np.float32)
    for d in DILATIONS:
        cnt += ((dist >= 0) & (dist % d == 0) & (dist <= ATT_SPAN * d)).astype(np.float32)
    out = np.zeros((DEC_KEYS, LANES), np.float32)
    out[:, :ATT_HEADS * DEC_T] = np.tile(cnt, (1, ATT_HEADS))
    return out


def _attn_decode_kernel(q_ref, kn_ref, vn_ref, kg_ref, kt_ref, vg_ref, vt_ref, mult_ref, o_ref):
    lane = lax.broadcasted_iota(jnp.int32, (DEC_T, ATT_WIDTH), 1)
    head_masks = [jnp.logical_and(lane >= HEAD_DIM * h, lane < HEAD_DIM * (h + 1)) for h in range(ATT_HEADS)]
    mult = mult_ref[...]
    seen = mult > 0.0
    g_rows = DEC_GATHER_GROUPS * DEC_T
    zpad = jnp.zeros((DEC_NEW_PAD - DEC_T, ATT_WIDTH), F32)
    nt = (((1,), (1,)), ((), ()))
    for b in range(DEC_BT):
        qb = q_ref[DEC_T * b:DEC_T * (b + 1), :] * ATT_SCALE
        qbd = jnp.concatenate([jnp.where(hm, qb, 0.0) for hm in head_masks]
                              + [jnp.zeros((LANES - ATT_HEADS * DEC_T, ATT_WIDTH), F32)], axis=0).astype(BF16)
        k_segs = [kg_ref[b].reshape(g_rows, ATT_WIDTH), kt_ref[b],
                  jnp.concatenate([kn_ref[DEC_T * b:DEC_T * (b + 1), :], zpad], axis=0)]
        v_segs = [vg_ref[b].reshape(g_rows, ATT_WIDTH), vt_ref[b],
                  jnp.concatenate([vn_ref[DEC_T * b:DEC_T * (b + 1), :], zpad], axis=0)]
        s = jnp.concatenate([lax.dot_general(ks.astype(BF16), qbd, nt, preferred_element_type=F32)
                             for ks in k_segs], axis=0)
        s = jnp.where(seen, s, NEG_INF)
        m = jnp.maximum(jnp.max(s, axis=0, keepdims=True), -1e30)
        p = mult * jnp.exp(s - m)
        l = jnp.sum(p, axis=0, keepdims=True)
        pt = (p / jnp.where(l > 0.0, l, 1.0)).T.astype(BF16)
        o = jnp.zeros((LANES, ATT_WIDTH), F32)
        lo = 0
        for vs in v_segs:
            o = o + jnp.dot(pt[:, lo:lo + vs.shape[0]], vs.astype(BF16), preferred_element_type=F32)
            lo += vs.shape[0]
        out = jnp.zeros((DEC_T, ATT_WIDTH), F32)
        for h, hm in enumerate(head_masks):
            out = out + jnp.where(hm, o[DEC_T * h:DEC_T * (h + 1), :], 0.0)
        o_ref[DEC_T * b:DEC_T * (b + 1), :] = out


def _attn_decode(q, k, v, cache_k, cache_v, layer, mult):
    n = q.shape[0]
    nseq = n // DEC_T
    ck5 = cache_k.reshape(DEPTH, nseq, MAX_WINDOW // 16, 16, ATT_WIDTH)
    cv5 = cache_v.reshape(DEPTH, nseq, MAX_WINDOW // 16, 16, ATT_WIDTH)
    ck4 = cache_k.reshape(DEPTH, nseq, MAX_WINDOW, ATT_WIDTH)
    cv4 = cache_v.reshape(DEPTH, nseq, MAX_WINDOW, ATT_WIDTH)
    new = pl.BlockSpec((DEC_T * DEC_BT, ATT_WIDTH), lambda g: (g, 0))
    gath = pl.BlockSpec((None, DEC_BT, DEC_GATHER_GROUPS, DEC_T, ATT_WIDTH), lambda g: (layer, g, 0, 0, 0))
    tail = pl.BlockSpec((None, DEC_BT, DEC_TAIL_ROWS, ATT_WIDTH),
                        lambda g: (layer, g, DEC_GATHER_ROWS // DEC_TAIL_ROWS, 0))
    return pl.pallas_call(
        _attn_decode_kernel,
        grid=(nseq // DEC_BT,),
        in_specs=[new, new, new, gath, tail, gath, tail, _const_spec((DEC_KEYS, LANES))],
        out_specs=new,
        out_shape=jax.ShapeDtypeStruct((n, ATT_WIDTH), F32),
        compiler_params=_cparams(("parallel",)),
        name="attn_decode",
    )(q, k, v, ck5, ck4, cv5, cv4, mult)


HPG = SSD_HEADS // SSD_GROUPS
B_OFF = SSD_WIDTH
C_OFF = SSD_WIDTH + SSD_GROUPS * SSD_STATE
NT_DIMS = (((1,), (1,)), ((), ()))


def _ssd_chunk_diag(xbc, dt, acum, pair_ok):
    acum_t = acum.T
    ys, xrs = [], []
    for g in range(SSD_GROUPS):
        bg = xbc[:, B_OFF + g * SSD_STATE:B_OFF + (g + 1) * SSD_STATE].astype(BF16)
        cg = xbc[:, C_OFF + g * SSD_STATE:C_OFF + (g + 1) * SSD_STATE].astype(BF16)
        cb = lax.dot_general(cg, bg, NT_DIMS, preferred_element_type=F32)
        for h in range(g * HPG, (g + 1) * HPG):
            xr = xbc[:, h * SSD_HEAD_DIM:(h + 1) * SSD_HEAD_DIM] * dt[:, h:h + 1]
            diff = acum[:, h:h + 1] - acum_t[h:h + 1, :]
            lmat = jnp.exp(jnp.where(pair_ok, diff, NEG_INF))
            ys.append(jnp.dot((cb * lmat).astype(BF16), xr.astype(BF16), preferred_element_type=F32))
            xrs.append(xr)
    return ys, xrs


def _ssd_finish(y, xs, z, dskip_ref, norm_ref):
    y = y + dskip_ref[...] * xs
    y = y * _silu(z)
    return _rms(y, norm_ref[...])


def _ssd_prompt_kernel(xbc_ref, z_ref, dt_ref, cw_ref, cb_ref, dtb_ref, alog_ref, dskip_ref, norm_ref,
                       out_ref, state_ref, tail_ref, xc_ref, st_ref):
    @pl.when(pl.program_id(1) == 0)
    def _():
        tail_ref[...] = jnp.zeros_like(tail_ref)
        st_ref[...] = jnp.zeros_like(st_ref)

    tb = xbc_ref.shape[0]
    xc_ref[...] = _silu(_conv_block_carry(xbc_ref[...], tail_ref, cw_ref, cb_ref))
    a_neg = -jnp.exp(alog_ref[...])
    row = _row_index((SSD_CHUNK, LANES))
    li = lax.broadcasted_iota(jnp.int32, (SSD_CHUNK, SSD_CHUNK), 0)
    si = lax.broadcasted_iota(jnp.int32, (SSD_CHUNK, SSD_CHUNK), 1)
    causal = li >= si

    def chunk(c, carry):
        r0 = pl.multiple_of(c * SSD_CHUNK, SSD_CHUNK)
        xbc = xc_ref[pl.ds(r0, SSD_CHUNK), :]
        dt = _softplus(dt_ref[pl.ds(r0, SSD_CHUNK), :] + dtb_ref[...])
        acum = _cumsum_rows(dt * a_neg, row, 7)
        last = acum[SSD_CHUNK - 1:SSD_CHUNK, :]
        decay = jnp.exp(last - acum)
        eac = jnp.exp(acum)
        tot = jnp.exp(last)
        ys, xrs = _ssd_chunk_diag(xbc, dt, acum, causal)
        xrd_t = jnp.concatenate([xrs[h] * decay[:, h:h + 1] for h in range(SSD_HEADS)], axis=1).T.astype(BF16)
        outs = []
        for h in range(SSD_HEADS):
            g = h // HPG
            bg = xbc[:, B_OFF + g * SSD_STATE:B_OFF + (g + 1) * SSD_STATE].astype(BF16)
            cg = xbc[:, C_OFF + g * SSD_STATE:C_OFF + (g + 1) * SSD_STATE].astype(BF16)
            prev = st_ref[h]
            y_off = lax.dot_general(cg, prev.astype(BF16), NT_DIMS, preferred_element_type=F32) * eac[:, h:h + 1]
            st = jnp.dot(xrd_t[h * SSD_HEAD_DIM:(h + 1) * SSD_HEAD_DIM, :], bg, preferred_element_type=F32)
            st_ref[h] = prev * tot[:, h:h + 1] + st
            outs.append(ys[h] + y_off)
        y = jnp.concatenate(outs, axis=1)
        out_ref[pl.ds(r0, SSD_CHUNK), :] = _ssd_finish(y, xbc[:, :SSD_WIDTH], z_ref[pl.ds(r0, SSD_CHUNK), :],
                                                       dskip_ref, norm_ref)
        return carry

    lax.fori_loop(0, tb // SSD_CHUNK, chunk, 0)
    state_ref[...] = st_ref[...]


def _ssd_prompt(xbc, z, dt, cw, cb, dtb, alog, dskip, norm, batch, tb):
    n = xbc.shape[0]
    nb = n // batch // tb
    blk = lambda width: pl.BlockSpec((tb, width), lambda b, i: (b * nb + i, 0))
    return pl.pallas_call(
        _ssd_prompt_kernel,
        grid=(batch, nb),
        in_specs=[blk(SSD_CONV_CH), blk(SSD_WIDTH), blk(LANES),
                  _const_spec((CONV_W, SSD_CONV_CH)), _const_spec((1, SSD_CONV_CH)),
                  _const_spec((1, LANES)), _const_spec((1, LANES)),
                  _const_spec((1, SSD_WIDTH)), _const_spec((1, SSD_WIDTH))],
        out_specs=[blk(SSD_WIDTH),
                   pl.BlockSpec((None, SSD_HEADS, SSD_HEAD_DIM, SSD_STATE), lambda b, i: (b, 0, 0, 0))],
        out_shape=[jax.ShapeDtypeStruct((n, SSD_WIDTH), F32),
                   jax.ShapeDtypeStruct((batch, SSD_HEADS, SSD_HEAD_DIM, SSD_STATE), F32)],
        scratch_shapes=[pltpu.VMEM((SUBLANES, SSD_CONV_CH), F32),
                        pltpu.VMEM((tb, SSD_CONV_CH), F32),
                        pltpu.VMEM((SSD_HEADS, SSD_HEAD_DIM, SSD_STATE), F32)],
        compiler_params=_cparams(("parallel", "arbitrary")),
        name="ssd_prompt",
    )(xbc, z, dt, cw, cb, dtb, alog, dskip, norm)


SSD_DEC_SEQ = SSD_CHUNK // DEC_T


def _ssd_decode_kernel(xbc_ref, z_ref, dt_ref, buf_ref, h0_ref, cw_ref, cb_ref, dtb_ref, alog_ref, dskip_ref,
                       norm_ref, out_ref, hnew_ref, xc_ref, xrd_ref, eac_ref, tot_ref, yoff_ref):
    xc = _silu(_conv_groups(xbc_ref[...], buf_ref[...], cw_ref, cb_ref))
    xc_ref[...] = xc
    a_neg = -jnp.exp(alog_ref[...])
    row = _row_index((SSD_CHUNK, LANES), DEC_T)
    li = lax.broadcasted_iota(jnp.int32, (SSD_CHUNK, SSD_CHUNK), 0)
    si = lax.broadcasted_iota(jnp.int32, (SSD_CHUNK, SSD_CHUNK), 1)
    same_seq_causal = jnp.logical_and(li >= si, (li - si) <= (li & (DEC_T - 1)))
    dt = _softplus(dt_ref[...] + dtb_ref[...])
    dta = dt * a_neg
    acum = _cumsum_rows(dta, row, 3)
    rest = _suffix_sum_rows(dta, row, DEC_T, 3)
    eac_ref[...] = jnp.exp(acum)
    tot_ref[...] = jnp.exp(acum + rest)
    decay = jnp.exp(rest)
    ys, xrs = _ssd_chunk_diag(xc, dt, acum, same_seq_causal)
    xrd_ref[...] = jnp.concatenate([xrs[h] * decay[:, h:h + 1] for h in range(SSD_HEADS)], axis=1).T
    seq_of_lane = lax.shift_right_logical(lax.broadcasted_iota(jnp.int32, (SSD_HEAD_DIM, SSD_CHUNK), 1), 3)
    b_all = [xc[:, B_OFF + g * SSD_STATE:B_OFF + (g + 1) * SSD_STATE].astype(BF16) for g in range(SSD_GROUPS)]

    def seq(b, carry):
        r0 = pl.multiple_of(b * DEC_T, DEC_T)
        rows = pl.ds(r0, DEC_T)
        xcb = xc_ref[rows, :]
        eac = eac_ref[rows, :]
        tot = tot_ref[rows, :]
        own = seq_of_lane == b
        outs = []
        for h in range(SSD_HEADS):
            g = h // HPG
            cg = xcb[:, C_OFF + g * SSD_STATE:C_OFF + (g + 1) * SSD_STATE].astype(BF16)
            prev = h0_ref[b, h]
            outs.append(lax.dot_general(cg, prev.astype(BF16), NT_DIMS, preferred_element_type=F32)
                        * eac[:, h:h + 1])
            lhs = jnp.where(own, xrd_ref[h * SSD_HEAD_DIM:(h + 1) * SSD_HEAD_DIM, :], 0.0).astype(BF16)
            st = jnp.dot(lhs, b_all[g], preferred_element_type=F32)
            hnew_ref[b, h] = prev * tot[0:1, h:h + 1] + st
        yoff_ref[rows, :] = jnp.concatenate(outs, axis=1)
        return carry

    lax.fori_loop(0, SSD_DEC_SEQ, seq, 0)
    y = jnp.concatenate(ys, axis=1) + yoff_ref[...]
    out_ref[...] = _ssd_finish(y, xc[:, :SSD_WIDTH], z_ref[...], dskip_ref, norm_ref)


def _ssd_decode(xbc, z, dt, bufpad, state, layer, cw, cb, dtb, alog, dskip, norm):
    n = xbc.shape[0]
    blk = lambda width: pl.BlockSpec((SSD_CHUNK, width), lambda i: (i, 0))
    st_in = pl.BlockSpec((None, SSD_DEC_SEQ, SSD_HEADS, SSD_HEAD_DIM, SSD_STATE), lambda i: (layer, i, 0, 0, 0))
    st_out = pl.BlockSpec((SSD_DEC_SEQ, SSD_HEADS, SSD_HEAD_DIM, SSD_STATE), lambda i: (i, 0, 0, 0))
    return pl.pallas_call(
        _ssd_decode_kernel,
        grid=(n // SSD_CHUNK,),
        in_specs=[blk(SSD_CONV_CH), blk(SSD_WIDTH), blk(LANES), blk(SSD_CONV_CH), st_in,
                  _const_spec((CONV_W, SSD_CONV_CH)), _const_spec((1, SSD_CONV_CH)),
                  _const_spec((1, LANES)), _const_spec((1, LANES)),
                  _const_spec((1, SSD_WIDTH)), _const_spec((1, SSD_WIDTH))],
        out_specs=[blk(SSD_WIDTH), st_out],
        out_shape=[jax.ShapeDtypeStruct((n, SSD_WIDTH), F32),
                   jax.ShapeDtypeStruct((n // DEC_T, SSD_HEADS, SSD_HEAD_DIM, SSD_STATE), F32)],
        scratch_shapes=[pltpu.VMEM((SSD_CHUNK, SSD_CONV_CH), F32),
                        pltpu.VMEM((SSD_WIDTH, SSD_CHUNK), F32),
                        pltpu.VMEM((SSD_CHUNK, LANES), F32),
                        pltpu.VMEM((SSD_CHUNK, LANES), F32),
                        pltpu.VMEM((SSD_CHUNK, SSD_WIDTH), F32)],
        compiler_params=_cparams(("parallel",)),
        name="ssd_decode",
    )(xbc, z, dt, bufpad, state, cw, cb, dtb, alog, dskip, norm)


DENSE_TM = 512
MIX_A_TB = 512
SSD_TB = 1024


def _pad_state_rows(buf):
    l, b, r, c = buf.shape
    return jnp.pad(buf, ((0, 0), (0, 0), (0, DEC_T - r), (0, 0))).reshape(l, b * DEC_T, c)


def kernel(x_prompt, x_sample, state_lru_h, state_lru_conv, cache_swa_k, cache_swa_v, state_ssd, state_ssd_conv,
           norm_mix_in, norm_mix_out, w_in, conv_a_w, conv_a_b, lru_wa, lru_ba, lru_wx, lru_bx, lru_lambda,
           conv_c_w, conv_c_b, dt_bias, a_log, d_skip, ssm_norm, w_out, norm_ffn_in, norm_ffn_out,
           w_gate_up, w_down):
    bp, seq, _ = x_prompt.shape
    bs, dec_t, _ = x_sample.shape
    assert dec_t == DEC_T and seq % ATT_BLK == 0 and bs % SSD_DEC_SEQ == 0

    w_in_b = jnp.pad(w_in, ((0, 0), (0, 0), (0, N_IN_PAD - N_IN))).astype(BF16)
    w_out_b = w_out.astype(BF16)
    w_gu_b = w_gate_up.astype(BF16)
    w_dn_b = w_down.astype(BF16)
    eye = jnp.eye(LRU_BLOCKS, dtype=F32)

    def block_diag(w):
        return (w[:, :, :, None, :] * eye[None, :, None, :, None]).reshape(
            DEPTH, LRU_WIDTH, LRU_WIDTH).astype(BF16)

    wa_bd = block_diag(lru_wa)
    wx_bd = block_diag(lru_wx)
    vec = lambda p: p[:, None, :]
    pad_lanes = lambda p: jnp.pad(p, ((0, 0), (0, LANES - p.shape[1])))[:, None, :]
    dtb_p = pad_lanes(dt_bias)
    alog_p = pad_lanes(a_log)
    dskip_p = jnp.repeat(d_skip, SSD_HEAD_DIM, axis=1)[:, None, :]
    buf_a = _pad_state_rows(state_lru_conv)
    buf_c = _pad_state_rows(state_ssd_conv)
    h0_a = jnp.pad(state_lru_h[:, :, None, :], ((0, 0), (0, 0), (0, DEC_T - 1), (0, 0))).reshape(
        DEPTH, bs * DEC_T, LRU_WIDTH)
    mult = jnp.asarray(_decode_key_multiplicity())

    yp = x_prompt.reshape(bp * seq, D_MODEL)
    ys = x_sample.reshape(bs * DEC_T, D_MODEL)
    p_new = [[] for _ in range(6)]
    s_new = [[] for _ in range(6)]
    for l in range(DEPTH):
        a_args = (conv_a_w[l], vec(conv_a_b)[l], wa_bd[l], vec(lru_ba)[l], wx_bd[l], vec(lru_bx)[l],
                  vec(lru_lambda)[l])
        c_args = (conv_c_w[l], vec(conv_c_b)[l], dtb_p[l], alog_p[l], dskip_p[l], vec(ssm_norm)[l])
        g_in, g_out = vec(norm_mix_in)[l], vec(norm_mix_out)[l]
        f_in, f_out = vec(norm_ffn_in)[l], vec(norm_ffn_out)[l]

        ga, xa, q, k, v, z, xbc, dt = _in_proj(yp, g_in, w_in_b[l], DENSE_TM)
        out_a, h_last = _mix_a_prompt(xa, ga, *a_args, batch=bp, tb=MIX_A_TB)
        out_b = _attn_prompt(q, k, v, bp)
        out_c, ssd_state = _ssd_prompt(xbc, z, dt, *c_args, batch=bp, tb=SSD_TB)
        yp = _out_proj(out_a, out_b, out_c, yp, w_out_b[l], g_out, DENSE_TM)
        yp = _ffn(yp, f_in, w_gu_b[l], w_dn_b[l], f_out, DENSE_TM)
        keep = min(MAX_WINDOW, seq)
        p_new[0].append(h_last.reshape(bp, LRU_WIDTH))
        p_new[1].append(xa.reshape(bp, seq, LRU_WIDTH)[:, seq - (CONV_W - 1):])
        p_new[2].append(k.reshape(bp, seq, ATT_HEADS, HEAD_DIM)[:, seq - keep:])
        p_new[3].append(v.reshape(bp, seq, ATT_HEADS, HEAD_DIM)[:, seq - keep:])
        p_new[4].append(ssd_state)
        p_new[5].append(xbc.reshape(bp, seq, SSD_CONV_CH)[:, seq - (CONV_W - 1):])

        ga, xa, q, k, v, z, xbc, dt = _in_proj(ys, g_in, w_in_b[l], DENSE_TM)
        out_a, h_all = _mix_a_decode(xa, ga, buf_a[l], h0_a[l], *a_args, tm=DENSE_TM)
        out_b = _attn_decode(q, k, v, cache_swa_k, cache_swa_v, l, mult)
        out_c, ssd_state = _ssd_decode(xbc, z, dt, buf_c[l], state_ssd, l, *c_args)
        ys = _out_proj(out_a, out_b, out_c, ys, w_out_b[l], g_out, DENSE_TM)
        ys = _ffn(ys, f_in, w_gu_b[l], w_dn_b[l], f_out, DENSE_TM)
        s_new[0].append(h_all.reshape(bs, DEC_T, LRU_WIDTH)[:, DEC_T - 1])
        s_new[1].append(xa.reshape(bs, DEC_T, LRU_WIDTH)[:, DEC_T - (CONV_W - 1):])
        s_new[2].append(k.reshape(bs, DEC_T, ATT_HEADS, HEAD_DIM))
        s_new[3].append(v.reshape(bs, DEC_T, ATT_HEADS, HEAD_DIM))
        s_new[4].append(ssd_state)
        s_new[5].append(xbc.reshape(bs, DEC_T, SSD_CONV_CH)[:, DEC_T - (CONV_W - 1):])

    outs_p = [jnp.stack(a) for a in p_new]
    outs_s = [jnp.stack(a) for a in s_new]
    return (yp.reshape(bp, seq, D_MODEL), ys.reshape(bs, DEC_T, D_MODEL), *outs_p, *outs_s)
```

```python
import functools

import numpy as np
import jax
import jax.numpy as jnp
from jax import lax
from jax.experimental import pallas as pl
from jax.experimental.pallas import tpu as pltpu

F32 = jnp.float32
BF16 = jnp.bfloat16

D_MODEL = 1024
DEPTH = 4
CONV_W = 4
HEAD_DIM = 64
ATT_WIDTH = 384
ATT_HEADS = 6
ATT_SPAN = 128
DILATIONS = (1, 4, 16)
MAX_WINDOW = 2048
ATT_SCALE = HEAD_DIM ** -0.5
SSD_WIDTH = 384
SSD_HEADS = 6
SSD_HEAD_DIM = 64
SSD_GROUPS = 2
SSD_STATE = 128
SSD_CHUNK = 128
SSD_CONV_CH = 896
LRU_WIDTH = 256
LRU_BLOCKS = 4
LRU_C = 8.0
D_FF = 2816
N_IN = 2950
EPS = 1e-6

LANES = 128
SUBLANES = 8
N_IN_PAD = 3072
VMEM_LIMIT = 56 * 1024 * 1024

_IN_GROUPS = ((0, 256), (256, 256), (512, 384), (896, 384), (1280, 384), (1664, 384), (2048, 896), (2944, 128))

NEG_INF = float("-inf")
NT_DIMS = (((1,), (1,)), ((), ()))


def _cparams(sem):
    return pltpu.CompilerParams(dimension_semantics=sem, vmem_limit_bytes=VMEM_LIMIT)


def _const_spec(shape):
    nd = len(shape)
    return pl.BlockSpec(shape, lambda *_: (0,) * nd, pipeline_mode=pl.Buffered(1))


def _rms(x, g):
    ms = jnp.mean(x * x, axis=-1, keepdims=True)
    return x * lax.rsqrt(ms + EPS) * g


def _sigmoid(x):
    return jax.nn.sigmoid(x)


def _silu(x):
    return x * jax.nn.sigmoid(x)


def _softplus(x):
    return jnp.maximum(x, 0.0) + jnp.log1p(jnp.exp(-jnp.abs(x)))


def _gelu_tanh(x):
    c = np.sqrt(2.0 / np.pi).astype(np.float32)
    return 0.5 * x * (1.0 + jnp.tanh(c * (x + 0.044715 * (x * x * x))))


def _roll_rows(x, shift):
    n = x.shape[0]
    shift = shift % n
    if shift == 0:
        return x
    return pltpu.roll(x, shift, 0)


def _row_index(shape, group=None):
    r = lax.broadcasted_iota(jnp.int32, shape, 0)
    if group is not None:
        r = jnp.bitwise_and(r, group - 1)
    return r


def _lin_scan(a, u, row, steps):
    s = 1
    for _ in range(steps):
        keep = row >= s
        a_sh = jnp.where(keep, _roll_rows(a, s), 1.0)
        u_sh = jnp.where(keep, _roll_rows(u, s), 0.0)
        u = a * u_sh + u
        a = a * a_sh
        s *= 2
    return a, u


def _cumsum_rows(x, row, steps):
    s = 1
    for _ in range(steps):
        x = x + jnp.where(row >= s, _roll_rows(x, s), 0.0)
        s *= 2
    return x


def _suffix_sum_rows(x, row, group, steps):
    incl = x
    s = 1
    for _ in range(steps):
        incl = incl + jnp.where(row < group - s, _roll_rows(incl, -s), 0.0)
        s *= 2
    return incl - x


def _conv_taps(x, shifted_fn, w_ref, b_ref):
    y = b_ref[...] + w_ref[CONV_W - 1:CONV_W, :] * x
    for s in range(1, CONV_W):
        y = y + w_ref[CONV_W - 1 - s:CONV_W - s, :] * shifted_fn(s)
    return y


def _conv_block_carry(x, tail_ref, w_ref, b_ref):
    tb = x.shape[0]
    row8 = _row_index((SUBLANES, x.shape[1]))
    tail = tail_ref[...]

    def shifted(s):
        rolled = _roll_rows(x, s)
        first = jnp.where(row8 >= s, rolled[:SUBLANES], _roll_rows(tail, s - (CONV_W - 1)))
        return jnp.concatenate([first, rolled[SUBLANES:]], axis=0)

    y = _conv_taps(x, shifted, w_ref, b_ref)
    tail_ref[...] = _roll_rows(x[tb - SUBLANES:tb], CONV_W - 1)
    return y


def _conv_groups(x, bufpad, w_ref, b_ref):
    row = _row_index(x.shape, SUBLANES)
    return _conv_taps(
        x, lambda s: jnp.where(row >= s, _roll_rows(x, s), _roll_rows(bufpad, s - (CONV_W - 1))), w_ref, b_ref)


def _lru_gates(xc, wa_ref, ba_ref, wx_ref, bx_ref, lam_ref):
    xb = xc.astype(BF16)
    r = _sigmoid(jnp.dot(xb, wa_ref[...], preferred_element_type=F32) + ba_ref[...])
    ig = _sigmoid(jnp.dot(xb, wx_ref[...], preferred_element_type=F32) + bx_ref[...])
    log_a = (-LRU_C) * r * _softplus(-lam_ref[...])
    a = jnp.exp(log_a)
    t = jnp.tanh(log_a)
    u = jnp.sqrt(-2.0 * t / (1.0 - t)) * (ig * xc)
    return a, u


def _in_proj_kernel(x_ref, g_ref, w_ref, *out_refs):
    h = _rms(x_ref[...], g_ref[...]).astype(BF16)
    for ref, (off, width) in zip(out_refs, _IN_GROUPS):
        ref[...] = jnp.dot(h, w_ref[:, off:off + width], preferred_element_type=F32)


def _in_proj(x, g, w, tm):
    n = x.shape[0]
    return pl.pallas_call(
        _in_proj_kernel,
        grid=(n // tm,),
        in_specs=[pl.BlockSpec((tm, D_MODEL), lambda i: (i, 0)),
                  _const_spec((1, D_MODEL)),
                  _const_spec((D_MODEL, N_IN_PAD))],
        out_specs=[pl.BlockSpec((tm, width), lambda i: (i, 0)) for _, width in _IN_GROUPS],
        out_shape=[jax.ShapeDtypeStruct((n, width), F32) for _, width in _IN_GROUPS],
        compiler_params=_cparams(("parallel",)),
        name="in_proj",
    )(x, g, w)


def _out_proj_kernel(a_ref, b_ref, c_ref, x_ref, w_ref, g_ref, o_ref):
    mix = jnp.dot(a_ref[...].astype(BF16), w_ref[0:LRU_WIDTH, :], preferred_element_type=F32)
    mix = mix + jnp.dot(b_ref[...].astype(BF16), w_ref[LRU_WIDTH:LRU_WIDTH + ATT_WIDTH, :],
                        preferred_element_type=F32)
    mix = mix + jnp.dot(c_ref[...].astype(BF16), w_ref[LRU_WIDTH + ATT_WIDTH:D_MODEL, :],
                        preferred_element_type=F32)
    o_ref[...] = x_ref[...] + _rms(mix, g_ref[...])


def _out_proj(a, b, c, x, w, g, tm):
    n = x.shape[0]
    row = lambda width: pl.BlockSpec((tm, width), lambda i: (i, 0))
    return pl.pallas_call(
        _out_proj_kernel,
        grid=(n // tm,),
        in_specs=[row(LRU_WIDTH), row(ATT_WIDTH), row(SSD_WIDTH), row(D_MODEL),
                  _const_spec((D_MODEL, D_MODEL)), _const_spec((1, D_MODEL))],
        out_specs=row(D_MODEL),
        out_shape=jax.ShapeDtypeStruct((n, D_MODEL), F32),
        compiler_params=_cparams(("parallel",)),
        name="out_proj",
    )(a, b, c, x, w, g)


FFN_CHUNK = 704


def _ffn_kernel(x_ref, gi_ref, wgu_ref, wd_ref, go_ref, o_ref):
    x = x_ref[...]
    h = _rms(x, gi_ref[...]).astype(BF16)
    acc = jnp.zeros(x.shape, F32)
    for c in range(D_FF // FFN_CHUNK):
        lo = c * FFN_CHUNK
        g = jnp.dot(h, wgu_ref[:, lo:lo + FFN_CHUNK], preferred_element_type=F32)
        u = jnp.dot(h, wgu_ref[:, D_FF + lo:D_FF + lo + FFN_CHUNK], preferred_element_type=F32)
        act = (_silu(g) * u).astype(BF16)
        acc = acc + jnp.dot(act, wd_ref[lo:lo + FFN_CHUNK, :], preferred_element_type=F32)
    o_ref[...] = x + _rms(acc, go_ref[...])


def _ffn(x, gi, wgu, wd, go, tm):
    n = x.shape[0]
    return pl.pallas_call(
        _ffn_kernel,
        grid=(n // tm,),
        in_specs=[pl.BlockSpec((tm, D_MODEL), lambda i: (i, 0)),
                  _const_spec((1, D_MODEL)),
                  _const_spec((D_MODEL, 2 * D_FF)),
                  _const_spec((D_FF, D_MODEL)),
                  _const_spec((1, D_MODEL))],
        out_specs=pl.BlockSpec((tm, D_MODEL), lambda i: (i, 0)),
        out_shape=jax.ShapeDtypeStruct((n, D_MODEL), F32),
        compiler_params=_cparams(("parallel",)),
        name="ffn",
    )(x, gi, wgu, wd, go)


def _mix_a_prompt_kernel(xa_ref, ga_ref, cw_ref, cb_ref, wa_ref, ba_ref, wx_ref, bx_ref, lam_ref,
                         out_ref, hlast_ref, tail_ref, hc_ref):
    @pl.when(pl.program_id(1) == 0)
    def _():
        tail_ref[...] = jnp.zeros_like(tail_ref)
        hc_ref[...] = jnp.zeros_like(hc_ref)

    x = xa_ref[...]
    tb = x.shape[0]
    xc = _conv_block_carry(x, tail_ref, cw_ref, cb_ref)
    a, u = _lru_gates(xc, wa_ref, ba_ref, wx_ref, bx_ref, lam_ref)
    row = _row_index(x.shape)
    a_cum, h = _lin_scan(a, u, row, int(np.log2(tb)))
    h = h + a_cum * hc_ref[0:1, :]
    last = h[tb - 1:tb, :]
    hc_ref[...] = jnp.broadcast_to(last, hc_ref.shape)
    hlast_ref[...] = last
    out_ref[...] = h * _gelu_tanh(ga_ref[...])


def _mix_a_prompt(xa, ga, cw, cb, wa, ba, wx, bx, lam, batch, tb):
    n = xa.shape[0]
    nb = n // batch // tb
    blk = pl.BlockSpec((tb, LRU_WIDTH), lambda b, i: (b * nb + i, 0))
    vec = _const_spec((1, LRU_WIDTH))
    mat = _const_spec((LRU_WIDTH, LRU_WIDTH))
    return pl.pallas_call(
        _mix_a_prompt_kernel,
        grid=(batch, nb),
        in_specs=[blk, blk, _const_spec((CONV_W, LRU_WIDTH)), vec, mat, vec, mat, vec, vec],
        out_specs=[blk, pl.BlockSpec((None, 1, LRU_WIDTH), lambda b, i: (b, 0, 0))],
        out_shape=[jax.ShapeDtypeStruct((n, LRU_WIDTH), F32),
                   jax.ShapeDtypeStruct((batch, 1, LRU_WIDTH), F32)],
        scratch_shapes=[pltpu.VMEM((SUBLANES, LRU_WIDTH), F32), pltpu.VMEM((SUBLANES, LRU_WIDTH), F32)],
        compiler_params=_cparams(("parallel", "arbitrary")),
        name="mix_a_prompt",
    )(xa, ga, cw, cb, wa, ba, wx, bx, lam)


def _mix_a_decode_kernel(xa_ref, ga_ref, buf_ref, h0_ref, cw_ref, cb_ref, wa_ref, ba_ref, wx_ref, bx_ref,
                         lam_ref, out_ref, h_ref):
    x = xa_ref[...]
    xc = _conv_groups(x, buf_ref[...], cw_ref, cb_ref)
    a, u = _lru_gates(xc, wa_ref, ba_ref, wx_ref, bx_ref, lam_ref)
    u = u + a * h0_ref[...]
    row = _row_index(x.shape, SUBLANES)
    _, h = _lin_scan(a, u, row, 3)
    h_ref[...] = h
    out_ref[...] = h * _gelu_tanh(ga_ref[...])


def _mix_a_decode(xa, ga, bufpad, h0pad, cw, cb, wa, ba, wx, bx, lam, tm):
    n = xa.shape[0]
    blk = pl.BlockSpec((tm, LRU_WIDTH), lambda i: (i, 0))
    vec = _const_spec((1, LRU_WIDTH))
    mat = _const_spec((LRU_WIDTH, LRU_WIDTH))
    return pl.pallas_call(
        _mix_a_decode_kernel,
        grid=(n // tm,),
        in_specs=[blk, blk, blk, blk, _const_spec((CONV_W, LRU_WIDTH)), vec, mat, vec, mat, vec, vec],
        out_specs=[blk, blk],
        out_shape=[jax.ShapeDtypeStruct((n, LRU_WIDTH), F32), jax.ShapeDtypeStruct((n, LRU_WIDTH), F32)],
        compiler_params=_cparams(("parallel",)),
        name="mix_a_decode",
    )(xa, ga, bufpad, h0pad, cw, cb, wa, ba, wx, bx, lam)


ATT_BLK = 2048
ATT_UNIT = 128
ATT_UNROLL = 4


def _attn_unit(q_t, kp_t, kc_t, vp_t, vc_t, bias, lane_lo):
    zero = jnp.zeros_like(q_t)
    qs = q_t * ATT_SCALE
    qq = jnp.concatenate([jnp.where(lane_lo, qs, zero), jnp.where(lane_lo, zero, qs)], axis=0).astype(BF16)
    kk = jnp.concatenate([kp_t, kc_t], axis=0).astype(BF16)
    s = lax.dot_general(qq, kk, (((1,), (1,)), ((), ())), preferred_element_type=F32) + bias
    m = jnp.max(s, axis=1, keepdims=True)
    p = jnp.exp(s - m).astype(BF16)
    pcat = jnp.concatenate([p[:ATT_UNIT], p[ATT_UNIT:]], axis=1)
    one_lo = jnp.where(lane_lo, 1.0, 0.0)
    one_hi = 1.0 - one_lo
    w = jnp.concatenate([
        jnp.concatenate([jnp.where(lane_lo, vp_t, zero), one_lo], axis=1),
        jnp.concatenate([jnp.where(lane_lo, vc_t, zero), one_lo], axis=1),
        jnp.concatenate([jnp.where(lane_lo, zero, vp_t), one_hi], axis=1),
        jnp.concatenate([jnp.where(lane_lo, zero, vc_t), one_hi], axis=1)], axis=0).astype(BF16)
    ol = jnp.dot(pcat, w, preferred_element_type=F32)
    m_t = jnp.where(lane_lo, jnp.broadcast_to(m[:ATT_UNIT], q_t.shape), jnp.broadcast_to(m[ATT_UNIT:], q_t.shape))
    return ol[:, :LANES], m_t, ol[:, LANES:]


def _attn_prompt_kernel(q_ref, kp_ref, kc_ref, vp_ref, vc_ref, o_ref, acc_ref, m_ref, l_ref):
    first_block = pl.program_id(2) == 0
    lane_lo = lax.broadcasted_iota(jnp.int32, (ATT_UNIT, LANES), 1) < HEAD_DIM
    qi = lax.broadcasted_iota(jnp.int32, (2 * ATT_UNIT, 2 * ATT_UNIT), 0) & (ATT_UNIT - 1)
    ki = lax.broadcasted_iota(jnp.int32, (2 * ATT_UNIT, 2 * ATT_UNIT), 1)
    is_prev = ki < ATT_UNIT
    dist = qi - ki + ATT_UNIT
    bias = jnp.where(dist >= 0, jnp.where(dist <= ATT_SPAN, 0.0, NEG_INF), NEG_INF)
    bias_first = bias + jnp.where(is_prev, jnp.where(first_block, NEG_INF, 0.0), 0.0)

    def rows(start, d):
        if d == 1:
            return pl.ds(start, ATT_UNIT)
        return pl.ds(start, ATT_UNIT, stride=d)

    def keep(bi, sl, o_t, m_t, l_t):
        acc_ref[bi, sl, :] = o_t
        m_ref[bi, sl, :] = m_t
        l_ref[bi, sl, :] = l_t

    for bi, d in enumerate(DILATIONS):
        nj = ATT_BLK // (ATT_UNIT * d)

        def head_unit(rho, carry, bi=bi, d=d, nj=nj):
            cur = rows(rho, d)
            prev = rows(rho + d * ATT_UNIT * (nj - 1), d)
            keep(bi, cur, *_attn_unit(q_ref[cur, :], kp_ref[prev, :], kc_ref[cur, :], vp_ref[prev, :],
                                      vc_ref[cur, :], bias_first, lane_lo))
            return carry

        def inner_unit(idx, carry, bi=bi, d=d):
            rho = idx & (d - 1)
            j = 1 + lax.shift_right_logical(idx, int(np.log2(d)))
            start = rho + d * ATT_UNIT * j
            cur = rows(start, d)
            prev = rows(start - d * ATT_UNIT, d)
            keep(bi, cur, *_attn_unit(q_ref[cur, :], kc_ref[prev, :], kc_ref[cur, :], vc_ref[prev, :],
                                      vc_ref[cur, :], bias, lane_lo))
            return carry

        lax.fori_loop(0, d, head_unit, 0, unroll=min(d, ATT_UNROLL))
        n_inner = d * (nj - 1)
        if n_inner:
            lax.fori_loop(0, n_inner, inner_unit, 0, unroll=ATT_UNROLL if n_inner % ATT_UNROLL == 0 else 3)

    def combine(c, carry):
        sl = pl.ds(pl.multiple_of(c * ATT_UNIT, ATT_UNIT), ATT_UNIT)
        ms = [m_ref[bi, sl, :] for bi in range(len(DILATIONS))]
        m = functools.reduce(jnp.maximum, ms)
        ws = [jnp.exp(mi - m) for mi in ms]
        num = sum(w * acc_ref[bi, sl, :] for bi, w in enumerate(ws))
        den = sum(w * l_ref[bi, sl, :] for bi, w in enumerate(ws))
        o_ref[sl, :] = num / den
        return carry

    lax.fori_loop(0, ATT_BLK // ATT_UNIT, combine, 0, unroll=2)


def _attn_prompt(q, k, v, batch):
    n = q.shape[0]
    nb = n // batch // ATT_BLK
    npair = ATT_WIDTH // LANES
    cur = pl.BlockSpec((ATT_BLK, LANES), lambda b, hp, i: (b * nb + i, hp))
    prev = pl.BlockSpec((ATT_BLK, LANES), lambda b, hp, i: (b * nb + jnp.maximum(i - 1, 0), hp))
    scratch = pltpu.VMEM((len(DILATIONS), ATT_BLK, LANES), F32)
    return pl.pallas_call(
        _attn_prompt_kernel,
        grid=(batch, npair, nb),
        in_specs=[cur, prev, cur, prev, cur],
        out_specs=cur,
        out_shape=jax.ShapeDtypeStruct((n, ATT_WIDTH), F32),
        scratch_shapes=[scratch, scratch, scratch],
        compiler_params=_cparams(("parallel", "parallel", "parallel")),
        name="attn_prompt",
    )(q, k, k, v, v)


DEC_T = 8
DEC_NEW_PAD = 128
DEC_KEYS = MAX_WINDOW + DEC_NEW_PAD
DEC_ROWS = ATT_HEADS * DEC_T
DEC_BT = 2


def _decode_key_multiplicity():
    pos = np.full((DEC_KEYS,), -10 ** 9, np.int64)
    pos[:MAX_WINDOW] = np.arange(MAX_WINDOW)
    pos[MAX_WINDOW:MAX_WINDOW + DEC_T] = MAX_WINDOW + np.arange(DEC_T)
    t = np.arange(DEC_T)
    dist = (MAX_WINDOW + t)[:, None] - pos[None, :]
    cnt = np.zeros(dist.shape, np.float32)
    for d in DILATIONS:
        cnt += ((dist >= 0) & (dist % d == 0) & (dist <= ATT_SPAN * d)).astype(np.float32)
    return np.tile(cnt, (ATT_HEADS, 1))


def _attn_decode_kernel(q_ref, kn_ref, vn_ref, kt_ref, vt_ref, mult_ref, o_ref):
    lane = lax.broadcasted_iota(jnp.int32, (DEC_T, ATT_WIDTH), 1)
    head_masks = [jnp.logical_and(lane >= HEAD_DIM * h, lane < HEAD_DIM * (h + 1)) for h in range(ATT_HEADS)]
    mult = mult_ref[...]
    seen = mult > 0.0
    zpad = jnp.zeros((DEC_NEW_PAD - DEC_T, ATT_WIDTH), F32)
    for b in range(DEC_BT):
        new_rows = slice(DEC_T * b, DEC_T * (b + 1))
        qb = q_ref[new_rows, :] * ATT_SCALE
        qbd = jnp.concatenate([jnp.where(hm, qb, 0.0) for hm in head_masks], axis=0).astype(BF16)
        kt = kt_ref[b].reshape(ATT_WIDTH, MAX_WINDOW).astype(BF16)
        kn = jnp.concatenate([kn_ref[new_rows, :], zpad], axis=0).astype(BF16)
        s = jnp.concatenate([jnp.dot(qbd, kt, preferred_element_type=F32),
                             lax.dot_general(qbd, kn, NT_DIMS, preferred_element_type=F32)], axis=1)
        s = jnp.where(seen, s, NEG_INF)
        m = jnp.max(s, axis=1, keepdims=True)
        p = mult * jnp.exp(s - m)
        p = (p / jnp.sum(p, axis=1, keepdims=True)).astype(BF16)
        vt = vt_ref[b].reshape(ATT_WIDTH, MAX_WINDOW).astype(BF16)
        vn = jnp.concatenate([vn_ref[new_rows, :], zpad], axis=0).astype(BF16)
        o = (lax.dot_general(p[:, :MAX_WINDOW], vt, NT_DIMS, preferred_element_type=F32)
             + jnp.dot(p[:, MAX_WINDOW:], vn, preferred_element_type=F32))
        out = jnp.zeros((DEC_T, ATT_WIDTH), F32)
        for h, hm in enumerate(head_masks):
            out = out + jnp.where(hm, o[DEC_T * h:DEC_T * (h + 1), :], 0.0)
        o_ref[new_rows, :] = out


def _attn_decode(q, k, v, cache_kt, cache_vt, layer, mult):
    n = q.shape[0]
    nseq = n // DEC_T
    new = pl.BlockSpec((DEC_T * DEC_BT, ATT_WIDTH), lambda g: (g, 0))
    win = pl.BlockSpec((None, DEC_BT, ATT_HEADS, HEAD_DIM, MAX_WINDOW), lambda g: (layer, g, 0, 0, 0))
    return pl.pallas_call(
        _attn_decode_kernel,
        grid=(nseq // DEC_BT,),
        in_specs=[new, new, new, win, win, _const_spec((DEC_ROWS, DEC_KEYS))],
        out_specs=new,
        out_shape=jax.ShapeDtypeStruct((n, ATT_WIDTH), F32),
        compiler_params=_cparams(("parallel",)),
        name="attn_decode",
    )(q, k, v, cache_kt, cache_vt, mult)


HPG = SSD_HEADS // SSD_GROUPS
B_OFF = SSD_WIDTH
C_OFF = SSD_WIDTH + SSD_GROUPS * SSD_STATE


def _ssd_chunk_diag(xbc, dt, acum, pair_ok):
    acum_t = acum.T
    ys, xrs = [], []
    for g in range(SSD_GROUPS):
        bg = xbc[:, B_OFF + g * SSD_STATE:B_OFF + (g + 1) * SSD_STATE].astype(BF16)
        cg = xbc[:, C_OFF + g * SSD_STATE:C_OFF + (g + 1) * SSD_STATE].astype(BF16)
        cb = lax.dot_general(cg, bg, NT_DIMS, preferred_element_type=F32)
        for h in range(g * HPG, (g + 1) * HPG):
            xr = xbc[:, h * SSD_HEAD_DIM:(h + 1) * SSD_HEAD_DIM] * dt[:, h:h + 1]
            diff = acum[:, h:h + 1] - acum_t[h:h + 1, :]
            lmat = jnp.exp(jnp.where(pair_ok, diff, NEG_INF))
            ys.append(jnp.dot((cb * lmat).astype(BF16), xr.astype(BF16), preferred_element_type=F32))
            xrs.append(xr)
    return ys, xrs


def _ssd_finish(y, xs, z, dskip_ref, norm_ref):
    y = y + dskip_ref[...] * xs
    y = y * _silu(z)
    return _rms(y, norm_ref[...])


def _ssd_prompt_kernel(xbc_ref, z_ref, dt_ref, cw_ref, cb_ref, dtb_ref, alog_ref, dskip_ref, norm_ref,
                       out_ref, state_ref, tail_ref, xc_ref, st_ref):
    @pl.when(pl.program_id(1) == 0)
    def _():
        tail_ref[...] = jnp.zeros_like(tail_ref)
        st_ref[...] = jnp.zeros_like(st_ref)

    tb = xbc_ref.shape[0]
    xc_ref[...] = _silu(_conv_block_carry(xbc_ref[...], tail_ref, cw_ref, cb_ref))
    a_neg = -jnp.exp(alog_ref[...])
    row = _row_index((SSD_CHUNK, LANES))
    li = lax.broadcasted_iota(jnp.int32, (SSD_CHUNK, SSD_CHUNK), 0)
    si = lax.broadcasted_iota(jnp.int32, (SSD_CHUNK, SSD_CHUNK), 1)
    causal = li >= si

    def chunk(c, carry):
        r0 = pl.multiple_of(c * SSD_CHUNK, SSD_CHUNK)
        xbc = xc_ref[pl.ds(r0, SSD_CHUNK), :]
        dt = _softplus(dt_ref[pl.ds(r0, SSD_CHUNK), :] + dtb_ref[...])
        acum = _cumsum_rows(dt * a_neg, row, 7)
        last = acum[SSD_CHUNK - 1:SSD_CHUNK, :]
        decay = jnp.exp(last - acum)
        eac = jnp.exp(acum)
        tot = jnp.exp(last)
        ys, xrs = _ssd_chunk_diag(xbc, dt, acum, causal)
        xrd_t = jnp.concatenate([xrs[h] * decay[:, h:h + 1] for h in range(SSD_HEADS)], axis=1).T.astype(BF16)
        outs = []
        for h in range(SSD_HEADS):
            g = h // HPG
            bg = xbc[:, B_OFF + g * SSD_STATE:B_OFF + (g + 1) * SSD_STATE].astype(BF16)
            cg = xbc[:, C_OFF + g * SSD_STATE:C_OFF + (g + 1) * SSD_STATE].astype(BF16)
            prev = st_ref[h]
            y_off = lax.dot_general(cg, prev.astype(BF16), NT_DIMS, preferred_element_type=F32) * eac[:, h:h + 1]
            st = jnp.dot(xrd_t[h * SSD_HEAD_DIM:(h + 1) * SSD_HEAD_DIM, :], bg, preferred_element_type=F32)
            st_ref[h] = prev * tot[:, h:h + 1] + st
            outs.append(ys[h] + y_off)
        y = jnp.concatenate(outs, axis=1)
        out_ref[pl.ds(r0, SSD_CHUNK), :] = _ssd_finish(y, xbc[:, :SSD_WIDTH], z_ref[pl.ds(r0, SSD_CHUNK), :],
                                                       dskip_ref, norm_ref)
        return carry

    lax.fori_loop(0, tb // SSD_CHUNK, chunk, 0)
    state_ref[...] = st_ref[...]


def _ssd_prompt(xbc, z, dt, cw, cb, dtb, alog, dskip, norm, batch, tb):
    n = xbc.shape[0]
    nb = n // batch // tb
    blk = lambda width: pl.BlockSpec((tb, width), lambda b, i: (b * nb + i, 0))
    return pl.pallas_call(
        _ssd_prompt_kernel,
        grid=(batch, nb),
        in_specs=[blk(SSD_CONV_CH), blk(SSD_WIDTH), blk(LANES),
                  _const_spec((CONV_W, SSD_CONV_CH)), _const_spec((1, SSD_CONV_CH)),
                  _const_spec((1, LANES)), _const_spec((1, LANES)),
                  _const_spec((1, SSD_WIDTH)), _const_spec((1, SSD_WIDTH))],
        out_specs=[blk(SSD_WIDTH),
                   pl.BlockSpec((None, SSD_HEADS, SSD_HEAD_DIM, SSD_STATE), lambda b, i: (b, 0, 0, 0))],
        out_shape=[jax.ShapeDtypeStruct((n, SSD_WIDTH), F32),
                   jax.ShapeDtypeStruct((batch, SSD_HEADS, SSD_HEAD_DIM, SSD_STATE), F32)],
        scratch_shapes=[pltpu.VMEM((SUBLANES, SSD_CONV_CH), F32),
                        pltpu.VMEM((tb, SSD_CONV_CH), F32),
                        pltpu.VMEM((SSD_HEADS, SSD_HEAD_DIM, SSD_STATE), F32)],
        compiler_params=_cparams(("parallel", "arbitrary")),
        name="ssd_prompt",
    )(xbc, z, dt, cw, cb, dtb, alog, dskip, norm)


SSD_DEC_SEQ = SSD_CHUNK // DEC_T


def _ssd_decode_kernel(xbc_ref, z_ref, dt_ref, buf_ref, h0_ref, cw_ref, cb_ref, dtb_ref, alog_ref, dskip_ref,
                       norm_ref, out_ref, hnew_ref, xc_ref, xrd_ref, eac_ref, tot_ref, yoff_ref):
    xc = _silu(_conv_groups(xbc_ref[...], buf_ref[...], cw_ref, cb_ref))
    xc_ref[...] = xc
    a_neg = -jnp.exp(alog_ref[...])
    row = _row_index((SSD_CHUNK, LANES), DEC_T)
    li = lax.broadcasted_iota(jnp.int32, (SSD_CHUNK, SSD_CHUNK), 0)
    si = lax.broadcasted_iota(jnp.int32, (SSD_CHUNK, SSD_CHUNK), 1)
    same_seq_causal = jnp.logical_and(li >= si, (li - si) <= (li & (DEC_T - 1)))
    dt = _softplus(dt_ref[...] + dtb_ref[...])
    dta = dt * a_neg
    acum = _cumsum_rows(dta, row, 3)
    rest = _suffix_sum_rows(dta, row, DEC_T, 3)
    eac_ref[...] = jnp.exp(acum)
    tot_ref[...] = jnp.exp(acum + rest)
    decay = jnp.exp(rest)
    ys, xrs = _ssd_chunk_diag(xc, dt, acum, same_seq_causal)
    xrd_ref[...] = jnp.concatenate([xrs[h] * decay[:, h:h + 1] for h in range(SSD_HEADS)], axis=1).T
    seq_of_lane = lax.shift_right_logical(lax.broadcasted_iota(jnp.int32, (SSD_HEAD_DIM, SSD_CHUNK), 1), 3)
    b_all = [xc[:, B_OFF + g * SSD_STATE:B_OFF + (g + 1) * SSD_STATE].astype(BF16) for g in range(SSD_GROUPS)]

    def seq(b, carry):
        r0 = pl.multiple_of(b * DEC_T, DEC_T)
        rows = pl.ds(r0, DEC_T)
        xcb = xc_ref[rows, :]
        eac = eac_ref[rows, :]
        tot = tot_ref[rows, :]
        own = seq_of_lane == b
        outs = []
        for h in range(SSD_HEADS):
            g = h // HPG
            cg = xcb[:, C_OFF + g * SSD_STATE:C_OFF + (g + 1) * SSD_STATE].astype(BF16)
            prev = h0_ref[b, h]
            outs.append(lax.dot_general(cg, prev.astype(BF16), NT_DIMS, preferred_element_type=F32)
                        * eac[:, h:h + 1])
            lhs = jnp.where(own, xrd_ref[h * SSD_HEAD_DIM:(h + 1) * SSD_HEAD_DIM, :], 0.0).astype(BF16)
            st = jnp.dot(lhs, b_all[g], preferred_element_type=F32)
            hnew_ref[b, h] = prev * tot[0:1, h:h + 1] + st
        yoff_ref[rows, :] = jnp.concatenate(outs, axis=1)
        return carry

    lax.fori_loop(0, SSD_DEC_SEQ, seq, 0)
    y = jnp.concatenate(ys, axis=1) + yoff_ref[...]
    out_ref[...] = _ssd_finish(y, xc[:, :SSD_WIDTH], z_ref[...], dskip_ref, norm_ref)


def _ssd_decode(xbc, z, dt, bufpad, state, layer, cw, cb, dtb, alog, dskip, norm):
    n = xbc.shape[0]
    blk = lambda width: pl.BlockSpec((SSD_CHUNK, width), lambda i: (i, 0))
    st_in = pl.BlockSpec((None, SSD_DEC_SEQ, SSD_HEADS, SSD_HEAD_DIM, SSD_STATE), lambda i: (layer, i, 0, 0, 0))
    st_out = pl.BlockSpec((SSD_DEC_SEQ, SSD_HEADS, SSD_HEAD_DIM, SSD_STATE), lambda i: (i, 0, 0, 0))
    return pl.pallas_call(
        _ssd_decode_kernel,
        grid=(n // SSD_CHUNK,),
        in_specs=[blk(SSD_CONV_CH), blk(SSD_WIDTH), blk(LANES), blk(SSD_CONV_CH), st_in,
                  _const_spec((CONV_W, SSD_CONV_CH)), _const_spec((1, SSD_CONV_CH)),
                  _const_spec((1, LANES)), _const_spec((1, LANES)),
                  _const_spec((1, SSD_WIDTH)), _const_spec((1, SSD_WIDTH))],
        out_specs=[blk(SSD_WIDTH), st_out],
        out_shape=[jax.ShapeDtypeStruct((n, SSD_WIDTH), F32),
                   jax.ShapeDtypeStruct((n // DEC_T, SSD_HEADS, SSD_HEAD_DIM, SSD_STATE), F32)],
        scratch_shapes=[pltpu.VMEM((SSD_CHUNK, SSD_CONV_CH), F32),
                        pltpu.VMEM((SSD_WIDTH, SSD_CHUNK), F32),
                        pltpu.VMEM((SSD_CHUNK, LANES), F32),
                        pltpu.VMEM((SSD_CHUNK, LANES), F32),
                        pltpu.VMEM((SSD_CHUNK, SSD_WIDTH), F32)],
        compiler_params=_cparams(("parallel",)),
        name="ssd_decode",
    )(xbc, z, dt, bufpad, state, cw, cb, dtb, alog, dskip, norm)


DENSE_TM = 512
MIX_A_TB = 512
SSD_TB = 1024


def _pad_state_rows(buf):
    l, b, r, c = buf.shape
    return jnp.pad(buf, ((0, 0), (0, 0), (0, DEC_T - r), (0, 0))).reshape(l, b * DEC_T, c)


def kernel(x_prompt, x_sample, state_lru_h, state_lru_conv, cache_swa_k, cache_swa_v, state_ssd, state_ssd_conv,
           norm_mix_in, norm_mix_out, w_in, conv_a_w, conv_a_b, lru_wa, lru_ba, lru_wx, lru_bx, lru_lambda,
           conv_c_w, conv_c_b, dt_bias, a_log, d_skip, ssm_norm, w_out, norm_ffn_in, norm_ffn_out,
           w_gate_up, w_down):
    bp, seq, _ = x_prompt.shape
    bs, dec_t, _ = x_sample.shape
    assert dec_t == DEC_T and seq % ATT_BLK == 0 and bs % SSD_DEC_SEQ == 0

    w_in_b = jnp.pad(w_in, ((0, 0), (0, 0), (0, N_IN_PAD - N_IN))).astype(BF16)
    w_out_b = w_out.astype(BF16)
    w_gu_b = w_gate_up.astype(BF16)
    w_dn_b = w_down.astype(BF16)
    eye = jnp.eye(LRU_BLOCKS, dtype=F32)

    def block_diag(w):
        return (w[:, :, :, None, :] * eye[None, :, None, :, None]).reshape(
            DEPTH, LRU_WIDTH, LRU_WIDTH).astype(BF16)

    wa_bd = block_diag(lru_wa)
    wx_bd = block_diag(lru_wx)
    vec = lambda p: p[:, None, :]
    pad_lanes = lambda p: jnp.pad(p, ((0, 0), (0, LANES - p.shape[1])))[:, None, :]
    dtb_p = pad_lanes(dt_bias)
    alog_p = pad_lanes(a_log)
    dskip_p = jnp.repeat(d_skip, SSD_HEAD_DIM, axis=1)[:, None, :]
    buf_a = _pad_state_rows(state_lru_conv)
    buf_c = _pad_state_rows(state_ssd_conv)
    h0_a = jnp.pad(state_lru_h[:, :, None, :], ((0, 0), (0, 0), (0, DEC_T - 1), (0, 0))).reshape(
        DEPTH, bs * DEC_T, LRU_WIDTH)
    mult = jnp.asarray(_decode_key_multiplicity())
    cache_kt = jnp.transpose(cache_swa_k, (0, 1, 3, 4, 2))
    cache_vt = jnp.transpose(cache_swa_v, (0, 1, 3, 4, 2))

    yp = x_prompt.reshape(bp * seq, D_MODEL)
    ys = x_sample.reshape(bs * DEC_T, D_MODEL)
    p_new = [[] for _ in range(6)]
    s_new = [[] for _ in range(6)]
    for l in range(DEPTH):
        a_args = (conv_a_w[l], vec(conv_a_b)[l], wa_bd[l], vec(lru_ba)[l], wx_bd[l], vec(lru_bx)[l],
                  vec(lru_lambda)[l])
        c_args = (conv_c_w[l], vec(conv_c_b)[l], dtb_p[l], alog_p[l], dskip_p[l], vec(ssm_norm)[l])
        g_in, g_out = vec(norm_mix_in)[l], vec(norm_mix_out)[l]
        f_in, f_out = vec(norm_ffn_in)[l], vec(norm_ffn_out)[l]

        ga, xa, q, k, v, z, xbc, dt = _in_proj(yp, g_in, w_in_b[l], DENSE_TM)
        out_a, h_last = _mix_a_prompt(xa, ga, *a_args, batch=bp, tb=MIX_A_TB)
        out_b = _attn_prompt(q, k, v, bp)
        out_c, ssd_state = _ssd_prompt(xbc, z, dt, *c_args, batch=bp, tb=SSD_TB)
        yp = _out_proj(out_a, out_b, out_c, yp, w_out_b[l], g_out, DENSE_TM)
        yp = _ffn(yp, f_in, w_gu_b[l], w_dn_b[l], f_out, DENSE_TM)
        keep = min(MAX_WINDOW, seq)
        p_new[0].append(h_last.reshape(bp, LRU_WIDTH))
        p_new[1].append(xa.reshape(bp, seq, LRU_WIDTH)[:, seq - (CONV_W - 1):])
        p_new[2].append(k.reshape(bp, seq, ATT_HEADS, HEAD_DIM)[:, seq - keep:])
        p_new[3].append(v.reshape(bp, seq, ATT_HEADS, HEAD_DIM)[:, seq - keep:])
        p_new[4].append(ssd_state)
        p_new[5].append(xbc.reshape(bp, seq, SSD_CONV_CH)[:, seq - (CONV_W - 1):])

        ga, xa, q, k, v, z, xbc, dt = _in_proj(ys, g_in, w_in_b[l], DENSE_TM)
        out_a, h_all = _mix_a_decode(xa, ga, buf_a[l], h0_a[l], *a_args, tm=DENSE_TM)
        out_b = _attn_decode(q, k, v, cache_kt, cache_vt, l, mult)
        out_c, ssd_state = _ssd_decode(xbc, z, dt, buf_c[l], state_ssd, l, *c_args)
        ys = _out_proj(out_a, out_b, out_c, ys, w_out_b[l], g_out, DENSE_TM)
        ys = _ffn(ys, f_in, w_gu_b[l], w_dn_b[l], f_out, DENSE_TM)
        s_new[0].append(h_all.reshape(bs, DEC_T, LRU_WIDTH)[:, DEC_T - 1])
        s_new[1].append(xa.reshape(bs, DEC_T, LRU_WIDTH)[:, DEC_T - (CONV_W - 1):])
        s_new[2].append(k.reshape(bs, DEC_T, ATT_HEADS, HEAD_DIM))
        s_new[3].append(v.reshape(bs, DEC_T, ATT_HEADS, HEAD_DIM))
        s_new[4].append(ssd_state)
        s_new[5].append(xbc.reshape(bs, DEC_T, SSD_CONV_CH)[:, DEC_T - (CONV_W - 1):])

    outs_p = [jnp.stack(a) for a in p_new]
    outs_s = [jnp.stack(a) for a in s_new]
    return (yp.reshape(bp, seq, D_MODEL), ys.reshape(bs, DEC_T, D_MODEL), *outs_p, *outs_s)
```

```python
import functools

import numpy as np
import jax
import jax.numpy as jnp
from jax import lax
from jax.experimental import pallas as pl
from jax.experimental.pallas import tpu as pltpu

F32 = jnp.float32
BF16 = jnp.bfloat16

D_MODEL = 1024
DEPTH = 4
CONV_W = 4
HEAD_DIM = 64
ATT_WIDTH = 384
ATT_HEADS = 6
ATT_SPAN = 128
DILATIONS = (1, 4, 16)
MAX_WINDOW = 2048
ATT_SCALE = HEAD_DIM ** -0.5
SSD_WIDTH = 384
SSD_HEADS = 6
SSD_HEAD_DIM = 64
SSD_GROUPS = 2
SSD_STATE = 128
SSD_CHUNK = 128
SSD_CONV_CH = 896
LRU_WIDTH = 256
LRU_BLOCKS = 4
LRU_C = 8.0
D_FF = 2816
N_IN = 2950
EPS = 1e-6

LANES = 128
SUBLANES = 8
N_IN_PAD = 3072
VMEM_LIMIT = 56 * 1024 * 1024

W_IN_COLS = {"ga": (0, 256), "xa": (256, 256), "q": (512, 384), "k": (896, 384), "v": (1280, 384),
             "z": (1664, 384), "xbc": (2048, 896), "dt": (2944, 6)}
P_COLS = {"xbc": (0, 896), "dt": (896, 128), "ga": (1024, 256), "xa": (1280, 256), "q": (1536, 384),
          "k": (1920, 384), "v": (2304, 384), "z": (2688, 384)}


def _pcol(name, width=None):
    off, w = P_COLS[name]
    width = width or w
    assert off % width == 0
    return off // width

NEG_INF = float("-inf")
NT_DIMS = (((1,), (1,)), ((), ()))


def _cparams(sem):
    return pltpu.CompilerParams(dimension_semantics=sem, vmem_limit_bytes=VMEM_LIMIT)


def _const_spec(shape):
    nd = len(shape)
    return pl.BlockSpec(shape, lambda *_: (0,) * nd, pipeline_mode=pl.Buffered(1))


def _rms(x, g):
    ms = jnp.mean(x * x, axis=-1, keepdims=True)
    return x * lax.rsqrt(ms + EPS) * g


def _sigmoid(x):
    return jax.nn.sigmoid(x)


def _silu(x):
    return x * jax.nn.sigmoid(x)


def _softplus(x):
    return jnp.maximum(x, 0.0) + jnp.log1p(jnp.exp(-jnp.abs(x)))


def _gelu_tanh(x):
    c = np.sqrt(2.0 / np.pi).astype(np.float32)
    return 0.5 * x * (1.0 + jnp.tanh(c * (x + 0.044715 * (x * x * x))))


def _roll_rows(x, shift):
    n = x.shape[0]
    shift = shift % n
    if shift == 0:
        return x
    return pltpu.roll(x, shift, 0)


def _row_index(shape, group=None):
    r = lax.broadcasted_iota(jnp.int32, shape, 0)
    if group is not None:
        r = jnp.bitwise_and(r, group - 1)
    return r


def _lin_scan(a, u, row, steps):
    s = 1
    for _ in range(steps):
        keep = row >= s
        a_sh = jnp.where(keep, _roll_rows(a, s), 1.0)
        u_sh = jnp.where(keep, _roll_rows(u, s), 0.0)
        u = a * u_sh + u
        a = a * a_sh
        s *= 2
    return a, u


def _cumsum_rows(x, row, steps):
    s = 1
    for _ in range(steps):
        x = x + jnp.where(row >= s, _roll_rows(x, s), 0.0)
        s *= 2
    return x


def _suffix_sum_rows(x, row, group, steps):
    incl = x
    s = 1
    for _ in range(steps):
        incl = incl + jnp.where(row < group - s, _roll_rows(incl, -s), 0.0)
        s *= 2
    return incl - x


def _conv_taps(x, shifted_fn, w_ref, b_ref):
    y = b_ref[...] + w_ref[CONV_W - 1:CONV_W, :] * x
    for s in range(1, CONV_W):
        y = y + w_ref[CONV_W - 1 - s:CONV_W - s, :] * shifted_fn(s)
    return y


def _conv_block_carry(x, tail_ref, w_ref, b_ref):
    tb, c = x.shape
    x3 = x.reshape(tb // SUBLANES, SUBLANES, c)
    sub = lax.broadcasted_iota(jnp.int32, (1, SUBLANES, c), 1)
    tail = tail_ref[...]

    def shifted(s):
        r = pltpu.roll(x3, s, 1)
        before = jnp.concatenate([pltpu.roll(tail, s, 0)[None], r[:-1]], axis=0)
        return jnp.where(sub < s, before, r)

    y = _conv_taps(x3, shifted, w_ref, b_ref)
    tail_ref[...] = x[tb - SUBLANES:tb]
    return y.reshape(tb, c)


def _conv_groups(x, bufpad, w_ref, b_ref):
    row = _row_index(x.shape, SUBLANES)
    return _conv_taps(
        x, lambda s: jnp.where(row >= s, _roll_rows(x, s), _roll_rows(bufpad, s - (CONV_W - 1))), w_ref, b_ref)


def _lru_gates(xc, wa_ref, ba_ref, wx_ref, bx_ref, lam_ref):
    xb = xc.astype(BF16)
    r = _sigmoid(jnp.dot(xb, wa_ref[...], preferred_element_type=F32) + ba_ref[...])
    ig = _sigmoid(jnp.dot(xb, wx_ref[...], preferred_element_type=F32) + bx_ref[...])
    log_a = (-LRU_C) * r * _softplus(-lam_ref[...])
    a = jnp.exp(log_a)
    t = jnp.tanh(log_a)
    u = jnp.sqrt(-2.0 * t / (1.0 - t)) * (ig * xc)
    return a, u


def _in_proj_kernel(x_ref, g_ref, w_ref, p_ref, *tail_refs, first_tail):
    h = _rms(x_ref[...], g_ref[...]).astype(BF16)
    p_ref[...] = jnp.dot(h, w_ref[...], preferred_element_type=F32)
    if tail_refs:
        kt_ref, vt_ref = tail_refs

        @pl.when(pl.program_id(1) >= first_tail)
        def _():
            k_off, v_off = P_COLS["k"][0], P_COLS["v"][0]
            kt_ref[...] = p_ref[:, k_off:k_off + ATT_WIDTH].T
            vt_ref[...] = p_ref[:, v_off:v_off + ATT_WIDTH].T


def _in_proj(x, g, w, batch, tm, with_tail):
    n = x.shape[0]
    nt = n // batch // tm
    first_tail = nt - MAX_WINDOW // tm
    row = lambda width: pl.BlockSpec((tm, width), lambda b, j: (b * nt + j, 0))
    out_specs = [row(N_IN_PAD)]
    out_shape = [jax.ShapeDtypeStruct((n, N_IN_PAD), F32)]
    if with_tail:
        tail = pl.BlockSpec((None, ATT_WIDTH, tm), lambda b, j: (b, 0, jnp.maximum(j - first_tail, 0)))
        out_specs += [tail, tail]
        out_shape += [jax.ShapeDtypeStruct((batch, ATT_WIDTH, MAX_WINDOW), F32)] * 2
    return pl.pallas_call(
        functools.partial(_in_proj_kernel, first_tail=first_tail),
        grid=(batch, nt),
        in_specs=[row(D_MODEL), _const_spec((1, D_MODEL)), _const_spec((D_MODEL, N_IN_PAD))],
        out_specs=out_specs,
        out_shape=out_shape,
        compiler_params=_cparams(("parallel", "arbitrary")),
        name="in_proj",
    )(x, g, w)


FFN_CHUNK = 256


def _post_kernel(a_ref, b_ref, c_ref, x_ref, wo_ref, go_ref, gi_ref, wgu_ref, wd_ref, gf_ref, o_ref):
    mixed = jnp.concatenate([a_ref[...], b_ref[...], c_ref[...]], axis=1).astype(BF16)
    x1 = x_ref[...] + _rms(jnp.dot(mixed, wo_ref[...], preferred_element_type=F32), go_ref[...])
    h = _rms(x1, gi_ref[...]).astype(BF16)
    acc = jnp.zeros(x1.shape, F32)
    for c in range(D_FF // FFN_CHUNK):
        lo = c * FFN_CHUNK
        g = jnp.dot(h, wgu_ref[:, lo:lo + FFN_CHUNK], preferred_element_type=F32)
        u = jnp.dot(h, wgu_ref[:, D_FF + lo:D_FF + lo + FFN_CHUNK], preferred_element_type=F32)
        act = (_silu(g) * u).astype(BF16)
        acc = acc + jnp.dot(act, wd_ref[lo:lo + FFN_CHUNK, :], preferred_element_type=F32)
    o_ref[...] = x1 + _rms(acc, gf_ref[...])


def _post(a, b, c, x, wo, go, gi, wgu, wd, gf, tm):
    n = x.shape[0]
    row = lambda width: pl.BlockSpec((tm, width), lambda i: (i, 0))
    vec = _const_spec((1, D_MODEL))
    return pl.pallas_call(
        _post_kernel,
        grid=(n // tm,),
        in_specs=[row(LRU_WIDTH), row(ATT_WIDTH), row(SSD_WIDTH), row(D_MODEL),
                  _const_spec((D_MODEL, D_MODEL)), vec, vec,
                  _const_spec((D_MODEL, 2 * D_FF)), _const_spec((D_FF, D_MODEL)), vec],
        out_specs=row(D_MODEL),
        out_shape=jax.ShapeDtypeStruct((n, D_MODEL), F32),
        compiler_params=_cparams(("parallel",)),
        name="post",
    )(a, b, c, x, wo, go, gi, wgu, wd, gf)


def _mix_a_prompt_kernel(xa_ref, ga_ref, cw_ref, cb_ref, wa_ref, ba_ref, wx_ref, bx_ref, lam_ref,
                         out_ref, hlast_ref, tail_ref, hc_ref):
    @pl.when(pl.program_id(1) == 0)
    def _():
        tail_ref[...] = jnp.zeros_like(tail_ref)
        hc_ref[...] = jnp.zeros_like(hc_ref)

    x = xa_ref[...]
    tb = x.shape[0]
    xc = _conv_block_carry(x, tail_ref, cw_ref, cb_ref)
    a, u = _lru_gates(xc, wa_ref, ba_ref, wx_ref, bx_ref, lam_ref)
    row = _row_index(x.shape)
    a_cum, h = _lin_scan(a, u, row, int(np.log2(tb)))
    h = h + a_cum * hc_ref[0:1, :]
    last = h[tb - 1:tb, :]
    hc_ref[...] = jnp.broadcast_to(last, hc_ref.shape)
    hlast_ref[...] = last
    out_ref[...] = h * _gelu_tanh(ga_ref[...])


def _mix_a_prompt(p, cw, cb, wa, ba, wx, bx, lam, batch, tb):
    n = p.shape[0]
    nb = n // batch // tb
    blk = pl.BlockSpec((tb, LRU_WIDTH), lambda b, i: (b * nb + i, 0))
    col = lambda name: pl.BlockSpec((tb, LRU_WIDTH), lambda b, i: (b * nb + i, _pcol(name)))
    vec = _const_spec((1, LRU_WIDTH))
    mat = _const_spec((LRU_WIDTH, LRU_WIDTH))
    return pl.pallas_call(
        _mix_a_prompt_kernel,
        grid=(batch, nb),
        in_specs=[col("xa"), col("ga"), _const_spec((CONV_W, LRU_WIDTH)), vec, mat, vec, mat, vec, vec],
        out_specs=[blk, pl.BlockSpec((None, 1, LRU_WIDTH), lambda b, i: (b, 0, 0))],
        out_shape=[jax.ShapeDtypeStruct((n, LRU_WIDTH), F32),
                   jax.ShapeDtypeStruct((batch, 1, LRU_WIDTH), F32)],
        scratch_shapes=[pltpu.VMEM((SUBLANES, LRU_WIDTH), F32), pltpu.VMEM((SUBLANES, LRU_WIDTH), F32)],
        compiler_params=_cparams(("parallel", "arbitrary")),
        name="mix_a_prompt",
    )(p, p, cw, cb, wa, ba, wx, bx, lam)


def _mix_a_decode_kernel(xa_ref, ga_ref, buf_ref, h0_ref, cw_ref, cb_ref, wa_ref, ba_ref, wx_ref, bx_ref,
                         lam_ref, out_ref, h_ref):
    x = xa_ref[...]
    xc = _conv_groups(x, buf_ref[...], cw_ref, cb_ref)
    a, u = _lru_gates(xc, wa_ref, ba_ref, wx_ref, bx_ref, lam_ref)
    u = u + a * h0_ref[...]
    row = _row_index(x.shape, SUBLANES)
    _, h = _lin_scan(a, u, row, 3)
    h_ref[...] = h
    out_ref[...] = h * _gelu_tanh(ga_ref[...])


def _mix_a_decode(p, bufpad, h0pad, cw, cb, wa, ba, wx, bx, lam, tm):
    n = p.shape[0]
    blk = pl.BlockSpec((tm, LRU_WIDTH), lambda i: (i, 0))
    col = lambda name: pl.BlockSpec((tm, LRU_WIDTH), lambda i: (i, _pcol(name)))
    vec = _const_spec((1, LRU_WIDTH))
    mat = _const_spec((LRU_WIDTH, LRU_WIDTH))
    return pl.pallas_call(
        _mix_a_decode_kernel,
        grid=(n // tm,),
        in_specs=[col("xa"), col("ga"), blk, blk, _const_spec((CONV_W, LRU_WIDTH)), vec, mat, vec, mat, vec, vec],
        out_specs=[blk, blk],
        out_shape=[jax.ShapeDtypeStruct((n, LRU_WIDTH), F32), jax.ShapeDtypeStruct((n, LRU_WIDTH), F32)],
        compiler_params=_cparams(("parallel",)),
        name="mix_a_decode",
    )(p, p, bufpad, h0pad, cw, cb, wa, ba, wx, bx, lam)


ATT_BLK = 2048
ATT_UNIT = 128
ATT_UNROLL = 4


def _attn_unit(q_t, kp_t, kc_t, vp_t, vc_t, bias, lane_lo):
    zero = jnp.zeros_like(q_t)
    qs = q_t * ATT_SCALE
    qq = jnp.concatenate([jnp.where(lane_lo, qs, zero), jnp.where(lane_lo, zero, qs)], axis=0).astype(BF16)
    kk = jnp.concatenate([kp_t, kc_t], axis=0).astype(BF16)
    s = lax.dot_general(qq, kk, (((1,), (1,)), ((), ())), preferred_element_type=F32) + bias
    m = jnp.max(s, axis=1, keepdims=True)
    p = jnp.exp(s - m).astype(BF16)
    pcat = jnp.concatenate([p[:ATT_UNIT], p[ATT_UNIT:]], axis=1)
    one_lo = jnp.where(lane_lo, 1.0, 0.0)
    one_hi = 1.0 - one_lo
    w = jnp.concatenate([
        jnp.concatenate([jnp.where(lane_lo, vp_t, zero), one_lo], axis=1),
        jnp.concatenate([jnp.where(lane_lo, vc_t, zero), one_lo], axis=1),
        jnp.concatenate([jnp.where(lane_lo, zero, vp_t), one_hi], axis=1),
        jnp.concatenate([jnp.where(lane_lo, zero, vc_t), one_hi], axis=1)], axis=0).astype(BF16)
    ol = jnp.dot(pcat, w, preferred_element_type=F32)
    m_t = jnp.where(lane_lo, jnp.broadcast_to(m[:ATT_UNIT], q_t.shape), jnp.broadcast_to(m[ATT_UNIT:], q_t.shape))
    return ol[:, :LANES], m_t, ol[:, LANES:]


def _attn_prompt_kernel(q_ref, kp_ref, kc_ref, vp_ref, vc_ref, o_ref, acc_ref, m_ref, l_ref):
    first_block = pl.program_id(2) == 0
    lane_lo = lax.broadcasted_iota(jnp.int32, (ATT_UNIT, LANES), 1) < HEAD_DIM
    qi = lax.broadcasted_iota(jnp.int32, (2 * ATT_UNIT, 2 * ATT_UNIT), 0) & (ATT_UNIT - 1)
    ki = lax.broadcasted_iota(jnp.int32, (2 * ATT_UNIT, 2 * ATT_UNIT), 1)
    is_prev = ki < ATT_UNIT
    dist = qi - ki + ATT_UNIT
    bias = jnp.where(dist >= 0, jnp.where(dist <= ATT_SPAN, 0.0, NEG_INF), NEG_INF)
    bias_first = bias + jnp.where(is_prev, jnp.where(first_block, NEG_INF, 0.0), 0.0)

    def rows(start, d):
        if d == 1:
            return pl.ds(start, ATT_UNIT)
        return pl.ds(start, ATT_UNIT, stride=d)

    def keep(bi, sl, o_t, m_t, l_t):
        acc_ref[bi, sl, :] = o_t
        m_ref[bi, sl, :] = m_t
        l_ref[bi, sl, :] = l_t

    for bi, d in enumerate(DILATIONS):
        nj = ATT_BLK // (ATT_UNIT * d)

        def head_unit(rho, carry, bi=bi, d=d, nj=nj):
            cur = rows(rho, d)
            prev = rows(rho + d * ATT_UNIT * (nj - 1), d)
            keep(bi, cur, *_attn_unit(q_ref[cur, :], kp_ref[prev, :], kc_ref[cur, :], vp_ref[prev, :],
                                      vc_ref[cur, :], bias_first, lane_lo))
            return carry

        def inner_unit(idx, carry, bi=bi, d=d):
            rho = idx & (d - 1)
            j = 1 + lax.shift_right_logical(idx, int(np.log2(d)))
            start = rho + d * ATT_UNIT * j
            cur = rows(start, d)
            prev = rows(start - d * ATT_UNIT, d)
            keep(bi, cur, *_attn_unit(q_ref[cur, :], kc_ref[prev, :], kc_ref[cur, :], vc_ref[prev, :],
                                      vc_ref[cur, :], bias, lane_lo))
            return carry

        lax.fori_loop(0, d, head_unit, 0, unroll=min(d, ATT_UNROLL))
        n_inner = d * (nj - 1)
        if n_inner:
            lax.fori_loop(0, n_inner, inner_unit, 0, unroll=ATT_UNROLL if n_inner % ATT_UNROLL == 0 else 3)

    def combine(c, carry):
        sl = pl.ds(pl.multiple_of(c * ATT_UNIT, ATT_UNIT), ATT_UNIT)
        ms = [m_ref[bi, sl, :] for bi in range(len(DILATIONS))]
        m = functools.reduce(jnp.maximum, ms)
        ws = [jnp.exp(mi - m) for mi in ms]
        num = sum(w * acc_ref[bi, sl, :] for bi, w in enumerate(ws))
        den = sum(w * l_ref[bi, sl, :] for bi, w in enumerate(ws))
        o_ref[sl, :] = num / den
        return carry

    lax.fori_loop(0, ATT_BLK // ATT_UNIT, combine, 0, unroll=2)


def _attn_prompt(p, batch):
    n = p.shape[0]
    nb = n // batch // ATT_BLK
    npair = ATT_WIDTH // LANES
    out = pl.BlockSpec((ATT_BLK, LANES), lambda b, hp, i: (b * nb + i, hp))
    cur = lambda name: pl.BlockSpec((ATT_BLK, LANES), lambda b, hp, i: (b * nb + i, _pcol(name, LANES) + hp))
    prev = lambda name: pl.BlockSpec((ATT_BLK, LANES),
                                     lambda b, hp, i: (b * nb + jnp.maximum(i - 1, 0), _pcol(name, LANES) + hp))
    scratch = pltpu.VMEM((len(DILATIONS), ATT_BLK, LANES), F32)
    return pl.pallas_call(
        _attn_prompt_kernel,
        grid=(batch, npair, nb),
        in_specs=[cur("q"), prev("k"), cur("k"), prev("v"), cur("v")],
        out_specs=out,
        out_shape=jax.ShapeDtypeStruct((n, ATT_WIDTH), F32),
        scratch_shapes=[scratch, scratch, scratch],
        compiler_params=_cparams(("parallel", "parallel", "parallel")),
        name="attn_prompt",
    )(p, p, p, p, p)


DEC_T = 8
DEC_NEW_PAD = 128
DEC_KEYS = MAX_WINDOW + DEC_NEW_PAD
DEC_ROWS = ATT_HEADS * DEC_T
DEC_BT = 2


def _decode_key_multiplicity():
    pos = np.full((DEC_KEYS,), -10 ** 9, np.int64)
    pos[:MAX_WINDOW] = np.arange(MAX_WINDOW)
    pos[MAX_WINDOW:MAX_WINDOW + DEC_T] = MAX_WINDOW + np.arange(DEC_T)
    t = np.arange(DEC_T)
    dist = (MAX_WINDOW + t)[:, None] - pos[None, :]
    cnt = np.zeros(dist.shape, np.float32)
    for d in DILATIONS:
        cnt += ((dist >= 0) & (dist % d == 0) & (dist <= ATT_SPAN * d)).astype(np.float32)
    return np.tile(cnt, (ATT_HEADS, 1))


def _attn_decode_kernel(q_ref, kn_ref, vn_ref, kt_ref, vt_ref, mult_ref, o_ref):
    lane = lax.broadcasted_iota(jnp.int32, (DEC_T, ATT_WIDTH), 1)
    head_masks = [jnp.logical_and(lane >= HEAD_DIM * h, lane < HEAD_DIM * (h + 1)) for h in range(ATT_HEADS)]
    mult = mult_ref[...]
    seen = mult > 0.0
    zpad = jnp.zeros((DEC_NEW_PAD - DEC_T, ATT_WIDTH), F32)
    for b in range(DEC_BT):
        new_rows = slice(DEC_T * b, DEC_T * (b + 1))
        qb = q_ref[new_rows, :] * ATT_SCALE
        qbd = jnp.concatenate([jnp.where(hm, qb, 0.0) for hm in head_masks], axis=0).astype(BF16)
        kt = kt_ref[b].reshape(ATT_WIDTH, MAX_WINDOW).astype(BF16)
        kn = jnp.concatenate([kn_ref[new_rows, :], zpad], axis=0).astype(BF16)
        s = jnp.concatenate([jnp.dot(qbd, kt, preferred_element_type=F32),
                             lax.dot_general(qbd, kn, NT_DIMS, preferred_element_type=F32)], axis=1)
        s = jnp.where(seen, s, NEG_INF)
        m = jnp.max(s, axis=1, keepdims=True)
        p = mult * jnp.exp(s - m)
        p = (p / jnp.sum(p, axis=1, keepdims=True)).astype(BF16)
        vt = vt_ref[b].reshape(ATT_WIDTH, MAX_WINDOW).astype(BF16)
        vn = jnp.concatenate([vn_ref[new_rows, :], zpad], axis=0).astype(BF16)
        o = (lax.dot_general(p[:, :MAX_WINDOW], vt, NT_DIMS, preferred_element_type=F32)
             + jnp.dot(p[:, MAX_WINDOW:], vn, preferred_element_type=F32))
        out = jnp.zeros((DEC_T, ATT_WIDTH), F32)
        for h, hm in enumerate(head_masks):
            out = out + jnp.where(hm, o[DEC_T * h:DEC_T * (h + 1), :], 0.0)
        o_ref[new_rows, :] = out


def _attn_decode(p, cache_kt, cache_vt, layer, mult):
    n = p.shape[0]
    nseq = n // DEC_T
    new = pl.BlockSpec((DEC_T * DEC_BT, ATT_WIDTH), lambda g: (g, 0))
    col = lambda name: pl.BlockSpec((DEC_T * DEC_BT, ATT_WIDTH), lambda g: (g, _pcol(name)))
    win = pl.BlockSpec((None, DEC_BT, ATT_HEADS, HEAD_DIM, MAX_WINDOW), lambda g: (layer, g, 0, 0, 0))
    return pl.pallas_call(
        _attn_decode_kernel,
        grid=(nseq // DEC_BT,),
        in_specs=[col("q"), col("k"), col("v"), win, win, _const_spec((DEC_ROWS, DEC_KEYS))],
        out_specs=new,
        out_shape=jax.ShapeDtypeStruct((n, ATT_WIDTH), F32),
        compiler_params=_cparams(("parallel",)),
        name="attn_decode",
    )(p, p, p, cache_kt, cache_vt, mult)


HPG = SSD_HEADS // SSD_GROUPS
B_OFF = SSD_WIDTH
C_OFF = SSD_WIDTH + SSD_GROUPS * SSD_STATE


def _ssd_chunk_diag(xbc, dt, acum, pair_ok):
    acum_t = acum.T
    ys, xrs = [], []
    for g in range(SSD_GROUPS):
        bg = xbc[:, B_OFF + g * SSD_STATE:B_OFF + (g + 1) * SSD_STATE].astype(BF16)
        cg = xbc[:, C_OFF + g * SSD_STATE:C_OFF + (g + 1) * SSD_STATE].astype(BF16)
        cb = lax.dot_general(cg, bg, NT_DIMS, preferred_element_type=F32)
        for h in range(g * HPG, (g + 1) * HPG):
            xr = xbc[:, h * SSD_HEAD_DIM:(h + 1) * SSD_HEAD_DIM] * dt[:, h:h + 1]
            diff = acum[:, h:h + 1] - acum_t[h:h + 1, :]
            lmat = jnp.exp(jnp.where(pair_ok, diff, NEG_INF))
            ys.append(jnp.dot((cb * lmat).astype(BF16), xr.astype(BF16), preferred_element_type=F32))
            xrs.append(xr)
    return ys, xrs


def _ssd_finish(y, xs, z, dskip_ref, norm_ref):
    y = y + dskip_ref[...] * xs
    y = y * _silu(z)
    return _rms(y, norm_ref[...])


DT_COPIES = 3
SSD_PAIRS = SSD_HEADS // 2
SSD_CONV_ROWS = 128


def _ssd_prompt_kernel(xbc_ref, z_ref, dt_ref, cw_ref, cb_ref, dtb_ref, alog_ref, dskip_ref, norm_ref,
                       out_ref, state_ref, tail_ref, xc_ref, st_ref):
    tb = xbc_ref.shape[0]

    @pl.when(pl.program_id(1) == 0)
    def _():
        tail_ref[...] = jnp.zeros_like(tail_ref)
        st_ref[...] = jnp.zeros_like(st_ref)

    for r0 in range(0, tb, SSD_CONV_ROWS):
        xc_ref[r0:r0 + SSD_CONV_ROWS, :] = _silu(
            _conv_block_carry(xbc_ref[r0:r0 + SSD_CONV_ROWS, :], tail_ref, cw_ref, cb_ref))
    a_neg = -jnp.exp(alog_ref[...])
    row = _row_index((SSD_CHUNK, LANES))
    lane = lax.broadcasted_iota(jnp.int32, (SSD_CHUNK, LANES), 1)
    lane_lo = lane < SSD_HEAD_DIM
    causal = row >= lane
    zero = jnp.zeros((SSD_CHUNK, LANES), F32)

    def pair_rows(t):
        return jnp.concatenate([jnp.where(lane_lo, t, zero), jnp.where(lane_lo, zero, t)], axis=0).astype(BF16)

    def chunk(c, carry):
        r0 = pl.multiple_of(c * SSD_CHUNK, SSD_CHUNK)
        rows = pl.ds(r0, SSD_CHUNK)
        xc = xc_ref[rows, :]
        dt = _softplus(dt_ref[rows, :] + dtb_ref[...])
        acum = _cumsum_rows(dt * a_neg, row, 7)
        last = acum[SSD_CHUNK - 1:SSD_CHUNK, :]
        tot = jnp.exp(last)
        pt = jnp.where(lane < SUBLANES, acum, jnp.where(lane < 2 * SUBLANES, dt, dt * jnp.exp(last - acum))).T
        bs = [xc[:, B_OFF + g * SSD_STATE:B_OFF + (g + 1) * SSD_STATE] for g in range(SSD_GROUPS)]
        cs = [xc[:, C_OFF + g * SSD_STATE:C_OFF + (g + 1) * SSD_STATE].astype(BF16) for g in range(SSD_GROUPS)]
        bts = [b.T for b in bs]
        cbs = [lax.dot_general(cs[g], bs[g].astype(BF16), NT_DIMS, preferred_element_type=F32)
               for g in range(SSD_GROUPS)]
        gs, eacs, btws = [], [], []
        for h in range(SSD_HEADS):
            g = h // HPG
            a_col = jnp.broadcast_to(acum[:, h:h + 1], (SSD_CHUNK, SSD_CHUNK))
            lmat = jnp.exp(jnp.where(causal, a_col - pt[h:h + 1, :], NEG_INF))
            gs.append((cbs[g] * lmat * pt[SUBLANES + h:SUBLANES + h + 1, :]).astype(BF16))
            eacs.append(jnp.exp(a_col))
            btws.append((bts[g] * pt[2 * SUBLANES + h:2 * SUBLANES + h + 1, :]).astype(BF16))
        outs = []
        for k in range(SSD_PAIRS):
            h0, h1 = 2 * k, 2 * k + 1
            g0, g1 = h0 // HPG, h1 // HPG
            x2 = pair_rows(xc[:, k * LANES:(k + 1) * LANES])
            y = jnp.dot(jnp.concatenate([gs[h0], gs[h1]], axis=1), x2, preferred_element_type=F32)
            st = st_ref[k]
            if g0 == g1:
                y_off = jnp.dot(cs[g0], st.astype(BF16), preferred_element_type=F32)
            else:
                y_off = jnp.dot(jnp.concatenate([cs[g0], cs[g1]], axis=1), pair_rows(st),
                                preferred_element_type=F32)
            outs.append(y + y_off * jnp.where(lane_lo, eacs[h0], eacs[h1]))
            upd = jnp.dot(jnp.concatenate([btws[h0], btws[h1]], axis=1), x2, preferred_element_type=F32)
            st_ref[k] = st * jnp.where(lane_lo, tot[:, h0:h0 + 1], tot[:, h1:h1 + 1]) + upd
        y = jnp.concatenate(outs, axis=1)
        out_ref[rows, :] = _ssd_finish(y, xc[:, :SSD_WIDTH], z_ref[rows, :], dskip_ref, norm_ref)
        return carry

    lax.fori_loop(0, tb // SSD_CHUNK, chunk, 0)
    for k in range(SSD_PAIRS):
        t = st_ref[k].T
        state_ref[2 * k] = t[:SSD_HEAD_DIM]
        state_ref[2 * k + 1] = t[SSD_HEAD_DIM:]


def _ssd_prompt(p, cw, cb, dtb, alog, dskip, norm, batch, tb):
    n = p.shape[0]
    nb = n // batch // tb
    blk = lambda width: pl.BlockSpec((tb, width), lambda b, i: (b * nb + i, 0))
    col = lambda name: pl.BlockSpec((tb, P_COLS[name][1]), lambda b, i: (b * nb + i, _pcol(name)))
    return pl.pallas_call(
        _ssd_prompt_kernel,
        grid=(batch, nb),
        in_specs=[col("xbc"), col("z"), col("dt"),
                  _const_spec((CONV_W, SSD_CONV_CH)), _const_spec((1, SSD_CONV_CH)),
                  _const_spec((1, LANES)), _const_spec((1, LANES)),
                  _const_spec((1, SSD_WIDTH)), _const_spec((1, SSD_WIDTH))],
        out_specs=[blk(SSD_WIDTH),
                   pl.BlockSpec((None, SSD_HEADS, SSD_HEAD_DIM, SSD_STATE), lambda b, i: (b, 0, 0, 0))],
        out_shape=[jax.ShapeDtypeStruct((n, SSD_WIDTH), F32),
                   jax.ShapeDtypeStruct((batch, SSD_HEADS, SSD_HEAD_DIM, SSD_STATE), F32)],
        scratch_shapes=[pltpu.VMEM((SUBLANES, SSD_CONV_CH), F32),
                        pltpu.VMEM((tb, SSD_CONV_CH), F32),
                        pltpu.VMEM((SSD_PAIRS, SSD_STATE, LANES), F32)],
        compiler_params=_cparams(("parallel", "arbitrary")),
        name="ssd_prompt",
    )(p, p, p, cw, cb, dtb, alog, dskip, norm)


SSD_DEC_SEQ = SSD_CHUNK // DEC_T


def _ssd_decode_kernel(xbc_ref, z_ref, dt_ref, buf_ref, h0_ref, cw_ref, cb_ref, dtb_ref, alog_ref, dskip_ref,
                       norm_ref, out_ref, hnew_ref, xc_ref, xrd_ref, eac_ref, tot_ref, yoff_ref):
    xc = _silu(_conv_groups(xbc_ref[...], buf_ref[...], cw_ref, cb_ref))
    xc_ref[...] = xc
    a_neg = -jnp.exp(alog_ref[...])
    row = _row_index((SSD_CHUNK, LANES), DEC_T)
    li = lax.broadcasted_iota(jnp.int32, (SSD_CHUNK, SSD_CHUNK), 0)
    si = lax.broadcasted_iota(jnp.int32, (SSD_CHUNK, SSD_CHUNK), 1)
    same_seq_causal = jnp.logical_and(li >= si, (li - si) <= (li & (DEC_T - 1)))
    dt = _softplus(dt_ref[...] + dtb_ref[...])
    dta = dt * a_neg
    acum = _cumsum_rows(dta, row, 3)
    rest = _suffix_sum_rows(dta, row, DEC_T, 3)
    eac_ref[...] = jnp.exp(acum)
    tot_ref[...] = jnp.exp(acum + rest)
    decay = jnp.exp(rest)
    ys, xrs = _ssd_chunk_diag(xc, dt, acum, same_seq_causal)
    xrd_ref[...] = jnp.concatenate([xrs[h] * decay[:, h:h + 1] for h in range(SSD_HEADS)], axis=1).T
    seq_of_lane = lax.shift_right_logical(lax.broadcasted_iota(jnp.int32, (SSD_HEAD_DIM, SSD_CHUNK), 1), 3)
    b_all = [xc[:, B_OFF + g * SSD_STATE:B_OFF + (g + 1) * SSD_STATE].astype(BF16) for g in range(SSD_GROUPS)]

    def seq(b, carry):
        r0 = pl.multiple_of(b * DEC_T, DEC_T)
        rows = pl.ds(r0, DEC_T)
        xcb = xc_ref[rows, :]
        eac = eac_ref[rows, :]
        tot = tot_ref[rows, :]
        own = seq_of_lane == b
        outs = []
        for h in range(SSD_HEADS):
            g = h // HPG
            cg = xcb[:, C_OFF + g * SSD_STATE:C_OFF + (g + 1) * SSD_STATE].astype(BF16)
            prev = h0_ref[b, h]
            outs.append(lax.dot_general(cg, prev.astype(BF16), NT_DIMS, preferred_element_type=F32)
                        * eac[:, h:h + 1])
            lhs = jnp.where(own, xrd_ref[h * SSD_HEAD_DIM:(h + 1) * SSD_HEAD_DIM, :], 0.0).astype(BF16)
            st = jnp.dot(lhs, b_all[g], preferred_element_type=F32)
            hnew_ref[b, h] = prev * tot[0:1, h:h + 1] + st
        yoff_ref[rows, :] = jnp.concatenate(outs, axis=1)
        return carry

    lax.fori_loop(0, SSD_DEC_SEQ, seq, 0)
    y = jnp.concatenate(ys, axis=1) + yoff_ref[...]
    out_ref[...] = _ssd_finish(y, xc[:, :SSD_WIDTH], z_ref[...], dskip_ref, norm_ref)


def _ssd_decode(p, bufpad, state, layer, cw, cb, dtb, alog, dskip, norm):
    n = p.shape[0]
    blk = lambda width: pl.BlockSpec((SSD_CHUNK, width), lambda i: (i, 0))
    col = lambda name: pl.BlockSpec((SSD_CHUNK, P_COLS[name][1]), lambda i: (i, _pcol(name)))
    st_in = pl.BlockSpec((None, SSD_DEC_SEQ, SSD_HEADS, SSD_HEAD_DIM, SSD_STATE), lambda i: (layer, i, 0, 0, 0))
    st_out = pl.BlockSpec((SSD_DEC_SEQ, SSD_HEADS, SSD_HEAD_DIM, SSD_STATE), lambda i: (i, 0, 0, 0))
    return pl.pallas_call(
        _ssd_decode_kernel,
        grid=(n // SSD_CHUNK,),
        in_specs=[col("xbc"), col("z"), col("dt"), blk(SSD_CONV_CH), st_in,
                  _const_spec((CONV_W, SSD_CONV_CH)), _const_spec((1, SSD_CONV_CH)),
                  _const_spec((1, LANES)), _const_spec((1, LANES)),
                  _const_spec((1, SSD_WIDTH)), _const_spec((1, SSD_WIDTH))],
        out_specs=[blk(SSD_WIDTH), st_out],
        out_shape=[jax.ShapeDtypeStruct((n, SSD_WIDTH), F32),
                   jax.ShapeDtypeStruct((n // DEC_T, SSD_HEADS, SSD_HEAD_DIM, SSD_STATE), F32)],
        scratch_shapes=[pltpu.VMEM((SSD_CHUNK, SSD_CONV_CH), F32),
                        pltpu.VMEM((SSD_WIDTH, SSD_CHUNK), F32),
                        pltpu.VMEM((SSD_CHUNK, LANES), F32),
                        pltpu.VMEM((SSD_CHUNK, LANES), F32),
                        pltpu.VMEM((SSD_CHUNK, SSD_WIDTH), F32)],
        compiler_params=_cparams(("parallel",)),
        name="ssd_decode",
    )(p, p, p, bufpad, state, cw, cb, dtb, alog, dskip, norm)


DENSE_TM = 512
MIX_A_TB = 512
SSD_TB = 1024


def _pad_state_rows(buf):
    l, b, r, c = buf.shape
    return jnp.pad(buf, ((0, 0), (0, 0), (0, DEC_T - r), (0, 0))).reshape(l, b * DEC_T, c)


def kernel(x_prompt, x_sample, state_lru_h, state_lru_conv, cache_swa_k, cache_swa_v, state_ssd, state_ssd_conv,
           norm_mix_in, norm_mix_out, w_in, conv_a_w, conv_a_b, lru_wa, lru_ba, lru_wx, lru_bx, lru_lambda,
           conv_c_w, conv_c_b, dt_bias, a_log, d_skip, ssm_norm, w_out, norm_ffn_in, norm_ffn_out,
           w_gate_up, w_down):
    bp, seq, _ = x_prompt.shape
    bs, dec_t, _ = x_sample.shape
    assert dec_t == DEC_T and seq % ATT_BLK == 0 and seq >= MAX_WINDOW and bs % SSD_DEC_SEQ == 0

    def dt_lanes(p):
        slot = jnp.pad(p, [(0, 0)] * (p.ndim - 1) + [(0, SUBLANES - SSD_HEADS)])
        rep = jnp.concatenate([slot] * DT_COPIES, axis=-1)
        return jnp.pad(rep, [(0, 0)] * (p.ndim - 1) + [(0, LANES - DT_COPIES * SUBLANES)])

    def w_cols(name):
        off, width = W_IN_COLS[name]
        blk = w_in[:, :, off:off + width]
        if name == "dt":
            return dt_lanes(blk)
        return jnp.pad(blk, ((0, 0), (0, 0), (0, P_COLS[name][1] - width)))

    w_in_b = jnp.concatenate([w_cols(name) for name in P_COLS], axis=2).astype(BF16)
    w_out_b = w_out.astype(BF16)
    w_gu_b = w_gate_up.astype(BF16)
    w_dn_b = w_down.astype(BF16)
    eye = jnp.eye(LRU_BLOCKS, dtype=F32)

    def block_diag(w):
        return (w[:, :, :, None, :] * eye[None, :, None, :, None]).reshape(
            DEPTH, LRU_WIDTH, LRU_WIDTH).astype(BF16)

    wa_bd = block_diag(lru_wa)
    wx_bd = block_diag(lru_wx)
    vec = lambda p: p[:, None, :]
    dtb_p = dt_lanes(dt_bias)[:, None, :]
    alog_p = dt_lanes(a_log)[:, None, :]
    dskip_p = jnp.repeat(d_skip, SSD_HEAD_DIM, axis=1)[:, None, :]
    buf_a = _pad_state_rows(state_lru_conv)
    buf_c = _pad_state_rows(state_ssd_conv)
    h0_a = jnp.pad(state_lru_h[:, :, None, :], ((0, 0), (0, 0), (0, DEC_T - 1), (0, 0))).reshape(
        DEPTH, bs * DEC_T, LRU_WIDTH)
    mult = jnp.asarray(_decode_key_multiplicity())
    cache_kt = jnp.transpose(cache_swa_k, (0, 1, 3, 4, 2))
    cache_vt = jnp.transpose(cache_swa_v, (0, 1, 3, 4, 2))

    yp = x_prompt.reshape(bp * seq, D_MODEL)
    ys = x_sample.reshape(bs * DEC_T, D_MODEL)
    p_new = [[] for _ in range(6)]
    s_new = [[] for _ in range(6)]
    for l in range(DEPTH):
        a_args = (conv_a_w[l], vec(conv_a_b)[l], wa_bd[l], vec(lru_ba)[l], wx_bd[l], vec(lru_bx)[l],
                  vec(lru_lambda)[l])
        c_args = (conv_c_w[l], vec(conv_c_b)[l], dtb_p[l], alog_p[l], dskip_p[l], vec(ssm_norm)[l])
        g_in, g_out = vec(norm_mix_in)[l], vec(norm_mix_out)[l]
        f_in, f_out = vec(norm_ffn_in)[l], vec(norm_ffn_out)[l]

        post_w = (w_out_b[l], g_out, f_in, w_gu_b[l], w_dn_b[l], f_out)

        pp, k_tail, v_tail = _in_proj(yp, g_in, w_in_b[l], bp, DENSE_TM, with_tail=True)
        out_a, h_last = _mix_a_prompt(pp, *a_args, batch=bp, tb=MIX_A_TB)
        out_b = _attn_prompt(pp, bp)
        out_c, ssd_state = _ssd_prompt(pp, *c_args, batch=bp, tb=SSD_TB)
        yp = _post(out_a, out_b, out_c, yp, *post_w, tm=DENSE_TM)
        pp3 = pp.reshape(bp, seq, N_IN_PAD)
        last = slice(seq - (CONV_W - 1), seq)
        tail_view = lambda t: jnp.transpose(t.reshape(bp, ATT_HEADS, HEAD_DIM, MAX_WINDOW), (0, 3, 1, 2))
        p_new[0].append(h_last.reshape(bp, LRU_WIDTH))
        p_new[1].append(pp3[:, last, P_COLS["xa"][0]:P_COLS["xa"][0] + LRU_WIDTH])
        p_new[2].append(tail_view(k_tail))
        p_new[3].append(tail_view(v_tail))
        p_new[4].append(ssd_state)
        p_new[5].append(pp3[:, last, P_COLS["xbc"][0]:P_COLS["xbc"][0] + SSD_CONV_CH])

        (ps,) = _in_proj(ys, g_in, w_in_b[l], 1, DENSE_TM, with_tail=False)
        out_a, h_all = _mix_a_decode(ps, buf_a[l], h0_a[l], *a_args, tm=DENSE_TM)
        out_b = _attn_decode(ps, cache_kt, cache_vt, l, mult)
        out_c, ssd_state = _ssd_decode(ps, buf_c[l], state_ssd, l, *c_args)
        ys = _post(out_a, out_b, out_c, ys, *post_w, tm=DENSE_TM)
        ps3 = ps.reshape(bs, DEC_T, N_IN_PAD)
        last = slice(DEC_T - (CONV_W - 1), DEC_T)
        new_kv = lambda name: ps3[:, :, P_COLS[name][0]:P_COLS[name][0] + ATT_WIDTH].reshape(
            bs, DEC_T, ATT_HEADS, HEAD_DIM)
        s_new[0].append(h_all.reshape(bs, DEC_T, LRU_WIDTH)[:, DEC_T - 1])
        s_new[1].append(ps3[:, last, P_COLS["xa"][0]:P_COLS["xa"][0] + LRU_WIDTH])
        s_new[2].append(new_kv("k"))
        s_new[3].append(new_kv("v"))
        s_new[4].append(ssd_state)
        s_new[5].append(ps3[:, last, P_COLS["xbc"][0]:P_COLS["xbc"][0] + SSD_CONV_CH])

    outs_p = [jnp.stack(a) for a in p_new]
    outs_s = [jnp.stack(a) for a in s_new]
    return (yp.reshape(bp, seq, D_MODEL), ys.reshape(bs, DEC_T, D_MODEL), *outs_p, *outs_s)
```

```python
import functools

import numpy as np
import jax
import jax.numpy as jnp
from jax import lax
from jax.experimental import pallas as pl
from jax.experimental.pallas import tpu as pltpu

F32 = jnp.float32
BF16 = jnp.bfloat16

D_MODEL = 1024
DEPTH = 4
CONV_W = 4
HEAD_DIM = 64
ATT_WIDTH = 384
ATT_HEADS = 6
ATT_SPAN = 128
DILATIONS = (1, 4, 16)
MAX_WINDOW = 2048
ATT_SCALE = HEAD_DIM ** -0.5
SSD_WIDTH = 384
SSD_HEADS = 6
SSD_HEAD_DIM = 64
SSD_GROUPS = 2
SSD_STATE = 128
SSD_CHUNK = 128
SSD_CONV_CH = 896
LRU_WIDTH = 256
LRU_BLOCKS = 4
LRU_C = 8.0
D_FF = 2816
N_IN = 2950
EPS = 1e-6

LANES = 128
SUBLANES = 8
N_IN_PAD = 3072
VMEM_LIMIT = 56 * 1024 * 1024

W_IN_COLS = {"ga": (0, 256), "xa": (256, 256), "q": (512, 384), "k": (896, 384), "v": (1280, 384),
             "z": (1664, 384), "xbc": (2048, 896), "dt": (2944, 6)}
P_COLS = {"xbc": (0, 896), "dt": (896, 128), "ga": (1024, 256), "xa": (1280, 256), "q": (1536, 384),
          "k": (1920, 384), "v": (2304, 384), "z": (2688, 384)}


def _pcol(name, width=None):
    off, w = P_COLS[name]
    width = width or w
    assert off % width == 0
    return off // width

NEG_INF = float("-inf")
NT_DIMS = (((1,), (1,)), ((), ()))


def _cparams(sem):
    return pltpu.CompilerParams(dimension_semantics=sem, vmem_limit_bytes=VMEM_LIMIT)


def _const_spec(shape):
    nd = len(shape)
    return pl.BlockSpec(shape, lambda *_: (0,) * nd, pipeline_mode=pl.Buffered(1))


def _rms(x, g):
    ms = jnp.mean(x * x, axis=-1, keepdims=True)
    return x * lax.rsqrt(ms + EPS) * g


def _sigmoid(x):
    return jax.nn.sigmoid(x)


def _silu(x):
    return x * jax.nn.sigmoid(x)


def _softplus(x):
    return jnp.maximum(x, 0.0) + jnp.log1p(jnp.exp(-jnp.abs(x)))


def _gelu_tanh(x):
    c = np.sqrt(2.0 / np.pi).astype(np.float32)
    return 0.5 * x * (1.0 + jnp.tanh(c * (x + 0.044715 * (x * x * x))))


def _roll_rows(x, shift):
    n = x.shape[0]
    shift = shift % n
    if shift == 0:
        return x
    return pltpu.roll(x, shift, 0)


def _row_index(shape, group=None):
    r = lax.broadcasted_iota(jnp.int32, shape, 0)
    if group is not None:
        r = jnp.bitwise_and(r, group - 1)
    return r


def _lin_scan(a, u, row, steps):
    s = 1
    for _ in range(steps):
        keep = row >= s
        a_sh = jnp.where(keep, _roll_rows(a, s), 1.0)
        u_sh = jnp.where(keep, _roll_rows(u, s), 0.0)
        u = a * u_sh + u
        a = a * a_sh
        s *= 2
    return a, u


def _cumsum_rows(x, row, steps):
    s = 1
    for _ in range(steps):
        x = x + jnp.where(row >= s, _roll_rows(x, s), 0.0)
        s *= 2
    return x


def _suffix_sum_rows(x, row, group, steps):
    incl = x
    s = 1
    for _ in range(steps):
        incl = incl + jnp.where(row < group - s, _roll_rows(incl, -s), 0.0)
        s *= 2
    return incl - x


def _conv_taps(x, shifted_fn, w_ref, b_ref):
    y = b_ref[...] + w_ref[CONV_W - 1:CONV_W, :] * x
    for s in range(1, CONV_W):
        y = y + w_ref[CONV_W - 1 - s:CONV_W - s, :] * shifted_fn(s)
    return y


def _conv_block_carry(x, tail_ref, w_ref, b_ref):
    tb, c = x.shape
    x3 = x.reshape(tb // SUBLANES, SUBLANES, c)
    sub = lax.broadcasted_iota(jnp.int32, (1, SUBLANES, c), 1)
    tail = tail_ref[...]

    def shifted(s):
        r = pltpu.roll(x3, s, 1)
        before = jnp.concatenate([pltpu.roll(tail, s, 0)[None], r[:-1]], axis=0)
        return jnp.where(sub < s, before, r)

    y = _conv_taps(x3, shifted, w_ref, b_ref)
    tail_ref[...] = x[tb - SUBLANES:tb]
    return y.reshape(tb, c)


def _conv_groups(x, bufpad, w_ref, b_ref):
    row = _row_index(x.shape, SUBLANES)
    return _conv_taps(
        x, lambda s: jnp.where(row >= s, _roll_rows(x, s), _roll_rows(bufpad, s - (CONV_W - 1))), w_ref, b_ref)


def _lru_gates(xc, wa_ref, ba_ref, wx_ref, bx_ref, lam_ref):
    xb = xc.astype(BF16)
    r = _sigmoid(jnp.dot(xb, wa_ref[...], preferred_element_type=F32) + ba_ref[...])
    ig = _sigmoid(jnp.dot(xb, wx_ref[...], preferred_element_type=F32) + bx_ref[...])
    log_a = (-LRU_C) * r * _softplus(-lam_ref[...])
    a = jnp.exp(log_a)
    t = jnp.tanh(log_a)
    u = jnp.sqrt(-2.0 * t / (1.0 - t)) * (ig * xc)
    return a, u


def _in_proj_kernel(x_ref, g_ref, w_ref, p_ref, *tail_refs, first_tail):
    h = _rms(x_ref[...], g_ref[...]).astype(BF16)
    p_ref[...] = jnp.dot(h, w_ref[...], preferred_element_type=F32)
    if tail_refs:
        kt_ref, vt_ref = tail_refs

        @pl.when(pl.program_id(1) >= first_tail)
        def _():
            k_off, v_off = P_COLS["k"][0], P_COLS["v"][0]
            kt_ref[...] = p_ref[:, k_off:k_off + ATT_WIDTH].T
            vt_ref[...] = p_ref[:, v_off:v_off + ATT_WIDTH].T


def _in_proj(x, g, w, batch, tm, with_tail):
    n = x.shape[0]
    nt = n // batch // tm
    first_tail = nt - MAX_WINDOW // tm
    row = lambda width: pl.BlockSpec((tm, width), lambda b, j: (b * nt + j, 0))
    out_specs = [row(N_IN_PAD)]
    out_shape = [jax.ShapeDtypeStruct((n, N_IN_PAD), F32)]
    if with_tail:
        tail = pl.BlockSpec((None, ATT_WIDTH, tm), lambda b, j: (b, 0, jnp.maximum(j - first_tail, 0)))
        out_specs += [tail, tail]
        out_shape += [jax.ShapeDtypeStruct((batch, ATT_WIDTH, MAX_WINDOW), F32)] * 2
    return pl.pallas_call(
        functools.partial(_in_proj_kernel, first_tail=first_tail),
        grid=(batch, nt),
        in_specs=[row(D_MODEL), _const_spec((1, D_MODEL)), _const_spec((D_MODEL, N_IN_PAD))],
        out_specs=out_specs,
        out_shape=out_shape,
        compiler_params=_cparams(("parallel", "arbitrary")),
        name="in_proj",
    )(x, g, w)


FFN_CHUNK = 256


def _post_kernel(a_ref, b_ref, c_ref, x_ref, wo_ref, go_ref, gi_ref, wgu_ref, wd_ref, gf_ref, o_ref):
    mixed = jnp.concatenate([a_ref[...], b_ref[...], c_ref[...]], axis=1).astype(BF16)
    x1 = x_ref[...] + _rms(jnp.dot(mixed, wo_ref[...], preferred_element_type=F32), go_ref[...])
    h = _rms(x1, gi_ref[...]).astype(BF16)
    acc = jnp.zeros(x1.shape, F32)
    for c in range(D_FF // FFN_CHUNK):
        lo = c * FFN_CHUNK
        g = jnp.dot(h, wgu_ref[:, lo:lo + FFN_CHUNK], preferred_element_type=F32)
        u = jnp.dot(h, wgu_ref[:, D_FF + lo:D_FF + lo + FFN_CHUNK], preferred_element_type=F32)
        act = (_silu(g) * u).astype(BF16)
        acc = acc + jnp.dot(act, wd_ref[lo:lo + FFN_CHUNK, :], preferred_element_type=F32)
    o_ref[...] = x1 + _rms(acc, gf_ref[...])


def _post(a, b, c, x, wo, go, gi, wgu, wd, gf, tm):
    n = x.shape[0]
    row = lambda width: pl.BlockSpec((tm, width), lambda i: (i, 0))
    vec = _const_spec((1, D_MODEL))
    return pl.pallas_call(
        _post_kernel,
        grid=(n // tm,),
        in_specs=[row(LRU_WIDTH), row(ATT_WIDTH), row(SSD_WIDTH), row(D_MODEL),
                  _const_spec((D_MODEL, D_MODEL)), vec, vec,
                  _const_spec((D_MODEL, 2 * D_FF)), _const_spec((D_FF, D_MODEL)), vec],
        out_specs=row(D_MODEL),
        out_shape=jax.ShapeDtypeStruct((n, D_MODEL), F32),
        compiler_params=_cparams(("parallel",)),
        name="post",
    )(a, b, c, x, wo, go, gi, wgu, wd, gf)


def _mix_a_prompt_kernel(xa_ref, ga_ref, cw_ref, cb_ref, wa_ref, ba_ref, wx_ref, bx_ref, lam_ref,
                         out_ref, hlast_ref, tail_ref, hc_ref):
    @pl.when(pl.program_id(1) == 0)
    def _():
        tail_ref[...] = jnp.zeros_like(tail_ref)
        hc_ref[...] = jnp.zeros_like(hc_ref)

    x = xa_ref[...]
    tb, c = x.shape
    xc = _conv_block_carry(x, tail_ref, cw_ref, cb_ref)
    a, u = _lru_gates(xc, wa_ref, ba_ref, wx_ref, bx_ref, lam_ref)
    gate = _gelu_tanh(ga_ref[...])
    nslab = tb // SUBLANES
    a3 = a.reshape(nslab, SUBLANES, c)
    u3 = u.reshape(nslab, SUBLANES, c)
    sub = lax.broadcasted_iota(jnp.int32, (1, SUBLANES, c), 1)
    s = 1
    while s < SUBLANES:
        keep = sub >= s
        a_sh = jnp.where(keep, pltpu.roll(a3, s, 1), 1.0)
        u_sh = jnp.where(keep, pltpu.roll(u3, s, 1), 0.0)
        u3 = a3 * u_sh + u3
        a3 = a3 * a_sh
        s *= 2
    h_row = hc_ref[0:1, :]
    for k in range(nslab):
        h = u3[k] + a3[k] * h_row
        h_row = h[SUBLANES - 1:SUBLANES, :]
        rows = slice(k * SUBLANES, (k + 1) * SUBLANES)
        out_ref[rows, :] = h * gate[rows, :]
    hc_ref[...] = jnp.broadcast_to(h_row, hc_ref.shape)
    hlast_ref[...] = h_row


def _mix_a_prompt(p, cw, cb, wa, ba, wx, bx, lam, batch, tb):
    n = p.shape[0]
    nb = n // batch // tb
    blk = pl.BlockSpec((tb, LRU_WIDTH), lambda b, i: (b * nb + i, 0))
    col = lambda name: pl.BlockSpec((tb, LRU_WIDTH), lambda b, i: (b * nb + i, _pcol(name)))
    vec = _const_spec((1, LRU_WIDTH))
    mat = _const_spec((LRU_WIDTH, LRU_WIDTH))
    return pl.pallas_call(
        _mix_a_prompt_kernel,
        grid=(batch, nb),
        in_specs=[col("xa"), col("ga"), _const_spec((CONV_W, LRU_WIDTH)), vec, mat, vec, mat, vec, vec],
        out_specs=[blk, pl.BlockSpec((None, 1, LRU_WIDTH), lambda b, i: (b, 0, 0))],
        out_shape=[jax.ShapeDtypeStruct((n, LRU_WIDTH), F32),
                   jax.ShapeDtypeStruct((batch, 1, LRU_WIDTH), F32)],
        scratch_shapes=[pltpu.VMEM((SUBLANES, LRU_WIDTH), F32), pltpu.VMEM((SUBLANES, LRU_WIDTH), F32)],
        compiler_params=_cparams(("parallel", "arbitrary")),
        name="mix_a_prompt",
    )(p, p, cw, cb, wa, ba, wx, bx, lam)


def _mix_a_decode_kernel(xa_ref, ga_ref, buf_ref, h0_ref, cw_ref, cb_ref, wa_ref, ba_ref, wx_ref, bx_ref,
                         lam_ref, out_ref, h_ref):
    x = xa_ref[...]
    xc = _conv_groups(x, buf_ref[...], cw_ref, cb_ref)
    a, u = _lru_gates(xc, wa_ref, ba_ref, wx_ref, bx_ref, lam_ref)
    u = u + a * h0_ref[...]
    row = _row_index(x.shape, SUBLANES)
    _, h = _lin_scan(a, u, row, 3)
    h_ref[...] = h
    out_ref[...] = h * _gelu_tanh(ga_ref[...])


def _mix_a_decode(p, bufpad, h0pad, cw, cb, wa, ba, wx, bx, lam, tm):
    n = p.shape[0]
    blk = pl.BlockSpec((tm, LRU_WIDTH), lambda i: (i, 0))
    col = lambda name: pl.BlockSpec((tm, LRU_WIDTH), lambda i: (i, _pcol(name)))
    vec = _const_spec((1, LRU_WIDTH))
    mat = _const_spec((LRU_WIDTH, LRU_WIDTH))
    return pl.pallas_call(
        _mix_a_decode_kernel,
        grid=(n // tm,),
        in_specs=[col("xa"), col("ga"), blk, blk, _const_spec((CONV_W, LRU_WIDTH)), vec, mat, vec, mat, vec, vec],
        out_specs=[blk, blk],
        out_shape=[jax.ShapeDtypeStruct((n, LRU_WIDTH), F32), jax.ShapeDtypeStruct((n, LRU_WIDTH), F32)],
        compiler_params=_cparams(("parallel",)),
        name="mix_a_decode",
    )(p, p, bufpad, h0pad, cw, cb, wa, ba, wx, bx, lam)


ATT_BLK = 2048
ATT_UNIT = 128
ATT_UNROLL = 4


def _attn_unit(q_t, kp_t, kc_t, vp_t, vc_t, bias, lane_lo):
    zero = jnp.zeros_like(q_t)
    qs = q_t * ATT_SCALE
    qq = jnp.concatenate([jnp.where(lane_lo, qs, zero), jnp.where(lane_lo, zero, qs)], axis=0).astype(BF16)
    kk = jnp.concatenate([kp_t, kc_t], axis=0).astype(BF16)
    s = lax.dot_general(qq, kk, (((1,), (1,)), ((), ())), preferred_element_type=F32) + bias
    m = jnp.max(s, axis=1, keepdims=True)
    p = jnp.exp(s - m).astype(BF16)
    pcat = jnp.concatenate([p[:ATT_UNIT], p[ATT_UNIT:]], axis=1)
    one_lo = jnp.where(lane_lo, 1.0, 0.0)
    one_hi = 1.0 - one_lo
    w = jnp.concatenate([
        jnp.concatenate([jnp.where(lane_lo, vp_t, zero), one_lo], axis=1),
        jnp.concatenate([jnp.where(lane_lo, vc_t, zero), one_lo], axis=1),
        jnp.concatenate([jnp.where(lane_lo, zero, vp_t), one_hi], axis=1),
        jnp.concatenate([jnp.where(lane_lo, zero, vc_t), one_hi], axis=1)], axis=0).astype(BF16)
    ol = jnp.dot(pcat, w, preferred_element_type=F32)
    m_t = jnp.where(lane_lo, jnp.broadcast_to(m[:ATT_UNIT], q_t.shape), jnp.broadcast_to(m[ATT_UNIT:], q_t.shape))
    return ol[:, :LANES], m_t, ol[:, LANES:]


DEC_T = 8
DEC_NEW_PAD = 128
DEC_KEYS = MAX_WINDOW + DEC_NEW_PAD
DEC_ROWS = ATT_HEADS * DEC_T
ATT_PHASES = 6


def _decode_key_multiplicity():
    pos = np.full((DEC_KEYS,), -10 ** 9, np.int64)
    pos[:MAX_WINDOW] = np.arange(MAX_WINDOW)
    pos[MAX_WINDOW:MAX_WINDOW + DEC_T] = MAX_WINDOW + np.arange(DEC_T)
    t = np.arange(DEC_T)
    dist = (MAX_WINDOW + t)[:, None] - pos[None, :]
    cnt = np.zeros(dist.shape, np.float32)
    for d in DILATIONS:
        cnt += ((dist >= 0) & (dist % d == 0) & (dist <= ATT_SPAN * d)).astype(np.float32)
    return np.tile(cnt, (ATT_HEADS, 1))


def _attn_decode_block(q_ref, kn_ref, vn_ref, kt_ref, vt_ref, mult_ref, o_ref):
    lane = lax.broadcasted_iota(jnp.int32, (DEC_T, ATT_WIDTH), 1)
    head_masks = [jnp.logical_and(lane >= HEAD_DIM * h, lane < HEAD_DIM * (h + 1)) for h in range(ATT_HEADS)]
    mult = mult_ref[...]
    seen = mult > 0.0
    zpad = jnp.zeros((DEC_NEW_PAD - DEC_T, ATT_WIDTH), F32)
    for b in range(kt_ref.shape[0]):
        new_rows = slice(DEC_T * b, DEC_T * (b + 1))
        qb = q_ref[new_rows, :] * ATT_SCALE
        qbd = jnp.concatenate([jnp.where(hm, qb, 0.0) for hm in head_masks], axis=0).astype(BF16)
        kt = kt_ref[b].reshape(ATT_WIDTH, MAX_WINDOW).astype(BF16)
        kn = jnp.concatenate([kn_ref[new_rows, :], zpad], axis=0).astype(BF16)
        s = jnp.concatenate([jnp.dot(qbd, kt, preferred_element_type=F32),
                             lax.dot_general(qbd, kn, NT_DIMS, preferred_element_type=F32)], axis=1)
        s = jnp.where(seen, s, NEG_INF)
        m = jnp.max(s, axis=1, keepdims=True)
        p = mult * jnp.exp(s - m)
        p = (p / jnp.sum(p, axis=1, keepdims=True)).astype(BF16)
        vt = vt_ref[b].reshape(ATT_WIDTH, MAX_WINDOW).astype(BF16)
        vn = jnp.concatenate([vn_ref[new_rows, :], zpad], axis=0).astype(BF16)
        o = (lax.dot_general(p[:, :MAX_WINDOW], vt, NT_DIMS, preferred_element_type=F32)
             + jnp.dot(p[:, MAX_WINDOW:], vn, preferred_element_type=F32))
        out = jnp.zeros((DEC_T, ATT_WIDTH), F32)
        for h, hm in enumerate(head_masks):
            out = out + jnp.where(hm, o[DEC_T * h:DEC_T * (h + 1), :], 0.0)
        o_ref[new_rows, :] = out


def _attn_kernel(q_ref, kp_ref, kc_ref, vp_ref, vc_ref, qn_ref, kn_ref, vn_ref, kt_ref, vt_ref, mult_ref,
                 o_ref, od_ref, acc_ref, m_ref, l_ref, *, dec_blocks):
    sub = pl.program_id(3)
    step = ((pl.program_id(0) * pl.num_programs(1) + pl.program_id(1)) * pl.num_programs(2)
            + pl.program_id(2)) * ATT_PHASES + sub

    @pl.when(step < dec_blocks)
    def _():
        _attn_decode_block(qn_ref, kn_ref, vn_ref, kt_ref, vt_ref, mult_ref, od_ref)

    first_block = pl.program_id(2) == 0
    lane_lo = lax.broadcasted_iota(jnp.int32, (ATT_UNIT, LANES), 1) < HEAD_DIM
    qi = lax.broadcasted_iota(jnp.int32, (2 * ATT_UNIT, 2 * ATT_UNIT), 0) & (ATT_UNIT - 1)
    ki = lax.broadcasted_iota(jnp.int32, (2 * ATT_UNIT, 2 * ATT_UNIT), 1)
    is_prev = ki < ATT_UNIT
    dist = qi - ki + ATT_UNIT
    bias = jnp.where(dist >= 0, jnp.where(dist <= ATT_SPAN, 0.0, NEG_INF), NEG_INF)
    bias_first = bias + jnp.where(is_prev, jnp.where(first_block, NEG_INF, 0.0), 0.0)

    def rows(start, d):
        if d == 1:
            return pl.ds(start, ATT_UNIT)
        return pl.ds(start, ATT_UNIT, stride=d)

    def keep(bi, sl, o_t, m_t, l_t):
        acc_ref[bi, sl, :] = o_t
        m_ref[bi, sl, :] = m_t
        l_ref[bi, sl, :] = l_t

    def head_unit(bi, rho):
        d = DILATIONS[bi]
        cur = rows(rho, d)
        prev = rows(rho + ATT_BLK - d * ATT_UNIT, d)
        keep(bi, cur, *_attn_unit(q_ref[cur, :], kp_ref[prev, :], kc_ref[cur, :], vp_ref[prev, :],
                                  vc_ref[cur, :], bias_first, lane_lo))

    def inner_unit(bi, idx):
        d = DILATIONS[bi]
        rho = idx & (d - 1)
        j = 1 + (idx >> int(np.log2(d)))
        start = rho + d * ATT_UNIT * j
        cur = rows(start, d)
        prev = rows(start - d * ATT_UNIT, d)
        keep(bi, cur, *_attn_unit(q_ref[cur, :], kc_ref[prev, :], kc_ref[cur, :], vc_ref[prev, :],
                                  vc_ref[cur, :], bias, lane_lo))

    def chunks(unit, bi, base, lo, hi):
        def body(c, carry):
            for u in range(ATT_UNROLL):
                unit(bi, base + ATT_UNROLL * c + u)
            return carry
        lax.fori_loop(lo, hi, body, 0)

    @pl.when(sub == 0)
    def _():
        head_unit(0, 0)
        for idx in range(ATT_UNROLL - 1):
            inner_unit(0, idx)

    @pl.when(sub <= 1)
    def _():
        chunks(inner_unit, 0, ATT_UNROLL - 1, sub, 1 + 2 * sub)

    @pl.when(sub == 2)
    def _():
        for rho in range(ATT_UNROLL):
            head_unit(1, rho)

    @pl.when(jnp.logical_or(sub == 2, sub == 3))
    def _():
        chunks(inner_unit, 1, 0, sub - 2, 2 * sub - 3)

    @pl.when(sub >= 4)
    def _():
        chunks(head_unit, 2, 0, 2 * sub - 8, 2 * sub - 6)

    @pl.when(sub == ATT_PHASES - 1)
    def _():
        def combine(c, carry):
            sl = pl.ds(pl.multiple_of(c * ATT_UNIT, ATT_UNIT), ATT_UNIT)
            ms = [m_ref[bi, sl, :] for bi in range(len(DILATIONS))]
            m = functools.reduce(jnp.maximum, ms)
            ws = [jnp.exp(mi - m) for mi in ms]
            num = sum(w * acc_ref[bi, sl, :] for bi, w in enumerate(ws))
            den = sum(w * l_ref[bi, sl, :] for bi, w in enumerate(ws))
            o_ref[sl, :] = num / den
            return carry

        lax.fori_loop(0, ATT_BLK // ATT_UNIT, combine, 0, unroll=2)


def _attn(pp, ps, cache_kt, cache_vt, layer, mult, batch):
    n = pp.shape[0]
    nb = n // batch // ATT_BLK
    npair = ATT_WIDTH // LANES
    nd = ps.shape[0]
    nseq = nd // DEC_T
    steps = batch * npair * nb * ATT_PHASES
    spb = -(-nseq // steps)
    assert nseq % spb == 0
    dec_blocks = nseq // spb

    def dec_blk(b, hp, i, s):
        return jnp.minimum(((b * npair + hp) * nb + i) * ATT_PHASES + s, dec_blocks - 1)

    out = pl.BlockSpec((ATT_BLK, LANES), lambda b, hp, i, s: (b * nb + i, hp))
    cur = lambda name: pl.BlockSpec((ATT_BLK, LANES), lambda b, hp, i, s: (b * nb + i, _pcol(name, LANES) + hp))
    prev = lambda name: pl.BlockSpec(
        (ATT_BLK, LANES), lambda b, hp, i, s: (b * nb + jnp.maximum(i - 1, 0), _pcol(name, LANES) + hp))
    new = lambda name: pl.BlockSpec((DEC_T * spb, ATT_WIDTH), lambda b, hp, i, s: (dec_blk(b, hp, i, s), _pcol(name)))
    win = pl.BlockSpec((None, spb, ATT_HEADS, HEAD_DIM, MAX_WINDOW),
                       lambda b, hp, i, s: (layer, dec_blk(b, hp, i, s), 0, 0, 0))
    scratch = pltpu.VMEM((len(DILATIONS), ATT_BLK, LANES), F32)
    return pl.pallas_call(
        functools.partial(_attn_kernel, dec_blocks=dec_blocks),
        grid=(batch, npair, nb, ATT_PHASES),
        in_specs=[cur("q"), prev("k"), cur("k"), prev("v"), cur("v"), new("q"), new("k"), new("v"), win, win,
                  _const_spec((DEC_ROWS, DEC_KEYS))],
        out_specs=[out, pl.BlockSpec((DEC_T * spb, ATT_WIDTH), lambda b, hp, i, s: (dec_blk(b, hp, i, s), 0))],
        out_shape=[jax.ShapeDtypeStruct((n, ATT_WIDTH), F32), jax.ShapeDtypeStruct((nd, ATT_WIDTH), F32)],
        scratch_shapes=[scratch, scratch, scratch],
        compiler_params=_cparams(("arbitrary", "arbitrary", "arbitrary", "arbitrary")),
        name="attn",
    )(pp, pp, pp, pp, pp, ps, ps, ps, cache_kt, cache_vt, mult)


HPG = SSD_HEADS // SSD_GROUPS
B_OFF = SSD_WIDTH
C_OFF = SSD_WIDTH + SSD_GROUPS * SSD_STATE


def _ssd_chunk_diag(xbc, dt, acum, pair_ok):
    acum_t = acum.T
    ys, xrs = [], []
    for g in range(SSD_GROUPS):
        bg = xbc[:, B_OFF + g * SSD_STATE:B_OFF + (g + 1) * SSD_STATE].astype(BF16)
        cg = xbc[:, C_OFF + g * SSD_STATE:C_OFF + (g + 1) * SSD_STATE].astype(BF16)
        cb = lax.dot_general(cg, bg, NT_DIMS, preferred_element_type=F32)
        for h in range(g * HPG, (g + 1) * HPG):
            xr = xbc[:, h * SSD_HEAD_DIM:(h + 1) * SSD_HEAD_DIM] * dt[:, h:h + 1]
            diff = acum[:, h:h + 1] - acum_t[h:h + 1, :]
            lmat = jnp.exp(jnp.where(pair_ok, diff, NEG_INF))
            ys.append(jnp.dot((cb * lmat).astype(BF16), xr.astype(BF16), preferred_element_type=F32))
            xrs.append(xr)
    return ys, xrs


def _ssd_finish(y, xs, z, dskip_ref, norm_ref):
    y = y + dskip_ref[...] * xs
    y = y * _silu(z)
    return _rms(y, norm_ref[...])


DT_COPIES = 3
SSD_PAIRS = SSD_HEADS // 2
SSD_CONV_ROWS = 128


def _ssd_prompt_kernel(xbc_ref, z_ref, dt_ref, cw_ref, cb_ref, dtb_ref, alog_ref, dskip_ref, norm_ref,
                       out_ref, state_ref, tail_ref, xc_ref, st_ref):
    tb = xbc_ref.shape[0]

    @pl.when(pl.program_id(1) == 0)
    def _():
        tail_ref[...] = jnp.zeros_like(tail_ref)
        st_ref[...] = jnp.zeros_like(st_ref)

    for r0 in range(0, tb, SSD_CONV_ROWS):
        xc_ref[r0:r0 + SSD_CONV_ROWS, :] = _silu(
            _conv_block_carry(xbc_ref[r0:r0 + SSD_CONV_ROWS, :], tail_ref, cw_ref, cb_ref))
    a_neg = -jnp.exp(alog_ref[...])
    row = _row_index((SSD_CHUNK, LANES))
    lane = lax.broadcasted_iota(jnp.int32, (SSD_CHUNK, LANES), 1)
    lane_lo = lane < SSD_HEAD_DIM
    causal = row >= lane
    zero = jnp.zeros((SSD_CHUNK, LANES), F32)

    def pair_rows(t):
        return jnp.concatenate([jnp.where(lane_lo, t, zero), jnp.where(lane_lo, zero, t)], axis=0).astype(BF16)

    def chunk(c, carry):
        r0 = pl.multiple_of(c * SSD_CHUNK, SSD_CHUNK)
        rows = pl.ds(r0, SSD_CHUNK)
        xc = xc_ref[rows, :]
        dt = _softplus(dt_ref[rows, :] + dtb_ref[...])
        acum = _cumsum_rows(dt * a_neg, row, 7)
        last = acum[SSD_CHUNK - 1:SSD_CHUNK, :]
        tot = jnp.exp(last)
        pt = jnp.where(lane < SUBLANES, acum, jnp.where(lane < 2 * SUBLANES, dt, dt * jnp.exp(last - acum))).T
        bs = [xc[:, B_OFF + g * SSD_STATE:B_OFF + (g + 1) * SSD_STATE] for g in range(SSD_GROUPS)]
        cs = [xc[:, C_OFF + g * SSD_STATE:C_OFF + (g + 1) * SSD_STATE].astype(BF16) for g in range(SSD_GROUPS)]
        bts = [b.T for b in bs]
        cbs = [lax.dot_general(cs[g], bs[g].astype(BF16), NT_DIMS, preferred_element_type=F32)
               for g in range(SSD_GROUPS)]
        gs, eacs, btws = [], [], []
        for h in range(SSD_HEADS):
            g = h // HPG
            a_col = jnp.broadcast_to(acum[:, h:h + 1], (SSD_CHUNK, SSD_CHUNK))
            lmat = jnp.exp(jnp.where(causal, a_col - pt[h:h + 1, :], NEG_INF))
            gs.append((cbs[g] * lmat * pt[SUBLANES + h:SUBLANES + h + 1, :]).astype(BF16))
            eacs.append(jnp.exp(a_col))
            btws.append((bts[g] * pt[2 * SUBLANES + h:2 * SUBLANES + h + 1, :]).astype(BF16))
        outs = []
        for k in range(SSD_PAIRS):
            h0, h1 = 2 * k, 2 * k + 1
            g0, g1 = h0 // HPG, h1 // HPG
            x2 = pair_rows(xc[:, k * LANES:(k + 1) * LANES])
            y = jnp.dot(jnp.concatenate([gs[h0], gs[h1]], axis=1), x2, preferred_element_type=F32)
            st = st_ref[k]
            if g0 == g1:
                y_off = jnp.dot(cs[g0], st.astype(BF16), preferred_element_type=F32)
            else:
                y_off = jnp.dot(jnp.concatenate([cs[g0], cs[g1]], axis=1), pair_rows(st),
                                preferred_element_type=F32)
            outs.append(y + y_off * jnp.where(lane_lo, eacs[h0], eacs[h1]))
            upd = jnp.dot(jnp.concatenate([btws[h0], btws[h1]], axis=1), x2, preferred_element_type=F32)
            st_ref[k] = st * jnp.where(lane_lo, tot[:, h0:h0 + 1], tot[:, h1:h1 + 1]) + upd
        y = jnp.concatenate(outs, axis=1)
        out_ref[rows, :] = _ssd_finish(y, xc[:, :SSD_WIDTH], z_ref[rows, :], dskip_ref, norm_ref)
        return carry

    lax.fori_loop(0, tb // SSD_CHUNK, chunk, 0)
    for k in range(SSD_PAIRS):
        t = st_ref[k].T
        state_ref[2 * k] = t[:SSD_HEAD_DIM]
        state_ref[2 * k + 1] = t[SSD_HEAD_DIM:]


def _ssd_prompt(p, cw, cb, dtb, alog, dskip, norm, batch, tb):
    n = p.shape[0]
    nb = n // batch // tb
    blk = lambda width: pl.BlockSpec((tb, width), lambda b, i: (b * nb + i, 0))
    col = lambda name: pl.BlockSpec((tb, P_COLS[name][1]), lambda b, i: (b * nb + i, _pcol(name)))
    return pl.pallas_call(
        _ssd_prompt_kernel,
        grid=(batch, nb),
        in_specs=[col("xbc"), col("z"), col("dt"),
                  _const_spec((CONV_W, SSD_CONV_CH)), _const_spec((1, SSD_CONV_CH)),
                  _const_spec((1, LANES)), _const_spec((1, LANES)),
                  _const_spec((1, SSD_WIDTH)), _const_spec((1, SSD_WIDTH))],
        out_specs=[blk(SSD_WIDTH),
                   pl.BlockSpec((None, SSD_HEADS, SSD_HEAD_DIM, SSD_STATE), lambda b, i: (b, 0, 0, 0))],
        out_shape=[jax.ShapeDtypeStruct((n, SSD_WIDTH), F32),
                   jax.ShapeDtypeStruct((batch, SSD_HEADS, SSD_HEAD_DIM, SSD_STATE), F32)],
        scratch_shapes=[pltpu.VMEM((SUBLANES, SSD_CONV_CH), F32),
                        pltpu.VMEM((tb, SSD_CONV_CH), F32),
                        pltpu.VMEM((SSD_PAIRS, SSD_STATE, LANES), F32)],
        compiler_params=_cparams(("parallel", "arbitrary")),
        name="ssd_prompt",
    )(p, p, p, cw, cb, dtb, alog, dskip, norm)


SSD_DEC_SEQ = SSD_CHUNK // DEC_T


def _ssd_decode_kernel(xbc_ref, z_ref, dt_ref, buf_ref, h0_ref, cw_ref, cb_ref, dtb_ref, alog_ref, dskip_ref,
                       norm_ref, out_ref, hnew_ref, xc_ref, xrd_ref, eac_ref, tot_ref, yoff_ref):
    xc = _silu(_conv_groups(xbc_ref[...], buf_ref[...], cw_ref, cb_ref))
    xc_ref[...] = xc
    a_neg = -jnp.exp(alog_ref[...])
    row = _row_index((SSD_CHUNK, LANES), DEC_T)
    li = lax.broadcasted_iota(jnp.int32, (SSD_CHUNK, SSD_CHUNK), 0)
    si = lax.broadcasted_iota(jnp.int32, (SSD_CHUNK, SSD_CHUNK), 1)
    same_seq_causal = jnp.logical_and(li >= si, (li - si) <= (li & (DEC_T - 1)))
    dt = _softplus(dt_ref[...] + dtb_ref[...])
    dta = dt * a_neg
    acum = _cumsum_rows(dta, row, 3)
    rest = _suffix_sum_rows(dta, row, DEC_T, 3)
    eac_ref[...] = jnp.exp(acum)
    tot_ref[...] = jnp.exp(acum + rest)
    decay = jnp.exp(rest)
    ys, xrs = _ssd_chunk_diag(xc, dt, acum, same_seq_causal)
    xrd_ref[...] = jnp.concatenate([xrs[h] * decay[:, h:h + 1] for h in range(SSD_HEADS)], axis=1).T
    seq_of_lane = lax.shift_right_logical(lax.broadcasted_iota(jnp.int32, (SSD_HEAD_DIM, SSD_CHUNK), 1), 3)
    b_all = [xc[:, B_OFF + g * SSD_STATE:B_OFF + (g + 1) * SSD_STATE].astype(BF16) for g in range(SSD_GROUPS)]

    def seq(b, carry):
        r0 = pl.multiple_of(b * DEC_T, DEC_T)
        rows = pl.ds(r0, DEC_T)
        xcb = xc_ref[rows, :]
        eac = eac_ref[rows, :]
        tot = tot_ref[rows, :]
        own = seq_of_lane == b
        outs = []
        for h in range(SSD_HEADS):
            g = h // HPG
            cg = xcb[:, C_OFF + g * SSD_STATE:C_OFF + (g + 1) * SSD_STATE].astype(BF16)
            prev = h0_ref[b, h]
            outs.append(lax.dot_general(cg, prev.astype(BF16), NT_DIMS, preferred_element_type=F32)
                        * eac[:, h:h + 1])
            lhs = jnp.where(own, xrd_ref[h * SSD_HEAD_DIM:(h + 1) * SSD_HEAD_DIM, :], 0.0).astype(BF16)
            st = jnp.dot(lhs, b_all[g], preferred_element_type=F32)
            hnew_ref[b, h] = prev * tot[0:1, h:h + 1] + st
        yoff_ref[rows, :] = jnp.concatenate(outs, axis=1)
        return carry

    lax.fori_loop(0, SSD_DEC_SEQ, seq, 0)
    y = jnp.concatenate(ys, axis=1) + yoff_ref[...]
    out_ref[...] = _ssd_finish(y, xc[:, :SSD_WIDTH], z_ref[...], dskip_ref, norm_ref)


def _ssd_decode(p, bufpad, state, layer, cw, cb, dtb, alog, dskip, norm):
    n = p.shape[0]
    blk = lambda width: pl.BlockSpec((SSD_CHUNK, width), lambda i: (i, 0))
    col = lambda name: pl.BlockSpec((SSD_CHUNK, P_COLS[name][1]), lambda i: (i, _pcol(name)))
    st_in = pl.BlockSpec((None, SSD_DEC_SEQ, SSD_HEADS, SSD_HEAD_DIM, SSD_STATE), lambda i: (layer, i, 0, 0, 0))
    st_out = pl.BlockSpec((SSD_DEC_SEQ, SSD_HEADS, SSD_HEAD_DIM, SSD_STATE), lambda i: (i, 0, 0, 0))
    return pl.pallas_call(
        _ssd_decode_kernel,
        grid=(n // SSD_CHUNK,),
        in_specs=[col("xbc"), col("z"), col("dt"), blk(SSD_CONV_CH), st_in,
                  _const_spec((CONV_W, SSD_CONV_CH)), _const_spec((1, SSD_CONV_CH)),
                  _const_spec((1, LANES)), _const_spec((1, LANES)),
                  _const_spec((1, SSD_WIDTH)), _const_spec((1, SSD_WIDTH))],
        out_specs=[blk(SSD_WIDTH), st_out],
        out_shape=[jax.ShapeDtypeStruct((n, SSD_WIDTH), F32),
                   jax.ShapeDtypeStruct((n // DEC_T, SSD_HEADS, SSD_HEAD_DIM, SSD_STATE), F32)],
        scratch_shapes=[pltpu.VMEM((SSD_CHUNK, SSD_CONV_CH), F32),
                        pltpu.VMEM((SSD_WIDTH, SSD_CHUNK), F32),
                        pltpu.VMEM((SSD_CHUNK, LANES), F32),
                        pltpu.VMEM((SSD_CHUNK, LANES), F32),
                        pltpu.VMEM((SSD_CHUNK, SSD_WIDTH), F32)],
        compiler_params=_cparams(("parallel",)),
        name="ssd_decode",
    )(p, p, p, bufpad, state, cw, cb, dtb, alog, dskip, norm)


DENSE_TM = 512
MIX_A_TB = 512
SSD_TB = 1024


def _pad_state_rows(buf):
    l, b, r, c = buf.shape
    return jnp.pad(buf, ((0, 0), (0, 0), (0, DEC_T - r), (0, 0))).reshape(l, b * DEC_T, c)


def kernel(x_prompt, x_sample, state_lru_h, state_lru_conv, cache_swa_k, cache_swa_v, state_ssd, state_ssd_conv,
           norm_mix_in, norm_mix_out, w_in, conv_a_w, conv_a_b, lru_wa, lru_ba, lru_wx, lru_bx, lru_lambda,
           conv_c_w, conv_c_b, dt_bias, a_log, d_skip, ssm_norm, w_out, norm_ffn_in, norm_ffn_out,
           w_gate_up, w_down):
    bp, seq, _ = x_prompt.shape
    bs, dec_t, _ = x_sample.shape
    assert dec_t == DEC_T and seq % ATT_BLK == 0 and seq >= MAX_WINDOW and bs % SSD_DEC_SEQ == 0

    def dt_lanes(p):
        slot = jnp.pad(p, [(0, 0)] * (p.ndim - 1) + [(0, SUBLANES - SSD_HEADS)])
        rep = jnp.concatenate([slot] * DT_COPIES, axis=-1)
        return jnp.pad(rep, [(0, 0)] * (p.ndim - 1) + [(0, LANES - DT_COPIES * SUBLANES)])

    def w_cols(name):
        off, width = W_IN_COLS[name]
        blk = w_in[:, :, off:off + width]
        if name == "dt":
            return dt_lanes(blk)
        return jnp.pad(blk, ((0, 0), (0, 0), (0, P_COLS[name][1] - width)))

    w_in_b = jnp.concatenate([w_cols(name) for name in P_COLS], axis=2).astype(BF16)
    w_out_b = w_out.astype(BF16)
    w_gu_b = w_gate_up.astype(BF16)
    w_dn_b = w_down.astype(BF16)
    eye = jnp.eye(LRU_BLOCKS, dtype=F32)

    def block_diag(w):
        return (w[:, :, :, None, :] * eye[None, :, None, :, None]).reshape(
            DEPTH, LRU_WIDTH, LRU_WIDTH).astype(BF16)

    wa_bd = block_diag(lru_wa)
    wx_bd = block_diag(lru_wx)
    vec = lambda p: p[:, None, :]
    dtb_p = dt_lanes(dt_bias)[:, None, :]
    alog_p = dt_lanes(a_log)[:, None, :]
    dskip_p = jnp.repeat(d_skip, SSD_HEAD_DIM, axis=1)[:, None, :]
    buf_a = _pad_state_rows(state_lru_conv)
    buf_c = _pad_state_rows(state_ssd_conv)
    h0_a = jnp.pad(state_lru_h[:, :, None, :], ((0, 0), (0, 0), (0, DEC_T - 1), (0, 0))).reshape(
        DEPTH, bs * DEC_T, LRU_WIDTH)
    mult = jnp.asarray(_decode_key_multiplicity())
    cache_kt = jnp.transpose(cache_swa_k, (0, 1, 3, 4, 2))
    cache_vt = jnp.transpose(cache_swa_v, (0, 1, 3, 4, 2))

    yp = x_prompt.reshape(bp * seq, D_MODEL)
    ys = x_sample.reshape(bs * DEC_T, D_MODEL)
    p_new = [[] for _ in range(6)]
    s_new = [[] for _ in range(6)]
    for l in range(DEPTH):
        a_args = (conv_a_w[l], vec(conv_a_b)[l], wa_bd[l], vec(lru_ba)[l], wx_bd[l], vec(lru_bx)[l],
                  vec(lru_lambda)[l])
        c_args = (conv_c_w[l], vec(conv_c_b)[l], dtb_p[l], alog_p[l], dskip_p[l], vec(ssm_norm)[l])
        g_in, g_out = vec(norm_mix_in)[l], vec(norm_mix_out)[l]
        f_in, f_out = vec(norm_ffn_in)[l], vec(norm_ffn_out)[l]

        post_w = (w_out_b[l], g_out, f_in, w_gu_b[l], w_dn_b[l], f_out)

        pp, k_tail, v_tail = _in_proj(yp, g_in, w_in_b[l], bp, DENSE_TM, with_tail=True)
        (ps,) = _in_proj(ys, g_in, w_in_b[l], 1, DENSE_TM, with_tail=False)
        out_b, out_b_s = _attn(pp, ps, cache_kt, cache_vt, l, mult, bp)

        out_a, h_last = _mix_a_prompt(pp, *a_args, batch=bp, tb=MIX_A_TB)
        out_c, ssd_state = _ssd_prompt(pp, *c_args, batch=bp, tb=SSD_TB)
        yp = _post(out_a, out_b, out_c, yp, *post_w, tm=DENSE_TM)
        pp3 = pp.reshape(bp, seq, N_IN_PAD)
        last = slice(seq - (CONV_W - 1), seq)
        tail_view = lambda t: jnp.transpose(t.reshape(bp, ATT_HEADS, HEAD_DIM, MAX_WINDOW), (0, 3, 1, 2))
        p_new[0].append(h_last.reshape(bp, LRU_WIDTH))
        p_new[1].append(pp3[:, last, P_COLS["xa"][0]:P_COLS["xa"][0] + LRU_WIDTH])
        p_new[2].append(tail_view(k_tail))
        p_new[3].append(tail_view(v_tail))
        p_new[4].append(ssd_state)
        p_new[5].append(pp3[:, last, P_COLS["xbc"][0]:P_COLS["xbc"][0] + SSD_CONV_CH])

        out_a, h_all = _mix_a_decode(ps, buf_a[l], h0_a[l], *a_args, tm=DENSE_TM)
        out_c, ssd_state = _ssd_decode(ps, buf_c[l], state_ssd, l, *c_args)
        ys = _post(out_a, out_b_s, out_c, ys, *post_w, tm=DENSE_TM)
        ps3 = ps.reshape(bs, DEC_T, N_IN_PAD)
        last = slice(DEC_T - (CONV_W - 1), DEC_T)
        new_kv = lambda name: ps3[:, :, P_COLS[name][0]:P_COLS[name][0] + ATT_WIDTH].reshape(
            bs, DEC_T, ATT_HEADS, HEAD_DIM)
        s_new[0].append(h_all.reshape(bs, DEC_T, LRU_WIDTH)[:, DEC_T - 1])
        s_new[1].append(ps3[:, last, P_COLS["xa"][0]:P_COLS["xa"][0] + LRU_WIDTH])
        s_new[2].append(new_kv("k"))
        s_new[3].append(new_kv("v"))
        s_new[4].append(ssd_state)
        s_new[5].append(ps3[:, last, P_COLS["xbc"][0]:P_COLS["xbc"][0] + SSD_CONV_CH])

    outs_p = [jnp.stack(a) for a in p_new]
    outs_s = [jnp.stack(a) for a in s_new]
    return (yp.reshape(bp, seq, D_MODEL), ys.reshape(bs, DEC_T, D_MODEL), *outs_p, *outs_s)
```

```python
import functools

import numpy as np
import jax
import jax.numpy as jnp
from jax import lax
from jax.experimental import pallas as pl
from jax.experimental.pallas import tpu as pltpu

F32 = jnp.float32
BF16 = jnp.bfloat16

D_MODEL = 1024
DEPTH = 4
CONV_W = 4
HEAD_DIM = 64
ATT_WIDTH = 384
ATT_HEADS = 6
ATT_SPAN = 128
DILATIONS = (1, 4, 16)
MAX_WINDOW = 2048
ATT_SCALE = HEAD_DIM ** -0.5
SSD_WIDTH = 384
SSD_HEADS = 6
SSD_HEAD_DIM = 64
SSD_GROUPS = 2
SSD_STATE = 128
SSD_CHUNK = 128
SSD_CONV_CH = 896
LRU_WIDTH = 256
LRU_BLOCKS = 4
LRU_C = 8.0
D_FF = 2816
N_IN = 2950
EPS = 1e-6

LANES = 128
SUBLANES = 8
N_IN_PAD = 3072
VMEM_LIMIT = 56 * 1024 * 1024

W_IN_COLS = {"ga": (0, 256), "xa": (256, 256), "q": (512, 384), "k": (896, 384), "v": (1280, 384),
             "z": (1664, 384), "xbc": (2048, 896), "dt": (2944, 6)}
P_COLS = {"xbc": (0, 896), "dt": (896, 128), "ga": (1024, 256), "xa": (1280, 256), "q": (1536, 384),
          "k": (1920, 384), "v": (2304, 384), "z": (2688, 384)}


def _pcol(name, width=None):
    off, w = P_COLS[name]
    width = width or w
    assert off % width == 0
    return off // width

NEG_INF = float("-inf")
NT_DIMS = (((1,), (1,)), ((), ()))


def _cparams(sem):
    return pltpu.CompilerParams(dimension_semantics=sem, vmem_limit_bytes=VMEM_LIMIT)


def _const_spec(shape):
    nd = len(shape)
    return pl.BlockSpec(shape, lambda *_: (0,) * nd, pipeline_mode=pl.Buffered(1))


def _rms(x, g):
    ms = jnp.mean(x * x, axis=-1, keepdims=True)
    return x * lax.rsqrt(ms + EPS) * g


def _sigmoid(x):
    return jax.nn.sigmoid(x)


def _silu(x):
    return x * jax.nn.sigmoid(x)


def _softplus(x):
    return jnp.maximum(x, 0.0) + jnp.log1p(jnp.exp(-jnp.abs(x)))


def _gelu_tanh(x):
    c = np.sqrt(2.0 / np.pi).astype(np.float32)
    return 0.5 * x * (1.0 + jnp.tanh(c * (x + 0.044715 * (x * x * x))))


def _roll_rows(x, shift):
    n = x.shape[0]
    shift = shift % n
    if shift == 0:
        return x
    return pltpu.roll(x, shift, 0)


def _row_index(shape, group=None):
    r = lax.broadcasted_iota(jnp.int32, shape, 0)
    if group is not None:
        r = jnp.bitwise_and(r, group - 1)
    return r


def _lin_scan(a, u, row, steps):
    s = 1
    for _ in range(steps):
        keep = row >= s
        a_sh = jnp.where(keep, _roll_rows(a, s), 1.0)
        u_sh = jnp.where(keep, _roll_rows(u, s), 0.0)
        u = a * u_sh + u
        a = a * a_sh
        s *= 2
    return a, u


def _cumsum_rows(x, row, steps):
    s = 1
    for _ in range(steps):
        x = x + jnp.where(row >= s, _roll_rows(x, s), 0.0)
        s *= 2
    return x


def _suffix_sum_rows(x, row, group, steps):
    incl = x
    s = 1
    for _ in range(steps):
        incl = incl + jnp.where(row < group - s, _roll_rows(incl, -s), 0.0)
        s *= 2
    return incl - x


def _conv_taps(x, shifted_fn, w_ref, b_ref):
    y = b_ref[...] + w_ref[CONV_W - 1:CONV_W, :] * x
    for s in range(1, CONV_W):
        y = y + w_ref[CONV_W - 1 - s:CONV_W - s, :] * shifted_fn(s)
    return y


def _conv_block_carry(x, tail_ref, w_ref, b_ref):
    tb, c = x.shape
    x3 = x.reshape(tb // SUBLANES, SUBLANES, c)
    sub = lax.broadcasted_iota(jnp.int32, (1, SUBLANES, c), 1)
    tail = tail_ref[...]

    def shifted(s):
        r = pltpu.roll(x3, s, 1)
        before = jnp.concatenate([pltpu.roll(tail, s, 0)[None], r[:-1]], axis=0)
        return jnp.where(sub < s, before, r)

    y = _conv_taps(x3, shifted, w_ref, b_ref)
    tail_ref[...] = x[tb - SUBLANES:tb]
    return y.reshape(tb, c)


def _conv_groups(x, bufpad, w_ref, b_ref):
    row = _row_index(x.shape, SUBLANES)
    return _conv_taps(
        x, lambda s: jnp.where(row >= s, _roll_rows(x, s), _roll_rows(bufpad, s - (CONV_W - 1))), w_ref, b_ref)


def _lru_gates(xc, wa_ref, ba_ref, wx_ref, bx_ref, lam_ref):
    xb = xc.astype(BF16)
    r = _sigmoid(jnp.dot(xb, wa_ref[...], preferred_element_type=F32) + ba_ref[...])
    ig = _sigmoid(jnp.dot(xb, wx_ref[...], preferred_element_type=F32) + bx_ref[...])
    log_a = (-LRU_C) * r * _softplus(-lam_ref[...])
    a = jnp.exp(log_a)
    t = jnp.tanh(log_a)
    u = jnp.sqrt(-2.0 * t / (1.0 - t)) * (ig * xc)
    return a, u


def _in_proj_kernel(x_ref, g_ref, w_ref, p_ref, *tail_refs, first_tail):
    h = _rms(x_ref[...], g_ref[...]).astype(BF16)
    p_ref[...] = jnp.dot(h, w_ref[...], preferred_element_type=F32)
    if tail_refs:
        kt_ref, vt_ref = tail_refs

        @pl.when(pl.program_id(1) >= first_tail)
        def _():
            k_off, v_off = P_COLS["k"][0], P_COLS["v"][0]
            kt_ref[...] = p_ref[:, k_off:k_off + ATT_WIDTH].T
            vt_ref[...] = p_ref[:, v_off:v_off + ATT_WIDTH].T


def _in_proj(x, g, w, batch, tm, with_tail):
    n = x.shape[0]
    nt = n // batch // tm
    first_tail = nt - MAX_WINDOW // tm
    row = lambda width: pl.BlockSpec((tm, width), lambda b, j: (b * nt + j, 0))
    out_specs = [row(N_IN_PAD)]
    out_shape = [jax.ShapeDtypeStruct((n, N_IN_PAD), F32)]
    if with_tail:
        tail = pl.BlockSpec((None, ATT_WIDTH, tm), lambda b, j: (b, 0, jnp.maximum(j - first_tail, 0)))
        out_specs += [tail, tail]
        out_shape += [jax.ShapeDtypeStruct((batch, ATT_WIDTH, MAX_WINDOW), F32)] * 2
    return pl.pallas_call(
        functools.partial(_in_proj_kernel, first_tail=first_tail),
        grid=(batch, nt),
        in_specs=[row(D_MODEL), _const_spec((1, D_MODEL)), _const_spec((D_MODEL, N_IN_PAD))],
        out_specs=out_specs,
        out_shape=out_shape,
        compiler_params=_cparams(("parallel", "arbitrary")),
        name="in_proj",
    )(x, g, w)


FFN_CHUNK = 256


def _post_kernel(a_ref, b_ref, c_ref, x_ref, wo_ref, go_ref, gi_ref, wgu_ref, wd_ref, gf_ref, o_ref):
    mixed = jnp.concatenate([a_ref[...], b_ref[...], c_ref[...]], axis=1).astype(BF16)
    x1 = x_ref[...] + _rms(jnp.dot(mixed, wo_ref[...], preferred_element_type=F32), go_ref[...])
    h = _rms(x1, gi_ref[...]).astype(BF16)
    acc = jnp.zeros(x1.shape, F32)
    for c in range(D_FF // FFN_CHUNK):
        lo = c * FFN_CHUNK
        g = jnp.dot(h, wgu_ref[:, lo:lo + FFN_CHUNK], preferred_element_type=F32)
        u = jnp.dot(h, wgu_ref[:, D_FF + lo:D_FF + lo + FFN_CHUNK], preferred_element_type=F32)
        act = (_silu(g) * u).astype(BF16)
        acc = acc + jnp.dot(act, wd_ref[lo:lo + FFN_CHUNK, :], preferred_element_type=F32)
    o_ref[...] = x1 + _rms(acc, gf_ref[...])


def _post(a, b, c, x, wo, go, gi, wgu, wd, gf, tm):
    n = x.shape[0]
    row = lambda width: pl.BlockSpec((tm, width), lambda i: (i, 0))
    vec = _const_spec((1, D_MODEL))
    return pl.pallas_call(
        _post_kernel,
        grid=(n // tm,),
        in_specs=[row(LRU_WIDTH), row(ATT_WIDTH), row(SSD_WIDTH), row(D_MODEL),
                  _const_spec((D_MODEL, D_MODEL)), vec, vec,
                  _const_spec((D_MODEL, 2 * D_FF)), _const_spec((D_FF, D_MODEL)), vec],
        out_specs=row(D_MODEL),
        out_shape=jax.ShapeDtypeStruct((n, D_MODEL), F32),
        compiler_params=_cparams(("parallel",)),
        name="post",
    )(a, b, c, x, wo, go, gi, wgu, wd, gf)


def _mix_a_prompt_kernel(xa_ref, ga_ref, cw_ref, cb_ref, wa_ref, ba_ref, wx_ref, bx_ref, lam_ref,
                         out_ref, hlast_ref, tail_ref, hc_ref):
    @pl.when(pl.program_id(1) == 0)
    def _():
        tail_ref[...] = jnp.zeros_like(tail_ref)
        hc_ref[...] = jnp.zeros_like(hc_ref)

    x = xa_ref[...]
    tb, c = x.shape
    xc = _conv_block_carry(x, tail_ref, cw_ref, cb_ref)
    a, u = _lru_gates(xc, wa_ref, ba_ref, wx_ref, bx_ref, lam_ref)
    gate = _gelu_tanh(ga_ref[...])
    nslab = tb // SUBLANES
    a3 = a.reshape(nslab, SUBLANES, c)
    u3 = u.reshape(nslab, SUBLANES, c)
    sub = lax.broadcasted_iota(jnp.int32, (1, SUBLANES, c), 1)
    s = 1
    while s < SUBLANES:
        keep = sub >= s
        a_sh = jnp.where(keep, pltpu.roll(a3, s, 1), 1.0)
        u_sh = jnp.where(keep, pltpu.roll(u3, s, 1), 0.0)
        u3 = a3 * u_sh + u3
        a3 = a3 * a_sh
        s *= 2
    h_row = hc_ref[0:1, :]
    for k in range(nslab):
        h = u3[k] + a3[k] * h_row
        h_row = h[SUBLANES - 1:SUBLANES, :]
        rows = slice(k * SUBLANES, (k + 1) * SUBLANES)
        out_ref[rows, :] = h * gate[rows, :]
    hc_ref[...] = jnp.broadcast_to(h_row, hc_ref.shape)
    hlast_ref[...] = h_row


def _mix_a_prompt(p, cw, cb, wa, ba, wx, bx, lam, batch, tb):
    n = p.shape[0]
    nb = n // batch // tb
    blk = pl.BlockSpec((tb, LRU_WIDTH), lambda b, i: (b * nb + i, 0))
    col = lambda name: pl.BlockSpec((tb, LRU_WIDTH), lambda b, i: (b * nb + i, _pcol(name)))
    vec = _const_spec((1, LRU_WIDTH))
    mat = _const_spec((LRU_WIDTH, LRU_WIDTH))
    return pl.pallas_call(
        _mix_a_prompt_kernel,
        grid=(batch, nb),
        in_specs=[col("xa"), col("ga"), _const_spec((CONV_W, LRU_WIDTH)), vec, mat, vec, mat, vec, vec],
        out_specs=[blk, pl.BlockSpec((None, 1, LRU_WIDTH), lambda b, i: (b, 0, 0))],
        out_shape=[jax.ShapeDtypeStruct((n, LRU_WIDTH), F32),
                   jax.ShapeDtypeStruct((batch, 1, LRU_WIDTH), F32)],
        scratch_shapes=[pltpu.VMEM((SUBLANES, LRU_WIDTH), F32), pltpu.VMEM((SUBLANES, LRU_WIDTH), F32)],
        compiler_params=_cparams(("parallel", "arbitrary")),
        name="mix_a_prompt",
    )(p, p, cw, cb, wa, ba, wx, bx, lam)


def _mix_a_decode_kernel(xa_ref, ga_ref, buf_ref, h0_ref, cw_ref, cb_ref, wa_ref, ba_ref, wx_ref, bx_ref,
                         lam_ref, out_ref, h_ref):
    x = xa_ref[...]
    xc = _conv_groups(x, buf_ref[...], cw_ref, cb_ref)
    a, u = _lru_gates(xc, wa_ref, ba_ref, wx_ref, bx_ref, lam_ref)
    u = u + a * h0_ref[...]
    row = _row_index(x.shape, SUBLANES)
    _, h = _lin_scan(a, u, row, 3)
    h_ref[...] = h
    out_ref[...] = h * _gelu_tanh(ga_ref[...])


def _mix_a_decode(p, bufpad, h0pad, cw, cb, wa, ba, wx, bx, lam, tm):
    n = p.shape[0]
    blk = pl.BlockSpec((tm, LRU_WIDTH), lambda i: (i, 0))
    col = lambda name: pl.BlockSpec((tm, LRU_WIDTH), lambda i: (i, _pcol(name)))
    vec = _const_spec((1, LRU_WIDTH))
    mat = _const_spec((LRU_WIDTH, LRU_WIDTH))
    return pl.pallas_call(
        _mix_a_decode_kernel,
        grid=(n // tm,),
        in_specs=[col("xa"), col("ga"), blk, blk, _const_spec((CONV_W, LRU_WIDTH)), vec, mat, vec, mat, vec, vec],
        out_specs=[blk, blk],
        out_shape=[jax.ShapeDtypeStruct((n, LRU_WIDTH), F32), jax.ShapeDtypeStruct((n, LRU_WIDTH), F32)],
        compiler_params=_cparams(("parallel",)),
        name="mix_a_decode",
    )(p, p, bufpad, h0pad, cw, cb, wa, ba, wx, bx, lam)


ATT_BLK = 2048
ATT_UNIT = 128
ATT_UNROLL = 4


def _attn_unit(q_t, kp_t, kc_t, vp_t, vc_t, bias, lane_lo):
    zero = jnp.zeros_like(q_t)
    qs = q_t * ATT_SCALE
    qq = jnp.concatenate([jnp.where(lane_lo, qs, zero), jnp.where(lane_lo, zero, qs)], axis=0).astype(BF16)
    kk = jnp.concatenate([kp_t, kc_t], axis=0).astype(BF16)
    s = lax.dot_general(qq, kk, (((1,), (1,)), ((), ())), preferred_element_type=F32) + bias
    m = jnp.max(s, axis=1, keepdims=True)
    p = jnp.exp(s - m).astype(BF16)
    pcat = jnp.concatenate([p[:ATT_UNIT], p[ATT_UNIT:]], axis=1)
    one_lo = jnp.where(lane_lo, 1.0, 0.0)
    one_hi = 1.0 - one_lo
    w = jnp.concatenate([
        jnp.concatenate([jnp.where(lane_lo, vp_t, zero), one_lo], axis=1),
        jnp.concatenate([jnp.where(lane_lo, vc_t, zero), one_lo], axis=1),
        jnp.concatenate([jnp.where(lane_lo, zero, vp_t), one_hi], axis=1),
        jnp.concatenate([jnp.where(lane_lo, zero, vc_t), one_hi], axis=1)], axis=0).astype(BF16)
    ol = jnp.dot(pcat, w, preferred_element_type=F32)
    m_t = jnp.where(lane_lo, jnp.broadcast_to(m[:ATT_UNIT], q_t.shape), jnp.broadcast_to(m[ATT_UNIT:], q_t.shape))
    return ol[:, :LANES], m_t, ol[:, LANES:]


DEC_T = 8
DEC_NEW_PAD = 128
DEC_KEYS = MAX_WINDOW + DEC_NEW_PAD
DEC_ROWS = ATT_HEADS * DEC_T
ATT_PHASES = len(DILATIONS)


def _decode_key_multiplicity():
    pos = np.full((DEC_KEYS,), -10 ** 9, np.int64)
    pos[:MAX_WINDOW] = np.arange(MAX_WINDOW)
    pos[MAX_WINDOW:MAX_WINDOW + DEC_T] = MAX_WINDOW + np.arange(DEC_T)
    t = np.arange(DEC_T)
    dist = (MAX_WINDOW + t)[:, None] - pos[None, :]
    cnt = np.zeros(dist.shape, np.float32)
    for d in DILATIONS:
        cnt += ((dist >= 0) & (dist % d == 0) & (dist <= ATT_SPAN * d)).astype(np.float32)
    return np.tile(cnt, (ATT_HEADS, 1))


def _attn_decode_block(q_ref, kn_ref, vn_ref, kt_ref, vt_ref, mult_ref, o_ref):
    lane = lax.broadcasted_iota(jnp.int32, (DEC_T, ATT_WIDTH), 1)
    head_masks = [jnp.logical_and(lane >= HEAD_DIM * h, lane < HEAD_DIM * (h + 1)) for h in range(ATT_HEADS)]
    mult = mult_ref[...]
    seen = mult > 0.0
    zpad = jnp.zeros((DEC_NEW_PAD - DEC_T, ATT_WIDTH), F32)
    for b in range(kt_ref.shape[0]):
        new_rows = slice(DEC_T * b, DEC_T * (b + 1))
        qb = q_ref[new_rows, :] * ATT_SCALE
        qbd = jnp.concatenate([jnp.where(hm, qb, 0.0) for hm in head_masks], axis=0).astype(BF16)
        kt = kt_ref[b].reshape(ATT_WIDTH, MAX_WINDOW).astype(BF16)
        kn = jnp.concatenate([kn_ref[new_rows, :], zpad], axis=0).astype(BF16)
        s = jnp.concatenate([jnp.dot(qbd, kt, preferred_element_type=F32),
                             lax.dot_general(qbd, kn, NT_DIMS, preferred_element_type=F32)], axis=1)
        s = jnp.where(seen, s, NEG_INF)
        m = jnp.max(s, axis=1, keepdims=True)
        p = mult * jnp.exp(s - m)
        p = (p / jnp.sum(p, axis=1, keepdims=True)).astype(BF16)
        vt = vt_ref[b].reshape(ATT_WIDTH, MAX_WINDOW).astype(BF16)
        vn = jnp.concatenate([vn_ref[new_rows, :], zpad], axis=0).astype(BF16)
        o = (lax.dot_general(p[:, :MAX_WINDOW], vt, NT_DIMS, preferred_element_type=F32)
             + jnp.dot(p[:, MAX_WINDOW:], vn, preferred_element_type=F32))
        out = jnp.zeros((DEC_T, ATT_WIDTH), F32)
        for h, hm in enumerate(head_masks):
            out = out + jnp.where(hm, o[DEC_T * h:DEC_T * (h + 1), :], 0.0)
        o_ref[new_rows, :] = out


def _attn_kernel(q_ref, kp_ref, kc_ref, vp_ref, vc_ref, qn_ref, kn_ref, vn_ref, kt_ref, vt_ref, mult_ref,
                 o_ref, od_ref, acc_ref, m_ref, l_ref, *, dec_blocks):
    sub = pl.program_id(3)
    step = ((pl.program_id(0) * pl.num_programs(1) + pl.program_id(1)) * pl.num_programs(2)
            + pl.program_id(2)) * ATT_PHASES + sub

    @pl.when(step < dec_blocks)
    def _():
        _attn_decode_block(qn_ref, kn_ref, vn_ref, kt_ref, vt_ref, mult_ref, od_ref)

    first_block = pl.program_id(2) == 0
    lane_lo = lax.broadcasted_iota(jnp.int32, (ATT_UNIT, LANES), 1) < HEAD_DIM
    qi = lax.broadcasted_iota(jnp.int32, (2 * ATT_UNIT, 2 * ATT_UNIT), 0) & (ATT_UNIT - 1)
    ki = lax.broadcasted_iota(jnp.int32, (2 * ATT_UNIT, 2 * ATT_UNIT), 1)
    is_prev = ki < ATT_UNIT
    dist = qi - ki + ATT_UNIT
    bias = jnp.where(dist >= 0, jnp.where(dist <= ATT_SPAN, 0.0, NEG_INF), NEG_INF)
    bias_first = bias + jnp.where(is_prev, jnp.where(first_block, NEG_INF, 0.0), 0.0)

    def rows(start, d):
        if d == 1:
            return pl.ds(start, ATT_UNIT)
        return pl.ds(start, ATT_UNIT, stride=d)

    def keep(bi, sl, o_t, m_t, l_t):
        acc_ref[bi, sl, :] = o_t
        m_ref[bi, sl, :] = m_t
        l_ref[bi, sl, :] = l_t

    def head_unit(bi, rho):
        d = DILATIONS[bi]
        cur = rows(rho, d)
        prev = rows(rho + ATT_BLK - d * ATT_UNIT, d)
        keep(bi, cur, *_attn_unit(q_ref[cur, :], kp_ref[prev, :], kc_ref[cur, :], vp_ref[prev, :],
                                  vc_ref[cur, :], bias_first, lane_lo))

    def inner_unit(bi, idx):
        d = DILATIONS[bi]
        rho = idx & (d - 1)
        j = 1 + (idx >> int(np.log2(d)))
        start = rho + d * ATT_UNIT * j
        cur = rows(start, d)
        prev = rows(start - d * ATT_UNIT, d)
        keep(bi, cur, *_attn_unit(q_ref[cur, :], kc_ref[prev, :], kc_ref[cur, :], vc_ref[prev, :],
                                  vc_ref[cur, :], bias, lane_lo))

    def chunks(unit, bi, base, lo, hi):
        def body(c, carry):
            for u in range(ATT_UNROLL):
                unit(bi, base + ATT_UNROLL * c + u)
            return carry
        lax.fori_loop(lo, hi, body, 0)

    @pl.when(sub == 0)
    def _():
        head_unit(0, 0)
        for idx in range(ATT_UNROLL - 1):
            inner_unit(0, idx)
        chunks(inner_unit, 0, ATT_UNROLL - 1, 0, 3)

    @pl.when(sub == 1)
    def _():
        for rho in range(ATT_UNROLL):
            head_unit(1, rho)
        chunks(inner_unit, 1, 0, 0, 3)

    @pl.when(sub == 2)
    def _():
        chunks(head_unit, 2, 0, 0, 4)

        def combine(c, carry):
            sl = pl.ds(pl.multiple_of(c * ATT_UNIT, ATT_UNIT), ATT_UNIT)
            ms = [m_ref[bi, sl, :] for bi in range(len(DILATIONS))]
            m = functools.reduce(jnp.maximum, ms)
            ws = [jnp.exp(mi - m) for mi in ms]
            num = sum(w * acc_ref[bi, sl, :] for bi, w in enumerate(ws))
            den = sum(w * l_ref[bi, sl, :] for bi, w in enumerate(ws))
            o_ref[sl, :] = num / den
            return carry

        lax.fori_loop(0, ATT_BLK // ATT_UNIT, combine, 0, unroll=2)


def _attn(pp, ps, cache_kt, cache_vt, layer, mult, batch):
    n = pp.shape[0]
    nb = n // batch // ATT_BLK
    npair = ATT_WIDTH // LANES
    nd = ps.shape[0]
    nseq = nd // DEC_T
    steps = batch * npair * nb * ATT_PHASES
    spb = -(-nseq // steps)
    assert nseq % spb == 0
    dec_blocks = nseq // spb

    def dec_blk(b, hp, i, s):
        return jnp.minimum(((b * npair + hp) * nb + i) * ATT_PHASES + s, dec_blocks - 1)

    out = pl.BlockSpec((ATT_BLK, LANES), lambda b, hp, i, s: (b * nb + i, hp))
    cur = lambda name: pl.BlockSpec((ATT_BLK, LANES), lambda b, hp, i, s: (b * nb + i, _pcol(name, LANES) + hp))
    prev = lambda name: pl.BlockSpec(
        (ATT_BLK, LANES), lambda b, hp, i, s: (b * nb + jnp.maximum(i - 1, 0), _pcol(name, LANES) + hp))
    new = lambda name: pl.BlockSpec((DEC_T * spb, ATT_WIDTH), lambda b, hp, i, s: (dec_blk(b, hp, i, s), _pcol(name)))
    win = pl.BlockSpec((None, spb, ATT_HEADS, HEAD_DIM, MAX_WINDOW),
                       lambda b, hp, i, s: (layer, dec_blk(b, hp, i, s), 0, 0, 0))
    scratch = pltpu.VMEM((len(DILATIONS), ATT_BLK, LANES), F32)
    return pl.pallas_call(
        functools.partial(_attn_kernel, dec_blocks=dec_blocks),
        grid=(batch, npair, nb, ATT_PHASES),
        in_specs=[cur("q"), prev("k"), cur("k"), prev("v"), cur("v"), new("q"), new("k"), new("v"), win, win,
                  _const_spec((DEC_ROWS, DEC_KEYS))],
        out_specs=[out, pl.BlockSpec((DEC_T * spb, ATT_WIDTH), lambda b, hp, i, s: (dec_blk(b, hp, i, s), 0))],
        out_shape=[jax.ShapeDtypeStruct((n, ATT_WIDTH), F32), jax.ShapeDtypeStruct((nd, ATT_WIDTH), F32)],
        scratch_shapes=[scratch, scratch, scratch],
        compiler_params=_cparams(("arbitrary", "arbitrary", "arbitrary", "arbitrary")),
        name="attn",
    )(pp, pp, pp, pp, pp, ps, ps, ps, cache_kt, cache_vt, mult)


HPG = SSD_HEADS // SSD_GROUPS
B_OFF = SSD_WIDTH
C_OFF = SSD_WIDTH + SSD_GROUPS * SSD_STATE


def _ssd_chunk_diag(xbc, dt, acum, pair_ok):
    acum_t = acum.T
    ys, xrs = [], []
    for g in range(SSD_GROUPS):
        bg = xbc[:, B_OFF + g * SSD_STATE:B_OFF + (g + 1) * SSD_STATE].astype(BF16)
        cg = xbc[:, C_OFF + g * SSD_STATE:C_OFF + (g + 1) * SSD_STATE].astype(BF16)
        cb = lax.dot_general(cg, bg, NT_DIMS, preferred_element_type=F32)
        for h in range(g * HPG, (g + 1) * HPG):
            xr = xbc[:, h * SSD_HEAD_DIM:(h + 1) * SSD_HEAD_DIM] * dt[:, h:h + 1]
            diff = acum[:, h:h + 1] - acum_t[h:h + 1, :]
            lmat = jnp.exp(jnp.where(pair_ok, diff, NEG_INF))
            ys.append(jnp.dot((cb * lmat).astype(BF16), xr.astype(BF16), preferred_element_type=F32))
            xrs.append(xr)
    return ys, xrs


def _ssd_finish(y, xs, z, dskip_ref, norm_ref):
    y = y + dskip_ref[...] * xs
    y = y * _silu(z)
    return _rms(y, norm_ref[...])


DT_COPIES = 3
SSD_PAIRS = SSD_HEADS // 2
SSD_CONV_ROWS = 128


def _ssd_prompt_kernel(xbc_ref, z_ref, dt_ref, cw_ref, cb_ref, dtb_ref, alog_ref, dskip_ref, norm_ref,
                       out_ref, state_ref, tail_ref, xc_ref, st_ref):
    tb = xbc_ref.shape[0]

    @pl.when(pl.program_id(1) == 0)
    def _():
        tail_ref[...] = jnp.zeros_like(tail_ref)
        st_ref[...] = jnp.zeros_like(st_ref)

    for r0 in range(0, tb, SSD_CONV_ROWS):
        xc_ref[r0:r0 + SSD_CONV_ROWS, :] = _silu(
            _conv_block_carry(xbc_ref[r0:r0 + SSD_CONV_ROWS, :], tail_ref, cw_ref, cb_ref))
    a_neg = -jnp.exp(alog_ref[...])
    row = _row_index((SSD_CHUNK, LANES))
    lane = lax.broadcasted_iota(jnp.int32, (SSD_CHUNK, LANES), 1)
    lane_lo = lane < SSD_HEAD_DIM
    causal = row >= lane
    zero = jnp.zeros((SSD_CHUNK, LANES), F32)

    def pair_rows(t):
        return jnp.concatenate([jnp.where(lane_lo, t, zero), jnp.where(lane_lo, zero, t)], axis=0).astype(BF16)

    def chunk(c, carry):
        r0 = pl.multiple_of(c * SSD_CHUNK, SSD_CHUNK)
        rows = pl.ds(r0, SSD_CHUNK)
        xc = xc_ref[rows, :]
        dt = _softplus(dt_ref[rows, :] + dtb_ref[...])
        acum = _cumsum_rows(dt * a_neg, row, 7)
        last = acum[SSD_CHUNK - 1:SSD_CHUNK, :]
        tot = jnp.exp(last)
        pt = jnp.where(lane < SUBLANES, acum, jnp.where(lane < 2 * SUBLANES, dt, dt * jnp.exp(last - acum))).T
        bs = [xc[:, B_OFF + g * SSD_STATE:B_OFF + (g + 1) * SSD_STATE] for g in range(SSD_GROUPS)]
        cs = [xc[:, C_OFF + g * SSD_STATE:C_OFF + (g + 1) * SSD_STATE].astype(BF16) for g in range(SSD_GROUPS)]
        bts = [b.T for b in bs]
        cbs = [lax.dot_general(cs[g], bs[g].astype(BF16), NT_DIMS, preferred_element_type=F32)
               for g in range(SSD_GROUPS)]
        gs, eacs, btws = [], [], []
        for h in range(SSD_HEADS):
            g = h // HPG
            a_col = jnp.broadcast_to(acum[:, h:h + 1], (SSD_CHUNK, SSD_CHUNK))
            lmat = jnp.exp(jnp.where(causal, a_col - pt[h:h + 1, :], NEG_INF))
            gs.append((cbs[g] * lmat * pt[SUBLANES + h:SUBLANES + h + 1, :]).astype(BF16))
            eacs.append(jnp.exp(a_col))
            btws.append((bts[g] * pt[2 * SUBLANES + h:2 * SUBLANES + h + 1, :]).astype(BF16))
        outs = []
        for k in range(SSD_PAIRS):
            h0, h1 = 2 * k, 2 * k + 1
            g0, g1 = h0 // HPG, h1 // HPG
            x2 = pair_rows(xc[:, k * LANES:(k + 1) * LANES])
            y = jnp.dot(jnp.concatenate([gs[h0], gs[h1]], axis=1), x2, preferred_element_type=F32)
            st = st_ref[k]
            if g0 == g1:
                y_off = jnp.dot(cs[g0], st.astype(BF16), preferred_element_type=F32)
            else:
                y_off = jnp.dot(jnp.concatenate([cs[g0], cs[g1]], axis=1), pair_rows(st),
                                preferred_element_type=F32)
            outs.append(y + y_off * jnp.where(lane_lo, eacs[h0], eacs[h1]))
            upd = jnp.dot(jnp.concatenate([btws[h0], btws[h1]], axis=1), x2, preferred_element_type=F32)
            st_ref[k] = st * jnp.where(lane_lo, tot[:, h0:h0 + 1], tot[:, h1:h1 + 1]) + upd
        y = jnp.concatenate(outs, axis=1)
        out_ref[rows, :] = _ssd_finish(y, xc[:, :SSD_WIDTH], z_ref[rows, :], dskip_ref, norm_ref)
        return carry

    lax.fori_loop(0, tb // SSD_CHUNK, chunk, 0)
    for k in range(SSD_PAIRS):
        t = st_ref[k].T
        state_ref[2 * k] = t[:SSD_HEAD_DIM]
        state_ref[2 * k + 1] = t[SSD_HEAD_DIM:]


def _ssd_prompt(p, cw, cb, dtb, alog, dskip, norm, batch, tb):
    n = p.shape[0]
    nb = n // batch // tb
    blk = lambda width: pl.BlockSpec((tb, width), lambda b, i: (b * nb + i, 0))
    col = lambda name: pl.BlockSpec((tb, P_COLS[name][1]), lambda b, i: (b * nb + i, _pcol(name)))
    return pl.pallas_call(
        _ssd_prompt_kernel,
        grid=(batch, nb),
        in_specs=[col("xbc"), col("z"), col("dt"),
                  _const_spec((CONV_W, SSD_CONV_CH)), _const_spec((1, SSD_CONV_CH)),
                  _const_spec((1, LANES)), _const_spec((1, LANES)),
                  _const_spec((1, SSD_WIDTH)), _const_spec((1, SSD_WIDTH))],
        out_specs=[blk(SSD_WIDTH),
                   pl.BlockSpec((None, SSD_HEADS, SSD_HEAD_DIM, SSD_STATE), lambda b, i: (b, 0, 0, 0))],
        out_shape=[jax.ShapeDtypeStruct((n, SSD_WIDTH), F32),
                   jax.ShapeDtypeStruct((batch, SSD_HEADS, SSD_HEAD_DIM, SSD_STATE), F32)],
        scratch_shapes=[pltpu.VMEM((SUBLANES, SSD_CONV_CH), F32),
                        pltpu.VMEM((tb, SSD_CONV_CH), F32),
                        pltpu.VMEM((SSD_PAIRS, SSD_STATE, LANES), F32)],
        compiler_params=_cparams(("parallel", "arbitrary")),
        name="ssd_prompt",
    )(p, p, p, cw, cb, dtb, alog, dskip, norm)


SSD_DEC_SEQ = SSD_CHUNK // DEC_T


def _ssd_decode_kernel(xbc_ref, z_ref, dt_ref, buf_ref, h0_ref, cw_ref, cb_ref, dtb_ref, alog_ref, dskip_ref,
                       norm_ref, out_ref, hnew_ref, xc_ref, xrd_ref, eac_ref, tot_ref, yoff_ref):
    xc = _silu(_conv_groups(xbc_ref[...], buf_ref[...], cw_ref, cb_ref))
    xc_ref[...] = xc
    a_neg = -jnp.exp(alog_ref[...])
    row = _row_index((SSD_CHUNK, LANES), DEC_T)
    li = lax.broadcasted_iota(jnp.int32, (SSD_CHUNK, SSD_CHUNK), 0)
    si = lax.broadcasted_iota(jnp.int32, (SSD_CHUNK, SSD_CHUNK), 1)
    same_seq_causal = jnp.logical_and(li >= si, (li - si) <= (li & (DEC_T - 1)))
    dt = _softplus(dt_ref[...] + dtb_ref[...])
    dta = dt * a_neg
    acum = _cumsum_rows(dta, row, 3)
    rest = _suffix_sum_rows(dta, row, DEC_T, 3)
    eac_ref[...] = jnp.exp(acum)
    tot_ref[...] = jnp.exp(acum + rest)
    decay = jnp.exp(rest)
    ys, xrs = _ssd_chunk_diag(xc, dt, acum, same_seq_causal)
    xrd_ref[...] = jnp.concatenate([xrs[h] * decay[:, h:h + 1] for h in range(SSD_HEADS)], axis=1).T
    seq_of_lane = lax.shift_right_logical(lax.broadcasted_iota(jnp.int32, (SSD_HEAD_DIM, SSD_CHUNK), 1), 3)
    b_all = [xc[:, B_OFF + g * SSD_STATE:B_OFF + (g + 1) * SSD_STATE].astype(BF16) for g in range(SSD_GROUPS)]

    def seq(b, carry):
        r0 = pl.multiple_of(b * DEC_T, DEC_T)
        rows = pl.ds(r0, DEC_T)
        xcb = xc_ref[rows, :]
        eac = eac_ref[rows, :]
        tot = tot_ref[rows, :]
        own = seq_of_lane == b
        outs = []
        for h in range(SSD_HEADS):
            g = h // HPG
            cg = xcb[:, C_OFF + g * SSD_STATE:C_OFF + (g + 1) * SSD_STATE].astype(BF16)
            prev = h0_ref[b, h]
            outs.append(lax.dot_general(cg, prev.astype(BF16), NT_DIMS, preferred_element_type=F32)
                        * eac[:, h:h + 1])
            lhs = jnp.where(own, xrd_ref[h * SSD_HEAD_DIM:(h + 1) * SSD_HEAD_DIM, :], 0.0).astype(BF16)
            st = jnp.dot(lhs, b_all[g], preferred_element_type=F32)
            hnew_ref[b, h] = prev * tot[0:1, h:h + 1] + st
        yoff_ref[rows, :] = jnp.concatenate(outs, axis=1)
        return carry

    lax.fori_loop(0, SSD_DEC_SEQ, seq, 0, unroll=4)
    y = jnp.concatenate(ys, axis=1) + yoff_ref[...]
    out_ref[...] = _ssd_finish(y, xc[:, :SSD_WIDTH], z_ref[...], dskip_ref, norm_ref)


def _ssd_decode(p, bufpad, state, layer, cw, cb, dtb, alog, dskip, norm):
    n = p.shape[0]
    blk = lambda width: pl.BlockSpec((SSD_CHUNK, width), lambda i: (i, 0))
    col = lambda name: pl.BlockSpec((SSD_CHUNK, P_COLS[name][1]), lambda i: (i, _pcol(name)))
    st_in = pl.BlockSpec((None, SSD_DEC_SEQ, SSD_HEADS, SSD_HEAD_DIM, SSD_STATE), lambda i: (layer, i, 0, 0, 0))
    st_out = pl.BlockSpec((SSD_DEC_SEQ, SSD_HEADS, SSD_HEAD_DIM, SSD_STATE), lambda i: (i, 0, 0, 0))
    return pl.pallas_call(
        _ssd_decode_kernel,
        grid=(n // SSD_CHUNK,),
        in_specs=[col("xbc"), col("z"), col("dt"), blk(SSD_CONV_CH), st_in,
                  _const_spec((CONV_W, SSD_CONV_CH)), _const_spec((1, SSD_CONV_CH)),
                  _const_spec((1, LANES)), _const_spec((1, LANES)),
                  _const_spec((1, SSD_WIDTH)), _const_spec((1, SSD_WIDTH))],
        out_specs=[blk(SSD_WIDTH), st_out],
        out_shape=[jax.ShapeDtypeStruct((n, SSD_WIDTH), F32),
                   jax.ShapeDtypeStruct((n // DEC_T, SSD_HEADS, SSD_HEAD_DIM, SSD_STATE), F32)],
        scratch_shapes=[pltpu.VMEM((SSD_CHUNK, SSD_CONV_CH), F32),
                        pltpu.VMEM((SSD_WIDTH, SSD_CHUNK), F32),
                        pltpu.VMEM((SSD_CHUNK, LANES), F32),
                        pltpu.VMEM((SSD_CHUNK, LANES), F32),
                        pltpu.VMEM((SSD_CHUNK, SSD_WIDTH), F32)],
        compiler_params=_cparams(("parallel",)),
        name="ssd_decode",
    )(p, p, p, bufpad, state, cw, cb, dtb, alog, dskip, norm)


DENSE_TM = 512
MIX_A_TB = 512
SSD_TB = 1024


def _pad_state_rows(buf):
    l, b, r, c = buf.shape
    return jnp.pad(buf, ((0, 0), (0, 0), (0, DEC_T - r), (0, 0))).reshape(l, b * DEC_T, c)


def kernel(x_prompt, x_sample, state_lru_h, state_lru_conv, cache_swa_k, cache_swa_v, state_ssd, state_ssd_conv,
           norm_mix_in, norm_mix_out, w_in, conv_a_w, conv_a_b, lru_wa, lru_ba, lru_wx, lru_bx, lru_lambda,
           conv_c_w, conv_c_b, dt_bias, a_log, d_skip, ssm_norm, w_out, norm_ffn_in, norm_ffn_out,
           w_gate_up, w_down):
    bp, seq, _ = x_prompt.shape
    bs, dec_t, _ = x_sample.shape
    assert dec_t == DEC_T and seq % ATT_BLK == 0 and seq >= MAX_WINDOW and bs % SSD_DEC_SEQ == 0

    def dt_lanes(p):
        slot = jnp.pad(p, [(0, 0)] * (p.ndim - 1) + [(0, SUBLANES - SSD_HEADS)])
        rep = jnp.concatenate([slot] * DT_COPIES, axis=-1)
        return jnp.pad(rep, [(0, 0)] * (p.ndim - 1) + [(0, LANES - DT_COPIES * SUBLANES)])

    def w_cols(name):
        off, width = W_IN_COLS[name]
        blk = w_in[:, :, off:off + width]
        if name == "dt":
            return dt_lanes(blk)
        return jnp.pad(blk, ((0, 0), (0, 0), (0, P_COLS[name][1] - width)))

    w_in_b = jnp.concatenate([w_cols(name) for name in P_COLS], axis=2).astype(BF16)
    w_out_b = w_out.astype(BF16)
    w_gu_b = w_gate_up.astype(BF16)
    w_dn_b = w_down.astype(BF16)
    eye = jnp.eye(LRU_BLOCKS, dtype=F32)

    def block_diag(w):
        return (w[:, :, :, None, :] * eye[None, :, None, :, None]).reshape(
            DEPTH, LRU_WIDTH, LRU_WIDTH).astype(BF16)

    wa_bd = block_diag(lru_wa)
    wx_bd = block_diag(lru_wx)
    vec = lambda p: p[:, None, :]
    dtb_p = dt_lanes(dt_bias)[:, None, :]
    alog_p = dt_lanes(a_log)[:, None, :]
    dskip_p = jnp.repeat(d_skip, SSD_HEAD_DIM, axis=1)[:, None, :]
    buf_a = _pad_state_rows(state_lru_conv)
    buf_c = _pad_state_rows(state_ssd_conv)
    h0_a = jnp.pad(state_lru_h[:, :, None, :], ((0, 0), (0, 0), (0, DEC_T - 1), (0, 0))).reshape(
        DEPTH, bs * DEC_T, LRU_WIDTH)
    mult = jnp.asarray(_decode_key_multiplicity())
    cache_kt = jnp.transpose(cache_swa_k, (0, 1, 3, 4, 2))
    cache_vt = jnp.transpose(cache_swa_v, (0, 1, 3, 4, 2))

    yp = x_prompt.reshape(bp * seq, D_MODEL)
    ys = x_sample.reshape(bs * DEC_T, D_MODEL)
    p_new = [[] for _ in range(6)]
    s_new = [[] for _ in range(6)]
    for l in range(DEPTH):
        a_args = (conv_a_w[l], vec(conv_a_b)[l], wa_bd[l], vec(lru_ba)[l], wx_bd[l], vec(lru_bx)[l],
                  vec(lru_lambda)[l])
        c_args = (conv_c_w[l], vec(conv_c_b)[l], dtb_p[l], alog_p[l], dskip_p[l], vec(ssm_norm)[l])
        g_in, g_out = vec(norm_mix_in)[l], vec(norm_mix_out)[l]
        f_in, f_out = vec(norm_ffn_in)[l], vec(norm_ffn_out)[l]

        post_w = (w_out_b[l], g_out, f_in, w_gu_b[l], w_dn_b[l], f_out)

        pp, k_tail, v_tail = _in_proj(yp, g_in, w_in_b[l], bp, DENSE_TM, with_tail=True)
        (ps,) = _in_proj(ys, g_in, w_in_b[l], 1, DENSE_TM, with_tail=False)
        out_b, out_b_s = _attn(pp, ps, cache_kt, cache_vt, l, mult, bp)

        out_a, h_last = _mix_a_prompt(pp, *a_args, batch=bp, tb=MIX_A_TB)
        out_c, ssd_state = _ssd_prompt(pp, *c_args, batch=bp, tb=SSD_TB)
        yp = _post(out_a, out_b, out_c, yp, *post_w, tm=DENSE_TM)
        pp3 = pp.reshape(bp, seq, N_IN_PAD)
        last = slice(seq - (CONV_W - 1), seq)
        tail_view = lambda t: jnp.transpose(t.reshape(bp, ATT_HEADS, HEAD_DIM, MAX_WINDOW), (0, 3, 1, 2))
        p_new[0].append(h_last.reshape(bp, LRU_WIDTH))
        p_new[1].append(pp3[:, last, P_COLS["xa"][0]:P_COLS["xa"][0] + LRU_WIDTH])
        p_new[2].append(tail_view(k_tail))
        p_new[3].append(tail_view(v_tail))
        p_new[4].append(ssd_state)
        p_new[5].append(pp3[:, last, P_COLS["xbc"][0]:P_COLS["xbc"][0] + SSD_CONV_CH])

        out_a, h_all = _mix_a_decode(ps, buf_a[l], h0_a[l], *a_args, tm=DENSE_TM)
        out_c, ssd_state = _ssd_decode(ps, buf_c[l], state_ssd, l, *c_args)
        ys = _post(out_a, out_b_s, out_c, ys, *post_w, tm=DENSE_TM)
        ps3 = ps.reshape(bs, DEC_T, N_IN_PAD)
        last = slice(DEC_T - (CONV_W - 1), DEC_T)
        new_kv = lambda name: ps3[:, :, P_COLS[name][0]:P_COLS[name][0] + ATT_WIDTH].reshape(
            bs, DEC_T, ATT_HEADS, HEAD_DIM)
        s_new[0].append(h_all.reshape(bs, DEC_T, LRU_WIDTH)[:, DEC_T - 1])
        s_new[1].append(ps3[:, last, P_COLS["xa"][0]:P_COLS["xa"][0] + LRU_WIDTH])
        s_new[2].append(new_kv("k"))
        s_new[3].append(new_kv("v"))
        s_new[4].append(ssd_state)
        s_new[5].append(ps3[:, last, P_COLS["xbc"][0]:P_COLS["xbc"][0] + SSD_CONV_CH])

    outs_p = [jnp.stack(a) for a in p_new]
    outs_s = [jnp.stack(a) for a in s_new]
    return (yp.reshape(bp, seq, D_MODEL), ys.reshape(bs, DEC_T, D_MODEL), *outs_p, *outs_s)
```

```python
import functools

import numpy as np
import jax
import jax.numpy as jnp
from jax import lax
from jax.experimental import pallas as pl
from jax.experimental.pallas import tpu as pltpu

F32 = jnp.float32
BF16 = jnp.bfloat16

D_MODEL = 1024
DEPTH = 4
CONV_W = 4
HEAD_DIM = 64
ATT_WIDTH = 384
ATT_HEADS = 6
ATT_SPAN = 128
DILATIONS = (1, 4, 16)
MAX_WINDOW = 2048
ATT_SCALE = HEAD_DIM ** -0.5
SSD_WIDTH = 384
SSD_HEADS = 6
SSD_HEAD_DIM = 64
SSD_GROUPS = 2
SSD_STATE = 128
SSD_CHUNK = 128
SSD_CONV_CH = 896
LRU_WIDTH = 256
LRU_BLOCKS = 4
LRU_C = 8.0
D_FF = 2816
N_IN = 2950
EPS = 1e-6

LANES = 128
SUBLANES = 8
N_IN_PAD = 3072
VMEM_LIMIT = 56 * 1024 * 1024

W_IN_COLS = {"ga": (0, 256), "xa": (256, 256), "q": (512, 384), "k": (896, 384), "v": (1280, 384),
             "z": (1664, 384), "xbc": (2048, 896), "dt": (2944, 6)}
P_COLS = {"xbc": (0, 896), "dt": (896, 128), "ga": (1024, 256), "xa": (1280, 256), "q": (1536, 384),
          "k": (1920, 384), "v": (2304, 384), "z": (2688, 384)}


def _pcol(name, width=None):
    off, w = P_COLS[name]
    width = width or w
    assert off % width == 0
    return off // width

NEG_INF = float("-inf")
NT_DIMS = (((1,), (1,)), ((), ()))


def _cparams(sem):
    return pltpu.CompilerParams(dimension_semantics=sem, vmem_limit_bytes=VMEM_LIMIT)


def _const_spec(shape, layer=None):
    nd = len(shape)
    if layer is None:
        return pl.BlockSpec(shape, lambda *_: (0,) * nd, pipeline_mode=pl.Buffered(1))
    return pl.BlockSpec((None,) + tuple(shape), lambda *_: (layer,) + (0,) * nd, pipeline_mode=pl.Buffered(1))


def _rms(x, g):
    ms = jnp.mean(x * x, axis=-1, keepdims=True)
    return x * lax.rsqrt(ms + EPS) * g


def _sigmoid(x):
    return jax.nn.sigmoid(x)


def _silu(x):
    return x * jax.nn.sigmoid(x)


def _softplus(x):
    return jnp.maximum(x, 0.0) + jnp.log1p(jnp.exp(-jnp.abs(x)))


def _gelu_tanh(x):
    c = np.sqrt(2.0 / np.pi).astype(np.float32)
    return 0.5 * x * (1.0 + jnp.tanh(c * (x + 0.044715 * (x * x * x))))


def _roll_rows(x, shift):
    n = x.shape[0]
    shift = shift % n
    if shift == 0:
        return x
    return pltpu.roll(x, shift, 0)


def _row_index(shape, group=None):
    r = lax.broadcasted_iota(jnp.int32, shape, 0)
    if group is not None:
        r = jnp.bitwise_and(r, group - 1)
    return r


def _lin_scan(a, u, row, steps):
    s = 1
    for _ in range(steps):
        keep = row >= s
        a_sh = jnp.where(keep, _roll_rows(a, s), 1.0)
        u_sh = jnp.where(keep, _roll_rows(u, s), 0.0)
        u = a * u_sh + u
        a = a * a_sh
        s *= 2
    return a, u


def _cumsum_rows(x, row, steps):
    s = 1
    for _ in range(steps):
        x = x + jnp.where(row >= s, _roll_rows(x, s), 0.0)
        s *= 2
    return x


def _suffix_sum_rows(x, row, group, steps):
    incl = x
    s = 1
    for _ in range(steps):
        incl = incl + jnp.where(row < group - s, _roll_rows(incl, -s), 0.0)
        s *= 2
    return incl - x


def _conv_taps(x, shifted_fn, w_ref, b_ref):
    y = b_ref[...] + w_ref[CONV_W - 1:CONV_W, :] * x
    for s in range(1, CONV_W):
        y = y + w_ref[CONV_W - 1 - s:CONV_W - s, :] * shifted_fn(s)
    return y


def _conv_block_carry(x, tail_ref, w_ref, b_ref):
    tb, c = x.shape
    x3 = x.reshape(tb // SUBLANES, SUBLANES, c)
    sub = lax.broadcasted_iota(jnp.int32, (1, SUBLANES, c), 1)
    tail = tail_ref[...]

    def shifted(s):
        r = pltpu.roll(x3, s, 1)
        before = jnp.concatenate([pltpu.roll(tail, s, 0)[None], r[:-1]], axis=0)
        return jnp.where(sub < s, before, r)

    y = _conv_taps(x3, shifted, w_ref, b_ref)
    tail_ref[...] = x[tb - SUBLANES:tb]
    return y.reshape(tb, c)


def _conv_groups(x, bufpad, w_ref, b_ref):
    row = _row_index(x.shape, SUBLANES)
    return _conv_taps(
        x, lambda s: jnp.where(row >= s, _roll_rows(x, s), _roll_rows(bufpad, s - (CONV_W - 1))), w_ref, b_ref)


def _lru_gates(xc, wa_ref, ba_ref, wx_ref, bx_ref, lam_ref):
    xb = xc.astype(BF16)
    r = _sigmoid(jnp.dot(xb, wa_ref[...], preferred_element_type=F32) + ba_ref[...])
    ig = _sigmoid(jnp.dot(xb, wx_ref[...], preferred_element_type=F32) + bx_ref[...])
    log_a = (-LRU_C) * r * _softplus(-lam_ref[...])
    a = jnp.exp(log_a)
    t = jnp.tanh(log_a)
    u = jnp.sqrt(-2.0 * t / (1.0 - t)) * (ig * xc)
    return a, u


def _in_proj_kernel(x_ref, g_ref, w_ref, p_ref, *tail_refs, first_tail):
    h = _rms(x_ref[...], g_ref[...]).astype(BF16)
    p_ref[...] = jnp.dot(h, w_ref[...], preferred_element_type=F32)
    if tail_refs:
        kt_ref, vt_ref = tail_refs

        @pl.when(pl.program_id(1) >= first_tail)
        def _():
            k_off, v_off = P_COLS["k"][0], P_COLS["v"][0]
            kt_ref[...] = p_ref[:, k_off:k_off + ATT_WIDTH].T
            vt_ref[...] = p_ref[:, v_off:v_off + ATT_WIDTH].T


def _in_proj(x, g, w, layer, batch, tm, with_tail):
    n = x.shape[0]
    nt = n // batch // tm
    first_tail = nt - MAX_WINDOW // tm
    row = lambda width: pl.BlockSpec((tm, width), lambda b, j: (b * nt + j, 0))
    out_specs = [row(N_IN_PAD)]
    out_shape = [jax.ShapeDtypeStruct((n, N_IN_PAD), F32)]
    if with_tail:
        tail = pl.BlockSpec((None, ATT_WIDTH, tm), lambda b, j: (b, 0, jnp.maximum(j - first_tail, 0)))
        out_specs += [tail, tail]
        out_shape += [jax.ShapeDtypeStruct((batch, ATT_WIDTH, MAX_WINDOW), F32)] * 2
    return pl.pallas_call(
        functools.partial(_in_proj_kernel, first_tail=first_tail),
        grid=(batch, nt),
        in_specs=[row(D_MODEL), _const_spec((1, D_MODEL), layer), _const_spec((D_MODEL, N_IN_PAD), layer)],
        out_specs=out_specs,
        out_shape=out_shape,
        compiler_params=_cparams(("parallel", "arbitrary")),
        name="in_proj",
    )(x, g, w)


FFN_CHUNK = 256


def _post_kernel(a_ref, b_ref, c_ref, x_ref, wo_ref, go_ref, gi_ref, wgu_ref, wd_ref, gf_ref, o_ref):
    mixed = jnp.concatenate([a_ref[...], b_ref[...], c_ref[...]], axis=1).astype(BF16)
    x1 = x_ref[...] + _rms(jnp.dot(mixed, wo_ref[...], preferred_element_type=F32), go_ref[...])
    h = _rms(x1, gi_ref[...]).astype(BF16)
    acc = jnp.zeros(x1.shape, F32)
    for c in range(D_FF // FFN_CHUNK):
        lo = c * FFN_CHUNK
        g = jnp.dot(h, wgu_ref[:, lo:lo + FFN_CHUNK], preferred_element_type=F32)
        u = jnp.dot(h, wgu_ref[:, D_FF + lo:D_FF + lo + FFN_CHUNK], preferred_element_type=F32)
        act = (_silu(g) * u).astype(BF16)
        acc = acc + jnp.dot(act, wd_ref[lo:lo + FFN_CHUNK, :], preferred_element_type=F32)
    o_ref[...] = x1 + _rms(acc, gf_ref[...])


def _post(a, b, c, x, wo, go, gi, wgu, wd, gf, layer, tm):
    n = x.shape[0]
    row = lambda width: pl.BlockSpec((tm, width), lambda i: (i, 0))
    vec = _const_spec((1, D_MODEL), layer)
    return pl.pallas_call(
        _post_kernel,
        grid=(n // tm,),
        in_specs=[row(LRU_WIDTH), row(ATT_WIDTH), row(SSD_WIDTH), row(D_MODEL),
                  _const_spec((D_MODEL, D_MODEL), layer), vec, vec,
                  _const_spec((D_MODEL, 2 * D_FF), layer), _const_spec((D_FF, D_MODEL), layer), vec],
        out_specs=row(D_MODEL),
        out_shape=jax.ShapeDtypeStruct((n, D_MODEL), F32),
        compiler_params=_cparams(("parallel",)),
        name="post",
    )(a, b, c, x, wo, go, gi, wgu, wd, gf)


def _mix_a_prompt_kernel(xa_ref, ga_ref, cw_ref, cb_ref, wa_ref, ba_ref, wx_ref, bx_ref, lam_ref,
                         out_ref, hlast_ref, tail_ref, hc_ref):
    @pl.when(pl.program_id(1) == 0)
    def _():
        tail_ref[...] = jnp.zeros_like(tail_ref)
        hc_ref[...] = jnp.zeros_like(hc_ref)

    x = xa_ref[...]
    tb, c = x.shape
    xc = _conv_block_carry(x, tail_ref, cw_ref, cb_ref)
    a, u = _lru_gates(xc, wa_ref, ba_ref, wx_ref, bx_ref, lam_ref)
    gate = _gelu_tanh(ga_ref[...])
    nslab = tb // SUBLANES
    a3 = a.reshape(nslab, SUBLANES, c)
    u3 = u.reshape(nslab, SUBLANES, c)
    sub = lax.broadcasted_iota(jnp.int32, (1, SUBLANES, c), 1)
    s = 1
    while s < SUBLANES:
        keep = sub >= s
        a_sh = jnp.where(keep, pltpu.roll(a3, s, 1), 1.0)
        u_sh = jnp.where(keep, pltpu.roll(u3, s, 1), 0.0)
        u3 = a3 * u_sh + u3
        a3 = a3 * a_sh
        s *= 2
    h_row = hc_ref[0:1, :]
    for k in range(nslab):
        h = u3[k] + a3[k] * h_row
        h_row = h[SUBLANES - 1:SUBLANES, :]
        rows = slice(k * SUBLANES, (k + 1) * SUBLANES)
        out_ref[rows, :] = h * gate[rows, :]
    hc_ref[...] = jnp.broadcast_to(h_row, hc_ref.shape)
    hlast_ref[...] = h_row


def _mix_a_prompt(p, cw, cb, wa, ba, wx, bx, lam, layer, batch, tb):
    n = p.shape[0]
    nb = n // batch // tb
    blk = pl.BlockSpec((tb, LRU_WIDTH), lambda b, i: (b * nb + i, 0))
    col = lambda name: pl.BlockSpec((tb, LRU_WIDTH), lambda b, i: (b * nb + i, _pcol(name)))
    vec = _const_spec((1, LRU_WIDTH), layer)
    mat = _const_spec((LRU_WIDTH, LRU_WIDTH), layer)
    return pl.pallas_call(
        _mix_a_prompt_kernel,
        grid=(batch, nb),
        in_specs=[col("xa"), col("ga"), _const_spec((CONV_W, LRU_WIDTH), layer), vec, mat, vec, mat, vec, vec],
        out_specs=[blk, pl.BlockSpec((None, 1, LRU_WIDTH), lambda b, i: (b, 0, 0))],
        out_shape=[jax.ShapeDtypeStruct((n, LRU_WIDTH), F32),
                   jax.ShapeDtypeStruct((batch, 1, LRU_WIDTH), F32)],
        scratch_shapes=[pltpu.VMEM((SUBLANES, LRU_WIDTH), F32), pltpu.VMEM((SUBLANES, LRU_WIDTH), F32)],
        compiler_params=_cparams(("parallel", "arbitrary")),
        name="mix_a_prompt",
    )(p, p, cw, cb, wa, ba, wx, bx, lam)


def _mix_a_decode_kernel(xa_ref, ga_ref, buf_ref, h0_ref, cw_ref, cb_ref, wa_ref, ba_ref, wx_ref, bx_ref,
                         lam_ref, out_ref, h_ref):
    x = xa_ref[...]
    xc = _conv_groups(x, buf_ref[...], cw_ref, cb_ref)
    a, u = _lru_gates(xc, wa_ref, ba_ref, wx_ref, bx_ref, lam_ref)
    u = u + a * h0_ref[...]
    row = _row_index(x.shape, SUBLANES)
    _, h = _lin_scan(a, u, row, 3)
    h_ref[...] = h
    out_ref[...] = h * _gelu_tanh(ga_ref[...])


def _mix_a_decode(p, bufpad, h0pad, cw, cb, wa, ba, wx, bx, lam, layer, tm):
    n = p.shape[0]
    blk = pl.BlockSpec((tm, LRU_WIDTH), lambda i: (i, 0))
    col = lambda name: pl.BlockSpec((tm, LRU_WIDTH), lambda i: (i, _pcol(name)))
    lay = pl.BlockSpec((None, tm, LRU_WIDTH), lambda i: (layer, i, 0))
    vec = _const_spec((1, LRU_WIDTH), layer)
    mat = _const_spec((LRU_WIDTH, LRU_WIDTH), layer)
    return pl.pallas_call(
        _mix_a_decode_kernel,
        grid=(n // tm,),
        in_specs=[col("xa"), col("ga"), lay, lay, _const_spec((CONV_W, LRU_WIDTH), layer), vec, mat, vec, mat, vec,
                  vec],
        out_specs=[blk, blk],
        out_shape=[jax.ShapeDtypeStruct((n, LRU_WIDTH), F32), jax.ShapeDtypeStruct((n, LRU_WIDTH), F32)],
        compiler_params=_cparams(("parallel",)),
        name="mix_a_decode",
    )(p, p, bufpad, h0pad, cw, cb, wa, ba, wx, bx, lam)


ATT_BLK = 2048
ATT_UNIT = 128
ATT_UNROLL = 4


def _attn_unit(q_t, kp_t, kc_t, vp_t, vc_t, bias, lane_lo):
    zero = jnp.zeros_like(q_t)
    qs = q_t * ATT_SCALE
    qq = jnp.concatenate([jnp.where(lane_lo, qs, zero), jnp.where(lane_lo, zero, qs)], axis=0).astype(BF16)
    kk = jnp.concatenate([kp_t, kc_t], axis=0).astype(BF16)
    s = lax.dot_general(qq, kk, (((1,), (1,)), ((), ())), preferred_element_type=F32) + bias
    m = jnp.max(s, axis=1, keepdims=True)
    p = jnp.exp(s - m).astype(BF16)
    pcat = jnp.concatenate([p[:ATT_UNIT], p[ATT_UNIT:]], axis=1)
    one_lo = jnp.where(lane_lo, 1.0, 0.0)
    one_hi = 1.0 - one_lo
    w = jnp.concatenate([
        jnp.concatenate([jnp.where(lane_lo, vp_t, zero), one_lo], axis=1),
        jnp.concatenate([jnp.where(lane_lo, vc_t, zero), one_lo], axis=1),
        jnp.concatenate([jnp.where(lane_lo, zero, vp_t), one_hi], axis=1),
        jnp.concatenate([jnp.where(lane_lo, zero, vc_t), one_hi], axis=1)], axis=0).astype(BF16)
    ol = jnp.dot(pcat, w, preferred_element_type=F32)
    m_t = jnp.where(lane_lo, jnp.broadcast_to(m[:ATT_UNIT], q_t.shape), jnp.broadcast_to(m[ATT_UNIT:], q_t.shape))
    return ol[:, :LANES], m_t, ol[:, LANES:]


DEC_T = 8
DEC_NEW_PAD = 128
DEC_KEYS = MAX_WINDOW + DEC_NEW_PAD
DEC_ROWS = ATT_HEADS * DEC_T
ATT_PHASES = len(DILATIONS)


def _decode_key_multiplicity():
    pos = np.full((DEC_KEYS,), -10 ** 9, np.int64)
    pos[:MAX_WINDOW] = np.arange(MAX_WINDOW)
    pos[MAX_WINDOW:MAX_WINDOW + DEC_T] = MAX_WINDOW + np.arange(DEC_T)
    t = np.arange(DEC_T)
    dist = (MAX_WINDOW + t)[:, None] - pos[None, :]
    cnt = np.zeros(dist.shape, np.float32)
    for d in DILATIONS:
        cnt += ((dist >= 0) & (dist % d == 0) & (dist <= ATT_SPAN * d)).astype(np.float32)
    return np.tile(cnt, (ATT_HEADS, 1))


def _attn_decode_block(q_ref, kn_ref, vn_ref, kt_ref, vt_ref, mult_ref, o_ref):
    lane = lax.broadcasted_iota(jnp.int32, (DEC_T, ATT_WIDTH), 1)
    head_masks = [jnp.logical_and(lane >= HEAD_DIM * h, lane < HEAD_DIM * (h + 1)) for h in range(ATT_HEADS)]
    mult = mult_ref[...]
    seen = mult > 0.0
    zpad = jnp.zeros((DEC_NEW_PAD - DEC_T, ATT_WIDTH), F32)
    for b in range(kt_ref.shape[0]):
        new_rows = slice(DEC_T * b, DEC_T * (b + 1))
        qb = q_ref[new_rows, :] * ATT_SCALE
        qbd = jnp.concatenate([jnp.where(hm, qb, 0.0) for hm in head_masks], axis=0).astype(BF16)
        kt = kt_ref[b].reshape(ATT_WIDTH, MAX_WINDOW).astype(BF16)
        kn = jnp.concatenate([kn_ref[new_rows, :], zpad], axis=0).astype(BF16)
        s = jnp.concatenate([jnp.dot(qbd, kt, preferred_element_type=F32),
                             lax.dot_general(qbd, kn, NT_DIMS, preferred_element_type=F32)], axis=1)
        s = jnp.where(seen, s, NEG_INF)
        m = jnp.max(s, axis=1, keepdims=True)
        p = mult * jnp.exp(s - m)
        p = (p / jnp.sum(p, axis=1, keepdims=True)).astype(BF16)
        vt = vt_ref[b].reshape(ATT_WIDTH, MAX_WINDOW).astype(BF16)
        vn = jnp.concatenate([vn_ref[new_rows, :], zpad], axis=0).astype(BF16)
        o = (lax.dot_general(p[:, :MAX_WINDOW], vt, NT_DIMS, preferred_element_type=F32)
             + jnp.dot(p[:, MAX_WINDOW:], vn, preferred_element_type=F32))
        out = jnp.zeros((DEC_T, ATT_WIDTH), F32)
        for h, hm in enumerate(head_masks):
            out = out + jnp.where(hm, o[DEC_T * h:DEC_T * (h + 1), :], 0.0)
        o_ref[new_rows, :] = out


def _attn_kernel(q_ref, kp_ref, kc_ref, vp_ref, vc_ref, qn_ref, kn_ref, vn_ref, kt_ref, vt_ref, mult_ref,
                 o_ref, od_ref, acc_ref, m_ref, l_ref, *, dec_blocks):
    sub = pl.program_id(3)
    step = ((pl.program_id(0) * pl.num_programs(1) + pl.program_id(1)) * pl.num_programs(2)
            + pl.program_id(2)) * ATT_PHASES + sub

    @pl.when(step < dec_blocks)
    def _():
        _attn_decode_block(qn_ref, kn_ref, vn_ref, kt_ref, vt_ref, mult_ref, od_ref)

    first_block = pl.program_id(2) == 0
    lane_lo = lax.broadcasted_iota(jnp.int32, (ATT_UNIT, LANES), 1) < HEAD_DIM
    qi = lax.broadcasted_iota(jnp.int32, (2 * ATT_UNIT, 2 * ATT_UNIT), 0) & (ATT_UNIT - 1)
    ki = lax.broadcasted_iota(jnp.int32, (2 * ATT_UNIT, 2 * ATT_UNIT), 1)
    is_prev = ki < ATT_UNIT
    dist = qi - ki + ATT_UNIT
    bias = jnp.where(dist >= 0, jnp.where(dist <= ATT_SPAN, 0.0, NEG_INF), NEG_INF)
    bias_first = bias + jnp.where(is_prev, jnp.where(first_block, NEG_INF, 0.0), 0.0)

    def rows(start, d):
        if d == 1:
            return pl.ds(start, ATT_UNIT)
        return pl.ds(start, ATT_UNIT, stride=d)

    def keep(bi, sl, o_t, m_t, l_t):
        acc_ref[bi, sl, :] = o_t
        m_ref[bi, sl, :] = m_t
        l_ref[bi, sl, :] = l_t

    def head_unit(bi, rho):
        d = DILATIONS[bi]
        cur = rows(rho, d)
        prev = rows(rho + ATT_BLK - d * ATT_UNIT, d)
        keep(bi, cur, *_attn_unit(q_ref[cur, :], kp_ref[prev, :], kc_ref[cur, :], vp_ref[prev, :],
                                  vc_ref[cur, :], bias_first, lane_lo))

    def inner_unit(bi, idx):
        d = DILATIONS[bi]
        rho = idx & (d - 1)
        j = 1 + (idx >> int(np.log2(d)))
        start = rho + d * ATT_UNIT * j
        cur = rows(start, d)
        prev = rows(start - d * ATT_UNIT, d)
        keep(bi, cur, *_attn_unit(q_ref[cur, :], kc_ref[prev, :], kc_ref[cur, :], vc_ref[prev, :],
                                  vc_ref[cur, :], bias, lane_lo))

    def chunks(unit, bi, base, lo, hi):
        def body(c, carry):
            for u in range(ATT_UNROLL):
                unit(bi, base + ATT_UNROLL * c + u)
            return carry
        lax.fori_loop(lo, hi, body, 0)

    @pl.when(sub == 0)
    def _():
        head_unit(0, 0)
        for idx in range(ATT_UNROLL - 1):
            inner_unit(0, idx)
        chunks(inner_unit, 0, ATT_UNROLL - 1, 0, 3)

    @pl.when(sub == 1)
    def _():
        for rho in range(ATT_UNROLL):
            head_unit(1, rho)
        chunks(inner_unit, 1, 0, 0, 3)

    @pl.when(sub == 2)
    def _():
        chunks(head_unit, 2, 0, 0, 4)

        def combine(c, carry):
            sl = pl.ds(pl.multiple_of(c * ATT_UNIT, ATT_UNIT), ATT_UNIT)
            ms = [m_ref[bi, sl, :] for bi in range(len(DILATIONS))]
            m = functools.reduce(jnp.maximum, ms)
            ws = [jnp.exp(mi - m) for mi in ms]
            num = sum(w * acc_ref[bi, sl, :] for bi, w in enumerate(ws))
            den = sum(w * l_ref[bi, sl, :] for bi, w in enumerate(ws))
            o_ref[sl, :] = num / den
            return carry

        lax.fori_loop(0, ATT_BLK // ATT_UNIT, combine, 0, unroll=2)


def _attn(pp, ps, cache_kt, cache_vt, layer, mult, batch):
    n = pp.shape[0]
    nb = n // batch // ATT_BLK
    npair = ATT_WIDTH // LANES
    nd = ps.shape[0]
    nseq = nd // DEC_T
    steps = batch * npair * nb * ATT_PHASES
    spb = -(-nseq // steps)
    assert nseq % spb == 0
    dec_blocks = nseq // spb

    def dec_blk(b, hp, i, s):
        return jnp.minimum(((b * npair + hp) * nb + i) * ATT_PHASES + s, dec_blocks - 1)

    out = pl.BlockSpec((ATT_BLK, LANES), lambda b, hp, i, s: (b * nb + i, hp))
    cur = lambda name: pl.BlockSpec((ATT_BLK, LANES), lambda b, hp, i, s: (b * nb + i, _pcol(name, LANES) + hp))
    prev = lambda name: pl.BlockSpec(
        (ATT_BLK, LANES), lambda b, hp, i, s: (b * nb + jnp.maximum(i - 1, 0), _pcol(name, LANES) + hp))
    new = lambda name: pl.BlockSpec((DEC_T * spb, ATT_WIDTH), lambda b, hp, i, s: (dec_blk(b, hp, i, s), _pcol(name)))
    win = pl.BlockSpec((None, spb, ATT_HEADS, HEAD_DIM, MAX_WINDOW),
                       lambda b, hp, i, s: (layer, dec_blk(b, hp, i, s), 0, 0, 0))
    scratch = pltpu.VMEM((len(DILATIONS), ATT_BLK, LANES), F32)
    return pl.pallas_call(
        functools.partial(_attn_kernel, dec_blocks=dec_blocks),
        grid=(batch, npair, nb, ATT_PHASES),
        in_specs=[cur("q"), prev("k"), cur("k"), prev("v"), cur("v"), new("q"), new("k"), new("v"), win, win,
                  _const_spec((DEC_ROWS, DEC_KEYS))],
        out_specs=[out, pl.BlockSpec((DEC_T * spb, ATT_WIDTH), lambda b, hp, i, s: (dec_blk(b, hp, i, s), 0))],
        out_shape=[jax.ShapeDtypeStruct((n, ATT_WIDTH), F32), jax.ShapeDtypeStruct((nd, ATT_WIDTH), F32)],
        scratch_shapes=[scratch, scratch, scratch],
        compiler_params=_cparams(("arbitrary", "arbitrary", "arbitrary", "arbitrary")),
        name="attn",
    )(pp, pp, pp, pp, pp, ps, ps, ps, cache_kt, cache_vt, mult)


HPG = SSD_HEADS // SSD_GROUPS
B_OFF = SSD_WIDTH
C_OFF = SSD_WIDTH + SSD_GROUPS * SSD_STATE


def _ssd_chunk_diag(xbc, dt, acum, pair_ok):
    acum_t = acum.T
    ys, xrs = [], []
    for g in range(SSD_GROUPS):
        bg = xbc[:, B_OFF + g * SSD_STATE:B_OFF + (g + 1) * SSD_STATE].astype(BF16)
        cg = xbc[:, C_OFF + g * SSD_STATE:C_OFF + (g + 1) * SSD_STATE].astype(BF16)
        cb = lax.dot_general(cg, bg, NT_DIMS, preferred_element_type=F32)
        for h in range(g * HPG, (g + 1) * HPG):
            xr = xbc[:, h * SSD_HEAD_DIM:(h + 1) * SSD_HEAD_DIM] * dt[:, h:h + 1]
            diff = acum[:, h:h + 1] - acum_t[h:h + 1, :]
            lmat = jnp.exp(jnp.where(pair_ok, diff, NEG_INF))
            ys.append(jnp.dot((cb * lmat).astype(BF16), xr.astype(BF16), preferred_element_type=F32))
            xrs.append(xr)
    return ys, xrs


def _ssd_finish(y, xs, z, dskip_ref, norm_ref):
    y = y + dskip_ref[...] * xs
    y = y * _silu(z)
    return _rms(y, norm_ref[...])


DT_COPIES = 3
SSD_PAIRS = SSD_HEADS // 2
SSD_CONV_ROWS = 128


def _ssd_prompt_kernel(xbc_ref, z_ref, dt_ref, cw_ref, cb_ref, dtb_ref, alog_ref, dskip_ref, norm_ref,
                       out_ref, state_ref, tail_ref, xc_ref, st_ref):
    tb = xbc_ref.shape[0]

    @pl.when(pl.program_id(1) == 0)
    def _():
        tail_ref[...] = jnp.zeros_like(tail_ref)
        st_ref[...] = jnp.zeros_like(st_ref)

    for r0 in range(0, tb, SSD_CONV_ROWS):
        xc_ref[r0:r0 + SSD_CONV_ROWS, :] = _silu(
            _conv_block_carry(xbc_ref[r0:r0 + SSD_CONV_ROWS, :], tail_ref, cw_ref, cb_ref))
    a_neg = -jnp.exp(alog_ref[...])
    row = _row_index((SSD_CHUNK, LANES))
    lane = lax.broadcasted_iota(jnp.int32, (SSD_CHUNK, LANES), 1)
    lane_lo = lane < SSD_HEAD_DIM
    causal = row >= lane
    zero = jnp.zeros((SSD_CHUNK, LANES), F32)

    def pair_rows(t):
        return jnp.concatenate([jnp.where(lane_lo, t, zero), jnp.where(lane_lo, zero, t)], axis=0).astype(BF16)

    def chunk(c, carry):
        r0 = pl.multiple_of(c * SSD_CHUNK, SSD_CHUNK)
        rows = pl.ds(r0, SSD_CHUNK)
        xc = xc_ref[rows, :]
        dt = _softplus(dt_ref[rows, :] + dtb_ref[...])
        acum = _cumsum_rows(dt * a_neg, row, 7)
        last = acum[SSD_CHUNK - 1:SSD_CHUNK, :]
        tot = jnp.exp(last)
        pt = jnp.where(lane < SUBLANES, acum, jnp.where(lane < 2 * SUBLANES, dt, dt * jnp.exp(last - acum))).T
        bs = [xc[:, B_OFF + g * SSD_STATE:B_OFF + (g + 1) * SSD_STATE] for g in range(SSD_GROUPS)]
        cs = [xc[:, C_OFF + g * SSD_STATE:C_OFF + (g + 1) * SSD_STATE].astype(BF16) for g in range(SSD_GROUPS)]
        bts = [b.T for b in bs]
        cbs = [lax.dot_general(cs[g], bs[g].astype(BF16), NT_DIMS, preferred_element_type=F32)
               for g in range(SSD_GROUPS)]
        gs, eacs, btws = [], [], []
        for h in range(SSD_HEADS):
            g = h // HPG
            a_col = jnp.broadcast_to(acum[:, h:h + 1], (SSD_CHUNK, SSD_CHUNK))
            lmat = jnp.exp(jnp.where(causal, a_col - pt[h:h + 1, :], NEG_INF))
            gs.append((cbs[g] * lmat * pt[SUBLANES + h:SUBLANES + h + 1, :]).astype(BF16))
            eacs.append(jnp.exp(a_col))
            btws.append((bts[g] * pt[2 * SUBLANES + h:2 * SUBLANES + h + 1, :]).astype(BF16))
        outs = []
        for k in range(SSD_PAIRS):
            h0, h1 = 2 * k, 2 * k + 1
            g0, g1 = h0 // HPG, h1 // HPG
            x2 = pair_rows(xc[:, k * LANES:(k + 1) * LANES])
            y = jnp.dot(jnp.concatenate([gs[h0], gs[h1]], axis=1), x2, preferred_element_type=F32)
            st = st_ref[k]
            if g0 == g1:
                y_off = jnp.dot(cs[g0], st.astype(BF16), preferred_element_type=F32)
            else:
                y_off = jnp.dot(jnp.concatenate([cs[g0], cs[g1]], axis=1), pair_rows(st),
                                preferred_element_type=F32)
            outs.append(y + y_off * jnp.where(lane_lo, eacs[h0], eacs[h1]))
            upd = jnp.dot(jnp.concatenate([btws[h0], btws[h1]], axis=1), x2, preferred_element_type=F32)
            st_ref[k] = st * jnp.where(lane_lo, tot[:, h0:h0 + 1], tot[:, h1:h1 + 1]) + upd
        y = jnp.concatenate(outs, axis=1)
        out_ref[rows, :] = _ssd_finish(y, xc[:, :SSD_WIDTH], z_ref[rows, :], dskip_ref, norm_ref)
        return carry

    lax.fori_loop(0, tb // SSD_CHUNK, chunk, 0)
    for k in range(SSD_PAIRS):
        t = st_ref[k].T
        state_ref[2 * k] = t[:SSD_HEAD_DIM]
        state_ref[2 * k + 1] = t[SSD_HEAD_DIM:]


def _ssd_prompt(p, cw, cb, dtb, alog, dskip, norm, layer, batch, tb):
    n = p.shape[0]
    nb = n // batch // tb
    blk = lambda width: pl.BlockSpec((tb, width), lambda b, i: (b * nb + i, 0))
    col = lambda name: pl.BlockSpec((tb, P_COLS[name][1]), lambda b, i: (b * nb + i, _pcol(name)))
    return pl.pallas_call(
        _ssd_prompt_kernel,
        grid=(batch, nb),
        in_specs=[col("xbc"), col("z"), col("dt"),
                  _const_spec((CONV_W, SSD_CONV_CH), layer), _const_spec((1, SSD_CONV_CH), layer),
                  _const_spec((1, LANES), layer), _const_spec((1, LANES), layer),
                  _const_spec((1, SSD_WIDTH), layer), _const_spec((1, SSD_WIDTH), layer)],
        out_specs=[blk(SSD_WIDTH),
                   pl.BlockSpec((None, SSD_HEADS, SSD_HEAD_DIM, SSD_STATE), lambda b, i: (b, 0, 0, 0))],
        out_shape=[jax.ShapeDtypeStruct((n, SSD_WIDTH), F32),
                   jax.ShapeDtypeStruct((batch, SSD_HEADS, SSD_HEAD_DIM, SSD_STATE), F32)],
        scratch_shapes=[pltpu.VMEM((SUBLANES, SSD_CONV_CH), F32),
                        pltpu.VMEM((tb, SSD_CONV_CH), F32),
                        pltpu.VMEM((SSD_PAIRS, SSD_STATE, LANES), F32)],
        compiler_params=_cparams(("parallel", "arbitrary")),
        name="ssd_prompt",
    )(p, p, p, cw, cb, dtb, alog, dskip, norm)


SSD_DEC_SEQ = SSD_CHUNK // DEC_T


def _ssd_decode_kernel(xbc_ref, z_ref, dt_ref, buf_ref, h0_ref, cw_ref, cb_ref, dtb_ref, alog_ref, dskip_ref,
                       norm_ref, out_ref, hnew_ref, xc_ref, xrd_ref, eac_ref, tot_ref, yoff_ref):
    xc = _silu(_conv_groups(xbc_ref[...], buf_ref[...], cw_ref, cb_ref))
    xc_ref[...] = xc
    a_neg = -jnp.exp(alog_ref[...])
    row = _row_index((SSD_CHUNK, LANES), DEC_T)
    li = lax.broadcasted_iota(jnp.int32, (SSD_CHUNK, SSD_CHUNK), 0)
    si = lax.broadcasted_iota(jnp.int32, (SSD_CHUNK, SSD_CHUNK), 1)
    same_seq_causal = jnp.logical_and(li >= si, (li - si) <= (li & (DEC_T - 1)))
    dt = _softplus(dt_ref[...] + dtb_ref[...])
    dta = dt * a_neg
    acum = _cumsum_rows(dta, row, 3)
    rest = _suffix_sum_rows(dta, row, DEC_T, 3)
    eac_ref[...] = jnp.exp(acum)
    tot_ref[...] = jnp.exp(acum + rest)
    decay = jnp.exp(rest)
    ys, xrs = _ssd_chunk_diag(xc, dt, acum, same_seq_causal)
    xrd_ref[...] = jnp.concatenate([xrs[h] * decay[:, h:h + 1] for h in range(SSD_HEADS)], axis=1).T
    seq_of_lane = lax.shift_right_logical(lax.broadcasted_iota(jnp.int32, (SSD_HEAD_DIM, SSD_CHUNK), 1), 3)
    b_all = [xc[:, B_OFF + g * SSD_STATE:B_OFF + (g + 1) * SSD_STATE].astype(BF16) for g in range(SSD_GROUPS)]

    def seq(b, carry):
        r0 = pl.multiple_of(b * DEC_T, DEC_T)
        rows = pl.ds(r0, DEC_T)
        xcb = xc_ref[rows, :]
        eac = eac_ref[rows, :]
        tot = tot_ref[rows, :]
        own = seq_of_lane == b
        outs = []
        for h in range(SSD_HEADS):
            g = h // HPG
            cg = xcb[:, C_OFF + g * SSD_STATE:C_OFF + (g + 1) * SSD_STATE].astype(BF16)
            prev = h0_ref[b, h]
            outs.append(lax.dot_general(cg, prev.astype(BF16), NT_DIMS, preferred_element_type=F32)
                        * eac[:, h:h + 1])
            lhs = jnp.where(own, xrd_ref[h * SSD_HEAD_DIM:(h + 1) * SSD_HEAD_DIM, :], 0.0).astype(BF16)
            st = jnp.dot(lhs, b_all[g], preferred_element_type=F32)
            hnew_ref[b, h] = prev * tot[0:1, h:h + 1] + st
        yoff_ref[rows, :] = jnp.concatenate(outs, axis=1)
        return carry

    lax.fori_loop(0, SSD_DEC_SEQ, seq, 0, unroll=4)
    y = jnp.concatenate(ys, axis=1) + yoff_ref[...]
    out_ref[...] = _ssd_finish(y, xc[:, :SSD_WIDTH], z_ref[...], dskip_ref, norm_ref)


def _ssd_decode(p, bufpad, state, layer, cw, cb, dtb, alog, dskip, norm):
    n = p.shape[0]
    blk = lambda width: pl.BlockSpec((SSD_CHUNK, width), lambda i: (i, 0))
    col = lambda name: pl.BlockSpec((SSD_CHUNK, P_COLS[name][1]), lambda i: (i, _pcol(name)))
    st_in = pl.BlockSpec((None, SSD_DEC_SEQ, SSD_HEADS, SSD_HEAD_DIM, SSD_STATE), lambda i: (layer, i, 0, 0, 0))
    st_out = pl.BlockSpec((SSD_DEC_SEQ, SSD_HEADS, SSD_HEAD_DIM, SSD_STATE), lambda i: (i, 0, 0, 0))
    return pl.pallas_call(
        _ssd_decode_kernel,
        grid=(n // SSD_CHUNK,),
        in_specs=[col("xbc"), col("z"), col("dt"),
                  pl.BlockSpec((None, SSD_CHUNK, SSD_CONV_CH), lambda i: (layer, i, 0)), st_in,
                  _const_spec((CONV_W, SSD_CONV_CH), layer), _const_spec((1, SSD_CONV_CH), layer),
                  _const_spec((1, LANES), layer), _const_spec((1, LANES), layer),
                  _const_spec((1, SSD_WIDTH), layer), _const_spec((1, SSD_WIDTH), layer)],
        out_specs=[blk(SSD_WIDTH), st_out],
        out_shape=[jax.ShapeDtypeStruct((n, SSD_WIDTH), F32),
                   jax.ShapeDtypeStruct((n // DEC_T, SSD_HEADS, SSD_HEAD_DIM, SSD_STATE), F32)],
        scratch_shapes=[pltpu.VMEM((SSD_CHUNK, SSD_CONV_CH), F32),
                        pltpu.VMEM((SSD_WIDTH, SSD_CHUNK), F32),
                        pltpu.VMEM((SSD_CHUNK, LANES), F32),
                        pltpu.VMEM((SSD_CHUNK, LANES), F32),
                        pltpu.VMEM((SSD_CHUNK, SSD_WIDTH), F32)],
        compiler_params=_cparams(("parallel",)),
        name="ssd_decode",
    )(p, p, p, bufpad, state, cw, cb, dtb, alog, dskip, norm)


DENSE_TM = 512
MIX_A_TB = 512
SSD_TB = 1024


def _pad_state_rows(buf):
    l, b, r, c = buf.shape
    return jnp.pad(buf, ((0, 0), (0, 0), (0, DEC_T - r), (0, 0))).reshape(l, b * DEC_T, c)


def kernel(x_prompt, x_sample, state_lru_h, state_lru_conv, cache_swa_k, cache_swa_v, state_ssd, state_ssd_conv,
           norm_mix_in, norm_mix_out, w_in, conv_a_w, conv_a_b, lru_wa, lru_ba, lru_wx, lru_bx, lru_lambda,
           conv_c_w, conv_c_b, dt_bias, a_log, d_skip, ssm_norm, w_out, norm_ffn_in, norm_ffn_out,
           w_gate_up, w_down):
    bp, seq, _ = x_prompt.shape
    bs, dec_t, _ = x_sample.shape
    assert dec_t == DEC_T and seq % ATT_BLK == 0 and seq >= MAX_WINDOW and bs % SSD_DEC_SEQ == 0

    def dt_lanes(p):
        slot = jnp.pad(p, [(0, 0)] * (p.ndim - 1) + [(0, SUBLANES - SSD_HEADS)])
        rep = jnp.concatenate([slot] * DT_COPIES, axis=-1)
        return jnp.pad(rep, [(0, 0)] * (p.ndim - 1) + [(0, LANES - DT_COPIES * SUBLANES)])

    def w_cols(name):
        off, width = W_IN_COLS[name]
        blk = w_in[:, :, off:off + width]
        if name == "dt":
            return dt_lanes(blk)
        return jnp.pad(blk, ((0, 0), (0, 0), (0, P_COLS[name][1] - width)))

    w_in_b = jnp.concatenate([w_cols(name) for name in P_COLS], axis=2).astype(BF16)
    w_out_b = w_out.astype(BF16)
    w_gu_b = w_gate_up.astype(BF16)
    w_dn_b = w_down.astype(BF16)
    eye = jnp.eye(LRU_BLOCKS, dtype=F32)

    def block_diag(w):
        return (w[:, :, :, None, :] * eye[None, :, None, :, None]).reshape(
            DEPTH, LRU_WIDTH, LRU_WIDTH).astype(BF16)

    wa_bd = block_diag(lru_wa)
    wx_bd = block_diag(lru_wx)
    vec = lambda p: p[:, None, :]
    dtb_p = dt_lanes(dt_bias)[:, None, :]
    alog_p = dt_lanes(a_log)[:, None, :]
    dskip_p = jnp.repeat(d_skip, SSD_HEAD_DIM, axis=1)[:, None, :]
    buf_a = _pad_state_rows(state_lru_conv)
    buf_c = _pad_state_rows(state_ssd_conv)
    h0_a = jnp.pad(state_lru_h[:, :, None, :], ((0, 0), (0, 0), (0, DEC_T - 1), (0, 0))).reshape(
        DEPTH, bs * DEC_T, LRU_WIDTH)
    mult = jnp.asarray(_decode_key_multiplicity())
    cache_kt = jnp.transpose(cache_swa_k, (0, 1, 3, 4, 2))
    cache_vt = jnp.transpose(cache_swa_v, (0, 1, 3, 4, 2))

    yp = x_prompt.reshape(bp * seq, D_MODEL)
    ys = x_sample.reshape(bs * DEC_T, D_MODEL)
    p_new = [[] for _ in range(6)]
    s_new = [[] for _ in range(6)]
    a_args = (conv_a_w, vec(conv_a_b), wa_bd, vec(lru_ba), wx_bd, vec(lru_bx), vec(lru_lambda))
    c_args = (conv_c_w, vec(conv_c_b), dtb_p, alog_p, dskip_p, vec(ssm_norm))
    post_w = (w_out_b, vec(norm_mix_out), vec(norm_ffn_in), w_gu_b, w_dn_b, vec(norm_ffn_out))
    g_in = vec(norm_mix_in)
    for l in range(DEPTH):
        pp, k_tail, v_tail = _in_proj(yp, g_in, w_in_b, l, bp, DENSE_TM, with_tail=True)
        (ps,) = _in_proj(ys, g_in, w_in_b, l, 1, DENSE_TM, with_tail=False)
        out_b, out_b_s = _attn(pp, ps, cache_kt, cache_vt, l, mult, bp)

        out_a, h_last = _mix_a_prompt(pp, *a_args, layer=l, batch=bp, tb=MIX_A_TB)
        out_c, ssd_state = _ssd_prompt(pp, *c_args, layer=l, batch=bp, tb=SSD_TB)
        yp = _post(out_a, out_b, out_c, yp, *post_w, layer=l, tm=DENSE_TM)
        pp3 = pp.reshape(bp, seq, N_IN_PAD)
        last = slice(seq - (CONV_W - 1), seq)
        tail_view = lambda t: jnp.transpose(t.reshape(bp, ATT_HEADS, HEAD_DIM, MAX_WINDOW), (0, 3, 1, 2))
        p_new[0].append(h_last.reshape(bp, LRU_WIDTH))
        p_new[1].append(pp3[:, last, P_COLS["xa"][0]:P_COLS["xa"][0] + LRU_WIDTH])
        p_new[2].append(tail_view(k_tail))
        p_new[3].append(tail_view(v_tail))
        p_new[4].append(ssd_state)
        p_new[5].append(pp3[:, last, P_COLS["xbc"][0]:P_COLS["xbc"][0] + SSD_CONV_CH])

        out_a, h_all = _mix_a_decode(ps, buf_a, h0_a, *a_args, layer=l, tm=DENSE_TM)
        out_c, ssd_state = _ssd_decode(ps, buf_c, state_ssd, l, *c_args)
        ys = _post(out_a, out_b_s, out_c, ys, *post_w, layer=l, tm=DENSE_TM)
        ps3 = ps.reshape(bs, DEC_T, N_IN_PAD)
        last = slice(DEC_T - (CONV_W - 1), DEC_T)
        new_kv = lambda name: ps3[:, :, P_COLS[name][0]:P_COLS[name][0] + ATT_WIDTH].reshape(
            bs, DEC_T, ATT_HEADS, HEAD_DIM)
        s_new[0].append(h_all.reshape(bs, DEC_T, LRU_WIDTH)[:, DEC_T - 1])
        s_new[1].append(ps3[:, last, P_COLS["xa"][0]:P_COLS["xa"][0] + LRU_WIDTH])
        s_new[2].append(new_kv("k"))
        s_new[3].append(new_kv("v"))
        s_new[4].append(ssd_state)
        s_new[5].append(ps3[:, last, P_COLS["xbc"][0]:P_COLS["xbc"][0] + SSD_CONV_CH])

    outs_p = [jnp.stack(a) for a in p_new]
    outs_s = [jnp.stack(a) for a in s_new]
    return (yp.reshape(bp, seq, D_MODEL), ys.reshape(bs, DEC_T, D_MODEL), *outs_p, *outs_s)
```

```python
import functools

import numpy as np
import jax
import jax.numpy as jnp
from jax import lax
from jax.experimental import pallas as pl
from jax.experimental.pallas import tpu as pltpu

F32 = jnp.float32
BF16 = jnp.bfloat16

D_MODEL = 1024
DEPTH = 4
CONV_W = 4
HEAD_DIM = 64
ATT_WIDTH = 384
ATT_HEADS = 6
ATT_SPAN = 128
DILATIONS = (1, 4, 16)
MAX_WINDOW = 2048
ATT_SCALE = HEAD_DIM ** -0.5
SSD_WIDTH = 384
SSD_HEADS = 6
SSD_HEAD_DIM = 64
SSD_GROUPS = 2
SSD_STATE = 128
SSD_CHUNK = 128
SSD_CONV_CH = 896
LRU_WIDTH = 256
LRU_BLOCKS = 4
LRU_C = 8.0
D_FF = 2816
N_IN = 2950
EPS = 1e-6

LANES = 128
SUBLANES = 8
N_IN_PAD = 3072
VMEM_LIMIT = 56 * 1024 * 1024

W_IN_COLS = {"ga": (0, 256), "xa": (256, 256), "q": (512, 384), "k": (896, 384), "v": (1280, 384),
             "z": (1664, 384), "xbc": (2048, 896), "dt": (2944, 6)}
P_COLS = {"xbc": (0, 896), "dt": (896, 128), "ga": (1024, 256), "xa": (1280, 256), "q": (1536, 384),
          "k": (1920, 384), "v": (2304, 384), "z": (2688, 384)}


def _pcol(name, width=None):
    off, w = P_COLS[name]
    width = width or w
    assert off % width == 0
    return off // width

NEG_INF = float("-inf")
NT_DIMS = (((1,), (1,)), ((), ()))


def _cparams(sem):
    return pltpu.CompilerParams(dimension_semantics=sem, vmem_limit_bytes=VMEM_LIMIT)


def _const_spec(shape, layer=None):
    nd = len(shape)
    if layer is None:
        return pl.BlockSpec(shape, lambda *_: (0,) * nd, pipeline_mode=pl.Buffered(1))
    return pl.BlockSpec((None,) + tuple(shape), lambda *_: (layer,) + (0,) * nd, pipeline_mode=pl.Buffered(1))


def _stacked_alias(stacked, arg_index, out_index):
    if stacked is None:
        return [], [], {}
    return [pl.BlockSpec(memory_space=pl.ANY)], [stacked], {arg_index: out_index}


def _drop_refs(kernel_fn, first, count=1):
    def wrapped(*refs, **kw):
        return kernel_fn(*refs[:first], *refs[first + count:], **kw)
    return wrapped


def _rms(x, g):
    ms = jnp.mean(x * x, axis=-1, keepdims=True)
    return x * lax.rsqrt(ms + EPS) * g


def _sigmoid(x):
    return jax.nn.sigmoid(x)


def _silu(x):
    return x * jax.nn.sigmoid(x)


def _softplus(x):
    return jnp.maximum(x, 0.0) + jnp.log1p(jnp.exp(-jnp.abs(x)))


def _gelu_tanh(x):
    c = np.sqrt(2.0 / np.pi).astype(np.float32)
    return 0.5 * x * (1.0 + jnp.tanh(c * (x + 0.044715 * (x * x * x))))


def _roll_rows(x, shift):
    n = x.shape[0]
    shift = shift % n
    if shift == 0:
        return x
    return pltpu.roll(x, shift, 0)


def _row_index(shape, group=None):
    r = lax.broadcasted_iota(jnp.int32, shape, 0)
    if group is not None:
        r = jnp.bitwise_and(r, group - 1)
    return r


def _lin_scan(a, u, row, steps):
    s = 1
    for _ in range(steps):
        keep = row >= s
        a_sh = jnp.where(keep, _roll_rows(a, s), 1.0)
        u_sh = jnp.where(keep, _roll_rows(u, s), 0.0)
        u = a * u_sh + u
        a = a * a_sh
        s *= 2
    return a, u


def _cumsum_rows(x, row, steps):
    s = 1
    for _ in range(steps):
        x = x + jnp.where(row >= s, _roll_rows(x, s), 0.0)
        s *= 2
    return x


def _suffix_sum_rows(x, row, group, steps):
    incl = x
    s = 1
    for _ in range(steps):
        incl = incl + jnp.where(row < group - s, _roll_rows(incl, -s), 0.0)
        s *= 2
    return incl - x


def _conv_taps(x, shifted_fn, w_ref, b_ref):
    y = b_ref[...] + w_ref[CONV_W - 1:CONV_W, :] * x
    for s in range(1, CONV_W):
        y = y + w_ref[CONV_W - 1 - s:CONV_W - s, :] * shifted_fn(s)
    return y


def _conv_block_carry(x, tail_ref, w_ref, b_ref):
    tb, c = x.shape
    x3 = x.reshape(tb // SUBLANES, SUBLANES, c)
    sub = lax.broadcasted_iota(jnp.int32, (1, SUBLANES, c), 1)
    tail = tail_ref[...]

    def shifted(s):
        r = pltpu.roll(x3, s, 1)
        before = jnp.concatenate([pltpu.roll(tail, s, 0)[None], r[:-1]], axis=0)
        return jnp.where(sub < s, before, r)

    y = _conv_taps(x3, shifted, w_ref, b_ref)
    tail_ref[...] = x[tb - SUBLANES:tb]
    return y.reshape(tb, c)


def _conv_groups(x, bufpad, w_ref, b_ref):
    row = _row_index(x.shape, SUBLANES)
    return _conv_taps(
        x, lambda s: jnp.where(row >= s, _roll_rows(x, s), _roll_rows(bufpad, s - (CONV_W - 1))), w_ref, b_ref)


def _lru_gates(xc, wa_ref, ba_ref, wx_ref, bx_ref, lam_ref):
    xb = xc.astype(BF16)
    r = _sigmoid(jnp.dot(xb, wa_ref[...], preferred_element_type=F32) + ba_ref[...])
    ig = _sigmoid(jnp.dot(xb, wx_ref[...], preferred_element_type=F32) + bx_ref[...])
    log_a = (-LRU_C) * r * _softplus(-lam_ref[...])
    a = jnp.exp(log_a)
    t = jnp.tanh(log_a)
    u = jnp.sqrt(-2.0 * t / (1.0 - t)) * (ig * xc)
    return a, u


def _in_proj_kernel(x_ref, g_ref, w_ref, p_ref, *tail_refs, first_tail):
    h = _rms(x_ref[...], g_ref[...]).astype(BF16)
    p_ref[...] = jnp.dot(h, w_ref[...], preferred_element_type=F32)
    if tail_refs:
        kt_ref, vt_ref = tail_refs

        @pl.when(pl.program_id(1) >= first_tail)
        def _():
            k_off, v_off = P_COLS["k"][0], P_COLS["v"][0]
            kt_ref[...] = p_ref[:, k_off:k_off + ATT_WIDTH].T
            vt_ref[...] = p_ref[:, v_off:v_off + ATT_WIDTH].T


def _in_proj(x, g, w, layer, batch, tm, with_tail, tails=None):
    n = x.shape[0]
    nt = n // batch // tm
    first_tail = nt - MAX_WINDOW // tm
    row = lambda width: pl.BlockSpec((tm, width), lambda b, j: (b * nt + j, 0))
    out_specs = [row(N_IN_PAD)]
    out_shape = [jax.ShapeDtypeStruct((n, N_IN_PAD), F32)]
    if with_tail:
        tail = pl.BlockSpec((None, None, ATT_WIDTH, tm),
                            lambda b, j: (layer, b, 0, jnp.maximum(j - first_tail, 0)))
        out_specs += [tail, tail]
        out_shape += [jax.ShapeDtypeStruct((DEPTH, batch, ATT_WIDTH, MAX_WINDOW), F32)] * 2
    body = functools.partial(_in_proj_kernel, first_tail=first_tail)
    extra_specs, extra_args, aliases = [], [], {}
    if tails is not None:
        body = _drop_refs(body, 3, 2)
        extra_specs = [pl.BlockSpec(memory_space=pl.ANY)] * 2
        extra_args = list(tails)
        aliases = {3: 1, 4: 2}
    return pl.pallas_call(
        body,
        grid=(batch, nt),
        in_specs=[row(D_MODEL), _const_spec((1, D_MODEL), layer), _const_spec((D_MODEL, N_IN_PAD), layer)]
        + extra_specs,
        out_specs=out_specs,
        out_shape=out_shape,
        input_output_aliases=aliases,
        compiler_params=_cparams(("parallel", "arbitrary")),
        name="in_proj",
    )(x, g, w, *extra_args)


FFN_CHUNK = 256


def _post_kernel(a_ref, b_ref, c_ref, x_ref, wo_ref, go_ref, gi_ref, wgu_ref, wd_ref, gf_ref, o_ref):
    mixed = jnp.concatenate([a_ref[...], b_ref[...], c_ref[...]], axis=1).astype(BF16)
    x1 = x_ref[...] + _rms(jnp.dot(mixed, wo_ref[...], preferred_element_type=F32), go_ref[...])
    h = _rms(x1, gi_ref[...]).astype(BF16)
    acc = jnp.zeros(x1.shape, F32)
    for c in range(D_FF // FFN_CHUNK):
        lo = c * FFN_CHUNK
        g = jnp.dot(h, wgu_ref[:, lo:lo + FFN_CHUNK], preferred_element_type=F32)
        u = jnp.dot(h, wgu_ref[:, D_FF + lo:D_FF + lo + FFN_CHUNK], preferred_element_type=F32)
        act = (_silu(g) * u).astype(BF16)
        acc = acc + jnp.dot(act, wd_ref[lo:lo + FFN_CHUNK, :], preferred_element_type=F32)
    o_ref[...] = x1 + _rms(acc, gf_ref[...])


def _post(a, b, c, x, wo, go, gi, wgu, wd, gf, layer, tm):
    n = x.shape[0]
    row = lambda width: pl.BlockSpec((tm, width), lambda i: (i, 0))
    vec = _const_spec((1, D_MODEL), layer)
    return pl.pallas_call(
        _post_kernel,
        grid=(n // tm,),
        in_specs=[row(LRU_WIDTH), row(ATT_WIDTH), row(SSD_WIDTH), row(D_MODEL),
                  _const_spec((D_MODEL, D_MODEL), layer), vec, vec,
                  _const_spec((D_MODEL, 2 * D_FF), layer), _const_spec((D_FF, D_MODEL), layer), vec],
        out_specs=row(D_MODEL),
        out_shape=jax.ShapeDtypeStruct((n, D_MODEL), F32),
        compiler_params=_cparams(("parallel",)),
        name="post",
    )(a, b, c, x, wo, go, gi, wgu, wd, gf)


def _mix_a_prompt_kernel(xa_ref, ga_ref, cw_ref, cb_ref, wa_ref, ba_ref, wx_ref, bx_ref, lam_ref,
                         out_ref, hlast_ref, tail_ref, hc_ref):
    @pl.when(pl.program_id(1) == 0)
    def _():
        tail_ref[...] = jnp.zeros_like(tail_ref)
        hc_ref[...] = jnp.zeros_like(hc_ref)

    x = xa_ref[...]
    tb, c = x.shape
    xc = _conv_block_carry(x, tail_ref, cw_ref, cb_ref)
    a, u = _lru_gates(xc, wa_ref, ba_ref, wx_ref, bx_ref, lam_ref)
    gate = _gelu_tanh(ga_ref[...])
    nslab = tb // SUBLANES
    a3 = a.reshape(nslab, SUBLANES, c)
    u3 = u.reshape(nslab, SUBLANES, c)
    sub = lax.broadcasted_iota(jnp.int32, (1, SUBLANES, c), 1)
    s = 1
    while s < SUBLANES:
        keep = sub >= s
        a_sh = jnp.where(keep, pltpu.roll(a3, s, 1), 1.0)
        u_sh = jnp.where(keep, pltpu.roll(u3, s, 1), 0.0)
        u3 = a3 * u_sh + u3
        a3 = a3 * a_sh
        s *= 2
    h_row = hc_ref[0:1, :]
    for k in range(nslab):
        h = u3[k] + a3[k] * h_row
        h_row = h[SUBLANES - 1:SUBLANES, :]
        rows = slice(k * SUBLANES, (k + 1) * SUBLANES)
        out_ref[rows, :] = h * gate[rows, :]
    hc_ref[...] = jnp.broadcast_to(h_row, hc_ref.shape)
    hlast_ref[...] = h_row


def _mix_a_prompt(p, cw, cb, wa, ba, wx, bx, lam, layer, batch, tb):
    n = p.shape[0]
    nb = n // batch // tb
    blk = pl.BlockSpec((tb, LRU_WIDTH), lambda b, i: (b * nb + i, 0))
    col = lambda name: pl.BlockSpec((tb, LRU_WIDTH), lambda b, i: (b * nb + i, _pcol(name)))
    vec = _const_spec((1, LRU_WIDTH), layer)
    mat = _const_spec((LRU_WIDTH, LRU_WIDTH), layer)
    return pl.pallas_call(
        _mix_a_prompt_kernel,
        grid=(batch, nb),
        in_specs=[col("xa"), col("ga"), _const_spec((CONV_W, LRU_WIDTH), layer), vec, mat, vec, mat, vec, vec],
        out_specs=[blk, pl.BlockSpec((None, 1, LRU_WIDTH), lambda b, i: (b, 0, 0))],
        out_shape=[jax.ShapeDtypeStruct((n, LRU_WIDTH), F32),
                   jax.ShapeDtypeStruct((batch, 1, LRU_WIDTH), F32)],
        scratch_shapes=[pltpu.VMEM((SUBLANES, LRU_WIDTH), F32), pltpu.VMEM((SUBLANES, LRU_WIDTH), F32)],
        compiler_params=_cparams(("parallel", "arbitrary")),
        name="mix_a_prompt",
    )(p, p, cw, cb, wa, ba, wx, bx, lam)


def _mix_a_decode_kernel(xa_ref, ga_ref, buf_ref, h0_ref, cw_ref, cb_ref, wa_ref, ba_ref, wx_ref, bx_ref,
                         lam_ref, out_ref, h_ref):
    x = xa_ref[...]
    xc = _conv_groups(x, buf_ref[...], cw_ref, cb_ref)
    a, u = _lru_gates(xc, wa_ref, ba_ref, wx_ref, bx_ref, lam_ref)
    u = u + a * h0_ref[...]
    row = _row_index(x.shape, SUBLANES)
    _, h = _lin_scan(a, u, row, 3)
    h_ref[...] = h
    out_ref[...] = h * _gelu_tanh(ga_ref[...])


def _mix_a_decode(p, bufpad, h0pad, cw, cb, wa, ba, wx, bx, lam, layer, tm):
    n = p.shape[0]
    blk = pl.BlockSpec((tm, LRU_WIDTH), lambda i: (i, 0))
    col = lambda name: pl.BlockSpec((tm, LRU_WIDTH), lambda i: (i, _pcol(name)))
    lay = pl.BlockSpec((None, tm, LRU_WIDTH), lambda i: (layer, i, 0))
    vec = _const_spec((1, LRU_WIDTH), layer)
    mat = _const_spec((LRU_WIDTH, LRU_WIDTH), layer)
    return pl.pallas_call(
        _mix_a_decode_kernel,
        grid=(n // tm,),
        in_specs=[col("xa"), col("ga"), lay, lay, _const_spec((CONV_W, LRU_WIDTH), layer), vec, mat, vec, mat, vec,
                  vec],
        out_specs=[blk, blk],
        out_shape=[jax.ShapeDtypeStruct((n, LRU_WIDTH), F32), jax.ShapeDtypeStruct((n, LRU_WIDTH), F32)],
        compiler_params=_cparams(("parallel",)),
        name="mix_a_decode",
    )(p, p, bufpad, h0pad, cw, cb, wa, ba, wx, bx, lam)


ATT_BLK = 2048
ATT_UNIT = 128
ATT_UNROLL = 4


def _attn_unit(q_t, kp_t, kc_t, vp_t, vc_t, bias, lane_lo):
    zero = jnp.zeros_like(q_t)
    qs = q_t * ATT_SCALE
    qq = jnp.concatenate([jnp.where(lane_lo, qs, zero), jnp.where(lane_lo, zero, qs)], axis=0).astype(BF16)
    kk = jnp.concatenate([kp_t, kc_t], axis=0).astype(BF16)
    s = lax.dot_general(qq, kk, (((1,), (1,)), ((), ())), preferred_element_type=F32) + bias
    m = jnp.max(s, axis=1, keepdims=True)
    p = jnp.exp(s - m).astype(BF16)
    pcat = jnp.concatenate([p[:ATT_UNIT], p[ATT_UNIT:]], axis=1)
    one_lo = jnp.where(lane_lo, 1.0, 0.0)
    one_hi = 1.0 - one_lo
    w = jnp.concatenate([
        jnp.concatenate([jnp.where(lane_lo, vp_t, zero), one_lo], axis=1),
        jnp.concatenate([jnp.where(lane_lo, vc_t, zero), one_lo], axis=1),
        jnp.concatenate([jnp.where(lane_lo, zero, vp_t), one_hi], axis=1),
        jnp.concatenate([jnp.where(lane_lo, zero, vc_t), one_hi], axis=1)], axis=0).astype(BF16)
    ol = jnp.dot(pcat, w, preferred_element_type=F32)
    m_t = jnp.where(lane_lo, jnp.broadcast_to(m[:ATT_UNIT], q_t.shape), jnp.broadcast_to(m[ATT_UNIT:], q_t.shape))
    return ol[:, :LANES], m_t, ol[:, LANES:]


DEC_T = 8
DEC_NEW_PAD = 128
DEC_KEYS = MAX_WINDOW + DEC_NEW_PAD
DEC_ROWS = ATT_HEADS * DEC_T
ATT_PHASES = len(DILATIONS)


def _decode_key_multiplicity():
    pos = np.full((DEC_KEYS,), -10 ** 9, np.int64)
    pos[:MAX_WINDOW] = np.arange(MAX_WINDOW)
    pos[MAX_WINDOW:MAX_WINDOW + DEC_T] = MAX_WINDOW + np.arange(DEC_T)
    t = np.arange(DEC_T)
    dist = (MAX_WINDOW + t)[:, None] - pos[None, :]
    cnt = np.zeros(dist.shape, np.float32)
    for d in DILATIONS:
        cnt += ((dist >= 0) & (dist % d == 0) & (dist <= ATT_SPAN * d)).astype(np.float32)
    return np.tile(cnt, (ATT_HEADS, 1))


def _attn_decode_block(q_ref, kn_ref, vn_ref, kt_ref, vt_ref, mult_ref, o_ref):
    lane = lax.broadcasted_iota(jnp.int32, (DEC_T, ATT_WIDTH), 1)
    head_masks = [jnp.logical_and(lane >= HEAD_DIM * h, lane < HEAD_DIM * (h + 1)) for h in range(ATT_HEADS)]
    mult = mult_ref[...]
    seen = mult > 0.0
    zpad = jnp.zeros((DEC_NEW_PAD - DEC_T, ATT_WIDTH), F32)
    for b in range(kt_ref.shape[0]):
        new_rows = slice(DEC_T * b, DEC_T * (b + 1))
        qb = q_ref[new_rows, :] * ATT_SCALE
        qbd = jnp.concatenate([jnp.where(hm, qb, 0.0) for hm in head_masks], axis=0).astype(BF16)
        kt = kt_ref[b].reshape(ATT_WIDTH, MAX_WINDOW).astype(BF16)
        kn = jnp.concatenate([kn_ref[new_rows, :], zpad], axis=0).astype(BF16)
        s = jnp.concatenate([jnp.dot(qbd, kt, preferred_element_type=F32),
                             lax.dot_general(qbd, kn, NT_DIMS, preferred_element_type=F32)], axis=1)
        s = jnp.where(seen, s, NEG_INF)
        m = jnp.max(s, axis=1, keepdims=True)
        p = mult * jnp.exp(s - m)
        p = (p / jnp.sum(p, axis=1, keepdims=True)).astype(BF16)
        vt = vt_ref[b].reshape(ATT_WIDTH, MAX_WINDOW).astype(BF16)
        vn = jnp.concatenate([vn_ref[new_rows, :], zpad], axis=0).astype(BF16)
        o = (lax.dot_general(p[:, :MAX_WINDOW], vt, NT_DIMS, preferred_element_type=F32)
             + jnp.dot(p[:, MAX_WINDOW:], vn, preferred_element_type=F32))
        out = jnp.zeros((DEC_T, ATT_WIDTH), F32)
        for h, hm in enumerate(head_masks):
            out = out + jnp.where(hm, o[DEC_T * h:DEC_T * (h + 1), :], 0.0)
        o_ref[new_rows, :] = out


def _attn_kernel(q_ref, kp_ref, kc_ref, vp_ref, vc_ref, qn_ref, kn_ref, vn_ref, kt_ref, vt_ref, mult_ref,
                 o_ref, od_ref, acc_ref, m_ref, l_ref, *, dec_blocks):
    sub = pl.program_id(3)
    step = ((pl.program_id(0) * pl.num_programs(1) + pl.program_id(1)) * pl.num_programs(2)
            + pl.program_id(2)) * ATT_PHASES + sub

    @pl.when(step < dec_blocks)
    def _():
        _attn_decode_block(qn_ref, kn_ref, vn_ref, kt_ref, vt_ref, mult_ref, od_ref)

    first_block = pl.program_id(2) == 0
    lane_lo = lax.broadcasted_iota(jnp.int32, (ATT_UNIT, LANES), 1) < HEAD_DIM
    qi = lax.broadcasted_iota(jnp.int32, (2 * ATT_UNIT, 2 * ATT_UNIT), 0) & (ATT_UNIT - 1)
    ki = lax.broadcasted_iota(jnp.int32, (2 * ATT_UNIT, 2 * ATT_UNIT), 1)
    is_prev = ki < ATT_UNIT
    dist = qi - ki + ATT_UNIT
    bias = jnp.where(dist >= 0, jnp.where(dist <= ATT_SPAN, 0.0, NEG_INF), NEG_INF)
    bias_first = bias + jnp.where(is_prev, jnp.where(first_block, NEG_INF, 0.0), 0.0)

    def rows(start, d):
        if d == 1:
            return pl.ds(start, ATT_UNIT)
        return pl.ds(start, ATT_UNIT, stride=d)

    def keep(bi, sl, o_t, m_t, l_t):
        acc_ref[bi, sl, :] = o_t
        m_ref[bi, sl, :] = m_t
        l_ref[bi, sl, :] = l_t

    def head_unit(bi, rho):
        d = DILATIONS[bi]
        cur = rows(rho, d)
        prev = rows(rho + ATT_BLK - d * ATT_UNIT, d)
        keep(bi, cur, *_attn_unit(q_ref[cur, :], kp_ref[prev, :], kc_ref[cur, :], vp_ref[prev, :],
                                  vc_ref[cur, :], bias_first, lane_lo))

    def inner_unit(bi, idx):
        d = DILATIONS[bi]
        rho = idx & (d - 1)
        j = 1 + (idx >> int(np.log2(d)))
        start = rho + d * ATT_UNIT * j
        cur = rows(start, d)
        prev = rows(start - d * ATT_UNIT, d)
        keep(bi, cur, *_attn_unit(q_ref[cur, :], kc_ref[prev, :], kc_ref[cur, :], vc_ref[prev, :],
                                  vc_ref[cur, :], bias, lane_lo))

    def chunks(unit, bi, base, lo, hi):
        def body(c, carry):
            for u in range(ATT_UNROLL):
                unit(bi, base + ATT_UNROLL * c + u)
            return carry
        lax.fori_loop(lo, hi, body, 0)

    @pl.when(sub == 0)
    def _():
        head_unit(0, 0)
        for idx in range(ATT_UNROLL - 1):
            inner_unit(0, idx)
        chunks(inner_unit, 0, ATT_UNROLL - 1, 0, 3)

    @pl.when(sub == 1)
    def _():
        for rho in range(ATT_UNROLL):
            head_unit(1, rho)
        chunks(inner_unit, 1, 0, 0, 3)

    @pl.when(sub == 2)
    def _():
        chunks(head_unit, 2, 0, 0, 4)

        def combine(c, carry):
            sl = pl.ds(pl.multiple_of(c * ATT_UNIT, ATT_UNIT), ATT_UNIT)
            ms = [m_ref[bi, sl, :] for bi in range(len(DILATIONS))]
            m = functools.reduce(jnp.maximum, ms)
            ws = [jnp.exp(mi - m) for mi in ms]
            num = sum(w * acc_ref[bi, sl, :] for bi, w in enumerate(ws))
            den = sum(w * l_ref[bi, sl, :] for bi, w in enumerate(ws))
            o_ref[sl, :] = num / den
            return carry

        lax.fori_loop(0, ATT_BLK // ATT_UNIT, combine, 0, unroll=2)


def _attn(pp, ps, cache_kt, cache_vt, layer, mult, batch):
    n = pp.shape[0]
    nb = n // batch // ATT_BLK
    npair = ATT_WIDTH // LANES
    nd = ps.shape[0]
    nseq = nd // DEC_T
    steps = batch * npair * nb * ATT_PHASES
    spb = -(-nseq // steps)
    assert nseq % spb == 0
    dec_blocks = nseq // spb

    def dec_blk(b, hp, i, s):
        return jnp.minimum(((b * npair + hp) * nb + i) * ATT_PHASES + s, dec_blocks - 1)

    out = pl.BlockSpec((ATT_BLK, LANES), lambda b, hp, i, s: (b * nb + i, hp))
    cur = lambda name: pl.BlockSpec((ATT_BLK, LANES), lambda b, hp, i, s: (b * nb + i, _pcol(name, LANES) + hp))
    prev = lambda name: pl.BlockSpec(
        (ATT_BLK, LANES), lambda b, hp, i, s: (b * nb + jnp.maximum(i - 1, 0), _pcol(name, LANES) + hp))
    new = lambda name: pl.BlockSpec((DEC_T * spb, ATT_WIDTH), lambda b, hp, i, s: (dec_blk(b, hp, i, s), _pcol(name)))
    win = pl.BlockSpec((None, spb, ATT_HEADS, HEAD_DIM, MAX_WINDOW),
                       lambda b, hp, i, s: (layer, dec_blk(b, hp, i, s), 0, 0, 0))
    scratch = pltpu.VMEM((len(DILATIONS), ATT_BLK, LANES), F32)
    return pl.pallas_call(
        functools.partial(_attn_kernel, dec_blocks=dec_blocks),
        grid=(batch, npair, nb, ATT_PHASES),
        in_specs=[cur("q"), prev("k"), cur("k"), prev("v"), cur("v"), new("q"), new("k"), new("v"), win, win,
                  _const_spec((DEC_ROWS, DEC_KEYS))],
        out_specs=[out, pl.BlockSpec((DEC_T * spb, ATT_WIDTH), lambda b, hp, i, s: (dec_blk(b, hp, i, s), 0))],
        out_shape=[jax.ShapeDtypeStruct((n, ATT_WIDTH), F32), jax.ShapeDtypeStruct((nd, ATT_WIDTH), F32)],
        scratch_shapes=[scratch, scratch, scratch],
        compiler_params=_cparams(("arbitrary", "arbitrary", "arbitrary", "arbitrary")),
        name="attn",
    )(pp, pp, pp, pp, pp, ps, ps, ps, cache_kt, cache_vt, mult)


HPG = SSD_HEADS // SSD_GROUPS
B_OFF = SSD_WIDTH
C_OFF = SSD_WIDTH + SSD_GROUPS * SSD_STATE


def _ssd_chunk_diag(xbc, dt, acum, pair_ok):
    acum_t = acum.T
    ys, xrs = [], []
    for g in range(SSD_GROUPS):
        bg = xbc[:, B_OFF + g * SSD_STATE:B_OFF + (g + 1) * SSD_STATE].astype(BF16)
        cg = xbc[:, C_OFF + g * SSD_STATE:C_OFF + (g + 1) * SSD_STATE].astype(BF16)
        cb = lax.dot_general(cg, bg, NT_DIMS, preferred_element_type=F32)
        for h in range(g * HPG, (g + 1) * HPG):
            xr = xbc[:, h * SSD_HEAD_DIM:(h + 1) * SSD_HEAD_DIM] * dt[:, h:h + 1]
            diff = acum[:, h:h + 1] - acum_t[h:h + 1, :]
            lmat = jnp.exp(jnp.where(pair_ok, diff, NEG_INF))
            ys.append(jnp.dot((cb * lmat).astype(BF16), xr.astype(BF16), preferred_element_type=F32))
            xrs.append(xr)
    return ys, xrs


def _ssd_finish(y, xs, z, dskip_ref, norm_ref):
    y = y + dskip_ref[...] * xs
    y = y * _silu(z)
    return _rms(y, norm_ref[...])


DT_COPIES = 3
SSD_PAIRS = SSD_HEADS // 2
SSD_CONV_ROWS = 128


def _ssd_prompt_kernel(xbc_ref, z_ref, dt_ref, cw_ref, cb_ref, dtb_ref, alog_ref, dskip_ref, norm_ref,
                       out_ref, state_ref, tail_ref, xc_ref, st_ref):
    tb = xbc_ref.shape[0]

    @pl.when(pl.program_id(1) == 0)
    def _():
        tail_ref[...] = jnp.zeros_like(tail_ref)
        st_ref[...] = jnp.zeros_like(st_ref)

    for r0 in range(0, tb, SSD_CONV_ROWS):
        xc_ref[r0:r0 + SSD_CONV_ROWS, :] = _silu(
            _conv_block_carry(xbc_ref[r0:r0 + SSD_CONV_ROWS, :], tail_ref, cw_ref, cb_ref))
    a_neg = -jnp.exp(alog_ref[...])
    row = _row_index((SSD_CHUNK, LANES))
    lane = lax.broadcasted_iota(jnp.int32, (SSD_CHUNK, LANES), 1)
    lane_lo = lane < SSD_HEAD_DIM
    causal = row >= lane
    zero = jnp.zeros((SSD_CHUNK, LANES), F32)

    def pair_rows(t):
        return jnp.concatenate([jnp.where(lane_lo, t, zero), jnp.where(lane_lo, zero, t)], axis=0).astype(BF16)

    def chunk(c, carry):
        r0 = pl.multiple_of(c * SSD_CHUNK, SSD_CHUNK)
        rows = pl.ds(r0, SSD_CHUNK)
        xc = xc_ref[rows, :]
        dt = _softplus(dt_ref[rows, :] + dtb_ref[...])
        acum = _cumsum_rows(dt * a_neg, row, 7)
        last = acum[SSD_CHUNK - 1:SSD_CHUNK, :]
        tot = jnp.exp(last)
        pt = jnp.where(lane < SUBLANES, acum, jnp.where(lane < 2 * SUBLANES, dt, dt * jnp.exp(last - acum))).T
        bs = [xc[:, B_OFF + g * SSD_STATE:B_OFF + (g + 1) * SSD_STATE] for g in range(SSD_GROUPS)]
        cs = [xc[:, C_OFF + g * SSD_STATE:C_OFF + (g + 1) * SSD_STATE].astype(BF16) for g in range(SSD_GROUPS)]
        bts = [b.T for b in bs]
        cbs = [lax.dot_general(cs[g], bs[g].astype(BF16), NT_DIMS, preferred_element_type=F32)
               for g in range(SSD_GROUPS)]
        gs, eacs, btws = [], [], []
        for h in range(SSD_HEADS):
            g = h // HPG
            a_col = jnp.broadcast_to(acum[:, h:h + 1], (SSD_CHUNK, SSD_CHUNK))
            lmat = jnp.exp(jnp.where(causal, a_col - pt[h:h + 1, :], NEG_INF))
            gs.append((cbs[g] * lmat * pt[SUBLANES + h:SUBLANES + h + 1, :]).astype(BF16))
            eacs.append(jnp.exp(a_col))
            btws.append((bts[g] * pt[2 * SUBLANES + h:2 * SUBLANES + h + 1, :]).astype(BF16))
        outs = []
        for k in range(SSD_PAIRS):
            h0, h1 = 2 * k, 2 * k + 1
            g0, g1 = h0 // HPG, h1 // HPG
            x2 = pair_rows(xc[:, k * LANES:(k + 1) * LANES])
            y = jnp.dot(jnp.concatenate([gs[h0], gs[h1]], axis=1), x2, preferred_element_type=F32)
            st = st_ref[k]
            if g0 == g1:
                y_off = jnp.dot(cs[g0], st.astype(BF16), preferred_element_type=F32)
            else:
                y_off = jnp.dot(jnp.concatenate([cs[g0], cs[g1]], axis=1), pair_rows(st),
                                preferred_element_type=F32)
            outs.append(y + y_off * jnp.where(lane_lo, eacs[h0], eacs[h1]))
            upd = jnp.dot(jnp.concatenate([btws[h0], btws[h1]], axis=1), x2, preferred_element_type=F32)
            st_ref[k] = st * jnp.where(lane_lo, tot[:, h0:h0 + 1], tot[:, h1:h1 + 1]) + upd
        y = jnp.concatenate(outs, axis=1)
        out_ref[rows, :] = _ssd_finish(y, xc[:, :SSD_WIDTH], z_ref[rows, :], dskip_ref, norm_ref)
        return carry

    lax.fori_loop(0, tb // SSD_CHUNK, chunk, 0)
    for k in range(SSD_PAIRS):
        t = st_ref[k].T
        state_ref[2 * k] = t[:SSD_HEAD_DIM]
        state_ref[2 * k + 1] = t[SSD_HEAD_DIM:]


def _ssd_prompt(p, cw, cb, dtb, alog, dskip, norm, layer, batch, tb):
    n = p.shape[0]
    nb = n // batch // tb
    blk = lambda width: pl.BlockSpec((tb, width), lambda b, i: (b * nb + i, 0))
    col = lambda name: pl.BlockSpec((tb, P_COLS[name][1]), lambda b, i: (b * nb + i, _pcol(name)))
    return pl.pallas_call(
        _ssd_prompt_kernel,
        grid=(batch, nb),
        in_specs=[col("xbc"), col("z"), col("dt"),
                  _const_spec((CONV_W, SSD_CONV_CH), layer), _const_spec((1, SSD_CONV_CH), layer),
                  _const_spec((1, LANES), layer), _const_spec((1, LANES), layer),
                  _const_spec((1, SSD_WIDTH), layer), _const_spec((1, SSD_WIDTH), layer)],
        out_specs=[blk(SSD_WIDTH),
                   pl.BlockSpec((None, SSD_HEADS, SSD_HEAD_DIM, SSD_STATE), lambda b, i: (b, 0, 0, 0))],
        out_shape=[jax.ShapeDtypeStruct((n, SSD_WIDTH), F32),
                   jax.ShapeDtypeStruct((batch, SSD_HEADS, SSD_HEAD_DIM, SSD_STATE), F32)],
        scratch_shapes=[pltpu.VMEM((SUBLANES, SSD_CONV_CH), F32),
                        pltpu.VMEM((tb, SSD_CONV_CH), F32),
                        pltpu.VMEM((SSD_PAIRS, SSD_STATE, LANES), F32)],
        compiler_params=_cparams(("parallel", "arbitrary")),
        name="ssd_prompt",
    )(p, p, p, cw, cb, dtb, alog, dskip, norm)


SSD_DEC_SEQ = SSD_CHUNK // DEC_T


def _ssd_decode_kernel(xbc_ref, z_ref, dt_ref, buf_ref, h0_ref, cw_ref, cb_ref, dtb_ref, alog_ref, dskip_ref,
                       norm_ref, out_ref, hnew_ref, xc_ref, xrd_ref, eac_ref, tot_ref, yoff_ref):
    xc = _silu(_conv_groups(xbc_ref[...], buf_ref[...], cw_ref, cb_ref))
    xc_ref[...] = xc
    a_neg = -jnp.exp(alog_ref[...])
    row = _row_index((SSD_CHUNK, LANES), DEC_T)
    li = lax.broadcasted_iota(jnp.int32, (SSD_CHUNK, SSD_CHUNK), 0)
    si = lax.broadcasted_iota(jnp.int32, (SSD_CHUNK, SSD_CHUNK), 1)
    same_seq_causal = jnp.logical_and(li >= si, (li - si) <= (li & (DEC_T - 1)))
    dt = _softplus(dt_ref[...] + dtb_ref[...])
    dta = dt * a_neg
    acum = _cumsum_rows(dta, row, 3)
    rest = _suffix_sum_rows(dta, row, DEC_T, 3)
    eac_ref[...] = jnp.exp(acum)
    tot_ref[...] = jnp.exp(acum + rest)
    decay = jnp.exp(rest)
    ys, xrs = _ssd_chunk_diag(xc, dt, acum, same_seq_causal)
    xrd_ref[...] = jnp.concatenate([xrs[h] * decay[:, h:h + 1] for h in range(SSD_HEADS)], axis=1).T
    seq_of_lane = lax.shift_right_logical(lax.broadcasted_iota(jnp.int32, (SSD_HEAD_DIM, SSD_CHUNK), 1), 3)
    b_all = [xc[:, B_OFF + g * SSD_STATE:B_OFF + (g + 1) * SSD_STATE].astype(BF16) for g in range(SSD_GROUPS)]

    def seq(b, carry):
        r0 = pl.multiple_of(b * DEC_T, DEC_T)
        rows = pl.ds(r0, DEC_T)
        xcb = xc_ref[rows, :]
        eac = eac_ref[rows, :]
        tot = tot_ref[rows, :]
        own = seq_of_lane == b
        outs = []
        for h in range(SSD_HEADS):
            g = h // HPG
            cg = xcb[:, C_OFF + g * SSD_STATE:C_OFF + (g + 1) * SSD_STATE].astype(BF16)
            prev = h0_ref[b, h]
            outs.append(lax.dot_general(cg, prev.astype(BF16), NT_DIMS, preferred_element_type=F32)
                        * eac[:, h:h + 1])
            lhs = jnp.where(own, xrd_ref[h * SSD_HEAD_DIM:(h + 1) * SSD_HEAD_DIM, :], 0.0).astype(BF16)
            st = jnp.dot(lhs, b_all[g], preferred_element_type=F32)
            hnew_ref[b, h] = prev * tot[0:1, h:h + 1] + st
        yoff_ref[rows, :] = jnp.concatenate(outs, axis=1)
        return carry

    lax.fori_loop(0, SSD_DEC_SEQ, seq, 0, unroll=4)
    y = jnp.concatenate(ys, axis=1) + yoff_ref[...]
    out_ref[...] = _ssd_finish(y, xc[:, :SSD_WIDTH], z_ref[...], dskip_ref, norm_ref)


def _ssd_decode(p, bufpad, state, stacked, layer, cw, cb, dtb, alog, dskip, norm):
    n = p.shape[0]
    blk = lambda width: pl.BlockSpec((SSD_CHUNK, width), lambda i: (i, 0))
    col = lambda name: pl.BlockSpec((SSD_CHUNK, P_COLS[name][1]), lambda i: (i, _pcol(name)))
    st_in = pl.BlockSpec((None, SSD_DEC_SEQ, SSD_HEADS, SSD_HEAD_DIM, SSD_STATE), lambda i: (layer, i, 0, 0, 0))
    extra_specs, extra_args, aliases = _stacked_alias(stacked, 11, 1)
    return pl.pallas_call(
        _ssd_decode_kernel if stacked is None else _drop_refs(_ssd_decode_kernel, 11),
        grid=(n // SSD_CHUNK,),
        in_specs=[col("xbc"), col("z"), col("dt"),
                  pl.BlockSpec((None, SSD_CHUNK, SSD_CONV_CH), lambda i: (layer, i, 0)), st_in,
                  _const_spec((CONV_W, SSD_CONV_CH), layer), _const_spec((1, SSD_CONV_CH), layer),
                  _const_spec((1, LANES), layer), _const_spec((1, LANES), layer),
                  _const_spec((1, SSD_WIDTH), layer), _const_spec((1, SSD_WIDTH), layer)] + extra_specs,
        out_specs=[blk(SSD_WIDTH), st_in],
        out_shape=[jax.ShapeDtypeStruct((n, SSD_WIDTH), F32), jax.ShapeDtypeStruct(state.shape, F32)],
        input_output_aliases=aliases,
        scratch_shapes=[pltpu.VMEM((SSD_CHUNK, SSD_CONV_CH), F32),
                        pltpu.VMEM((SSD_WIDTH, SSD_CHUNK), F32),
                        pltpu.VMEM((SSD_CHUNK, LANES), F32),
                        pltpu.VMEM((SSD_CHUNK, LANES), F32),
                        pltpu.VMEM((SSD_CHUNK, SSD_WIDTH), F32)],
        compiler_params=_cparams(("parallel",)),
        name="ssd_decode",
    )(p, p, p, bufpad, state, cw, cb, dtb, alog, dskip, norm, *extra_args)


DENSE_TM = 512
MIX_A_TB = 512
SSD_TB = 1024


def _pad_state_rows(buf):
    l, b, r, c = buf.shape
    return jnp.pad(buf, ((0, 0), (0, 0), (0, DEC_T - r), (0, 0))).reshape(l, b * DEC_T, c)


def kernel(x_prompt, x_sample, state_lru_h, state_lru_conv, cache_swa_k, cache_swa_v, state_ssd, state_ssd_conv,
           norm_mix_in, norm_mix_out, w_in, conv_a_w, conv_a_b, lru_wa, lru_ba, lru_wx, lru_bx, lru_lambda,
           conv_c_w, conv_c_b, dt_bias, a_log, d_skip, ssm_norm, w_out, norm_ffn_in, norm_ffn_out,
           w_gate_up, w_down):
    bp, seq, _ = x_prompt.shape
    bs, dec_t, _ = x_sample.shape
    assert dec_t == DEC_T and seq % ATT_BLK == 0 and seq >= MAX_WINDOW and bs % SSD_DEC_SEQ == 0

    def dt_lanes(p):
        slot = jnp.pad(p, [(0, 0)] * (p.ndim - 1) + [(0, SUBLANES - SSD_HEADS)])
        rep = jnp.concatenate([slot] * DT_COPIES, axis=-1)
        return jnp.pad(rep, [(0, 0)] * (p.ndim - 1) + [(0, LANES - DT_COPIES * SUBLANES)])

    def w_cols(name):
        off, width = W_IN_COLS[name]
        blk = w_in[:, :, off:off + width]
        if name == "dt":
            return dt_lanes(blk)
        return jnp.pad(blk, ((0, 0), (0, 0), (0, P_COLS[name][1] - width)))

    w_in_b = jnp.concatenate([w_cols(name) for name in P_COLS], axis=2).astype(BF16)
    w_out_b = w_out.astype(BF16)
    w_gu_b = w_gate_up.astype(BF16)
    w_dn_b = w_down.astype(BF16)
    eye = jnp.eye(LRU_BLOCKS, dtype=F32)

    def block_diag(w):
        return (w[:, :, :, None, :] * eye[None, :, None, :, None]).reshape(
            DEPTH, LRU_WIDTH, LRU_WIDTH).astype(BF16)

    wa_bd = block_diag(lru_wa)
    wx_bd = block_diag(lru_wx)
    vec = lambda p: p[:, None, :]
    dtb_p = dt_lanes(dt_bias)[:, None, :]
    alog_p = dt_lanes(a_log)[:, None, :]
    dskip_p = jnp.repeat(d_skip, SSD_HEAD_DIM, axis=1)[:, None, :]
    buf_a = _pad_state_rows(state_lru_conv)
    buf_c = _pad_state_rows(state_ssd_conv)
    h0_a = jnp.pad(state_lru_h[:, :, None, :], ((0, 0), (0, 0), (0, DEC_T - 1), (0, 0))).reshape(
        DEPTH, bs * DEC_T, LRU_WIDTH)
    mult = jnp.asarray(_decode_key_multiplicity())
    cache_kt = jnp.transpose(cache_swa_k, (0, 1, 3, 4, 2))
    cache_vt = jnp.transpose(cache_swa_v, (0, 1, 3, 4, 2))

    yp = x_prompt.reshape(bp * seq, D_MODEL)
    ys = x_sample.reshape(bs * DEC_T, D_MODEL)
    p_new = [[] for _ in range(6)]
    s_new = [[] for _ in range(6)]
    a_args = (conv_a_w, vec(conv_a_b), wa_bd, vec(lru_ba), wx_bd, vec(lru_bx), vec(lru_lambda))
    c_args = (conv_c_w, vec(conv_c_b), dtb_p, alog_p, dskip_p, vec(ssm_norm))
    post_w = (w_out_b, vec(norm_mix_out), vec(norm_ffn_in), w_gu_b, w_dn_b, vec(norm_ffn_out))
    g_in = vec(norm_mix_in)
    tails, s_ssd = None, None
    for l in range(DEPTH):
        pp, *tails = _in_proj(yp, g_in, w_in_b, l, bp, DENSE_TM, with_tail=True, tails=tails)
        (ps,) = _in_proj(ys, g_in, w_in_b, l, 1, DENSE_TM, with_tail=False)
        out_b, out_b_s = _attn(pp, ps, cache_kt, cache_vt, l, mult, bp)

        out_a, h_last = _mix_a_prompt(pp, *a_args, layer=l, batch=bp, tb=MIX_A_TB)
        out_c, ssd_state = _ssd_prompt(pp, *c_args, layer=l, batch=bp, tb=SSD_TB)
        yp = _post(out_a, out_b, out_c, yp, *post_w, layer=l, tm=DENSE_TM)
        pp3 = pp.reshape(bp, seq, N_IN_PAD)
        last = slice(seq - (CONV_W - 1), seq)
        p_new[0].append(h_last.reshape(bp, LRU_WIDTH))
        p_new[1].append(pp3[:, last, P_COLS["xa"][0]:P_COLS["xa"][0] + LRU_WIDTH])
        p_new[4].append(ssd_state)
        p_new[5].append(pp3[:, last, P_COLS["xbc"][0]:P_COLS["xbc"][0] + SSD_CONV_CH])

        out_a, h_all = _mix_a_decode(ps, buf_a, h0_a, *a_args, layer=l, tm=DENSE_TM)
        out_c, s_ssd = _ssd_decode(ps, buf_c, state_ssd, s_ssd, l, *c_args)
        ys = _post(out_a, out_b_s, out_c, ys, *post_w, layer=l, tm=DENSE_TM)
        ps3 = ps.reshape(bs, DEC_T, N_IN_PAD)
        last = slice(DEC_T - (CONV_W - 1), DEC_T)
        new_kv = lambda name: ps3[:, :, P_COLS[name][0]:P_COLS[name][0] + ATT_WIDTH].reshape(
            bs, DEC_T, ATT_HEADS, HEAD_DIM)
        s_new[0].append(h_all.reshape(bs, DEC_T, LRU_WIDTH)[:, DEC_T - 1])
        s_new[1].append(ps3[:, last, P_COLS["xa"][0]:P_COLS["xa"][0] + LRU_WIDTH])
        s_new[2].append(new_kv("k"))
        s_new[3].append(new_kv("v"))
        s_new[5].append(ps3[:, last, P_COLS["xbc"][0]:P_COLS["xbc"][0] + SSD_CONV_CH])

    stack = lambda parts: jnp.stack(parts) if parts else None
    outs_p = [stack(a) for a in p_new]
    outs_s = [stack(a) for a in s_new]
    outs_p[2], outs_p[3] = [jnp.transpose(t.reshape(DEPTH, bp, ATT_HEADS, HEAD_DIM, MAX_WINDOW), (0, 1, 4, 2, 3))
                            for t in tails]
    outs_s[4] = s_ssd
    return (yp.reshape(bp, seq, D_MODEL), ys.reshape(bs, DEC_T, D_MODEL), *outs_p, *outs_s)
```

```python
import functools

import numpy as np
import jax
import jax.numpy as jnp
from jax import lax
from jax.experimental import pallas as pl
from jax.experimental.pallas import tpu as pltpu

F32 = jnp.float32
BF16 = jnp.bfloat16

D_MODEL = 1024
DEPTH = 4
CONV_W = 4
HEAD_DIM = 64
ATT_WIDTH = 384
ATT_HEADS = 6
ATT_SPAN = 128
DILATIONS = (1, 4, 16)
MAX_WINDOW = 2048
ATT_SCALE = HEAD_DIM ** -0.5
SSD_WIDTH = 384
SSD_HEADS = 6
SSD_HEAD_DIM = 64
SSD_GROUPS = 2
SSD_STATE = 128
SSD_CHUNK = 128
SSD_CONV_CH = 896
LRU_WIDTH = 256
LRU_BLOCKS = 4
LRU_C = 8.0
D_FF = 2816
N_IN = 2950
EPS = 1e-6

LANES = 128
SUBLANES = 8
N_IN_PAD = 3072
VMEM_LIMIT = 56 * 1024 * 1024

W_IN_COLS = {"ga": (0, 256), "xa": (256, 256), "q": (512, 384), "k": (896, 384), "v": (1280, 384),
             "z": (1664, 384), "xbc": (2048, 896), "dt": (2944, 6)}
P_COLS = {"xbc": (0, 896), "dt": (896, 128), "ga": (1024, 256), "xa": (1280, 256), "q": (1536, 384),
          "k": (1920, 384), "v": (2304, 384), "z": (2688, 384)}


def _pcol(name, width=None):
    off, w = P_COLS[name]
    width = width or w
    assert off % width == 0
    return off // width

NEG_INF = float("-inf")
NT_DIMS = (((1,), (1,)), ((), ()))


def _cparams(sem):
    return pltpu.CompilerParams(dimension_semantics=sem, vmem_limit_bytes=VMEM_LIMIT)


def _const_spec(shape, layer=None):
    nd = len(shape)
    if layer is None:
        return pl.BlockSpec(shape, lambda *_: (0,) * nd, pipeline_mode=pl.Buffered(1))
    return pl.BlockSpec((None,) + tuple(shape), lambda *_: (layer,) + (0,) * nd, pipeline_mode=pl.Buffered(1))


def _stacked_alias(stacked, arg_index, out_index):
    if stacked is None:
        return [], [], {}
    return [pl.BlockSpec(memory_space=pl.ANY)], [stacked], {arg_index: out_index}


def _drop_refs(kernel_fn, first, count=1):
    def wrapped(*refs, **kw):
        return kernel_fn(*refs[:first], *refs[first + count:], **kw)
    return wrapped


def _rms(x, g):
    ms = jnp.mean(x * x, axis=-1, keepdims=True)
    return x * lax.rsqrt(ms + EPS) * g


def _sigmoid(x):
    return jax.nn.sigmoid(x)


def _silu(x):
    return x * jax.nn.sigmoid(x)


def _softplus(x):
    return jnp.maximum(x, 0.0) + jnp.log1p(jnp.exp(-jnp.abs(x)))


def _gelu_tanh(x):
    c = np.sqrt(2.0 / np.pi).astype(np.float32)
    return 0.5 * x * (1.0 + jnp.tanh(c * (x + 0.044715 * (x * x * x))))


def _roll_rows(x, shift):
    n = x.shape[0]
    shift = shift % n
    if shift == 0:
        return x
    return pltpu.roll(x, shift, 0)


def _row_index(shape, group=None):
    r = lax.broadcasted_iota(jnp.int32, shape, 0)
    if group is not None:
        r = jnp.bitwise_and(r, group - 1)
    return r


def _lin_scan(a, u, row, steps):
    s = 1
    for _ in range(steps):
        keep = row >= s
        a_sh = jnp.where(keep, _roll_rows(a, s), 1.0)
        u_sh = jnp.where(keep, _roll_rows(u, s), 0.0)
        u = a * u_sh + u
        a = a * a_sh
        s *= 2
    return a, u


def _cumsum_rows(x, row, steps):
    s = 1
    for _ in range(steps):
        x = x + jnp.where(row >= s, _roll_rows(x, s), 0.0)
        s *= 2
    return x


def _suffix_sum_rows(x, row, group, steps):
    incl = x
    s = 1
    for _ in range(steps):
        incl = incl + jnp.where(row < group - s, _roll_rows(incl, -s), 0.0)
        s *= 2
    return incl - x


def _conv_taps(x, shifted_fn, w_ref, b_ref):
    y = b_ref[...] + w_ref[CONV_W - 1:CONV_W, :] * x
    for s in range(1, CONV_W):
        y = y + w_ref[CONV_W - 1 - s:CONV_W - s, :] * shifted_fn(s)
    return y


def _conv_block_carry(x, tail_ref, w_ref, b_ref):
    tb, c = x.shape
    x3 = x.reshape(tb // SUBLANES, SUBLANES, c)
    sub = lax.broadcasted_iota(jnp.int32, (1, SUBLANES, c), 1)
    tail = tail_ref[...]

    def shifted(s):
        r = pltpu.roll(x3, s, 1)
        before = jnp.concatenate([pltpu.roll(tail, s, 0)[None], r[:-1]], axis=0)
        return jnp.where(sub < s, before, r)

    y = _conv_taps(x3, shifted, w_ref, b_ref)
    tail_ref[...] = x[tb - SUBLANES:tb]
    return y.reshape(tb, c)


def _conv_groups(x, bufpad, w_ref, b_ref):
    row = _row_index(x.shape, SUBLANES)
    return _conv_taps(
        x, lambda s: jnp.where(row >= s, _roll_rows(x, s), _roll_rows(bufpad, s - (CONV_W - 1))), w_ref, b_ref)


def _lru_gates(xc, wa_ref, ba_ref, wx_ref, bx_ref, lam_ref):
    xb = xc.astype(BF16)
    r = _sigmoid(jnp.dot(xb, wa_ref[...], preferred_element_type=F32) + ba_ref[...])
    ig = _sigmoid(jnp.dot(xb, wx_ref[...], preferred_element_type=F32) + bx_ref[...])
    log_a = (-LRU_C) * r * _softplus(-lam_ref[...])
    a = jnp.exp(log_a)
    t = jnp.tanh(log_a)
    u = jnp.sqrt(-2.0 * t / (1.0 - t)) * (ig * xc)
    return a, u


def _in_proj_kernel(x_ref, g_ref, w_ref, p_ref, *tail_refs, first_tail):
    h = _rms(x_ref[...], g_ref[...]).astype(BF16)
    p_ref[...] = jnp.dot(h, w_ref[...], preferred_element_type=F32)
    if tail_refs:
        kt_ref, vt_ref = tail_refs

        @pl.when(pl.program_id(1) >= first_tail)
        def _():
            k_off, v_off = P_COLS["k"][0], P_COLS["v"][0]
            kt_ref[...] = p_ref[:, k_off:k_off + ATT_WIDTH].T
            vt_ref[...] = p_ref[:, v_off:v_off + ATT_WIDTH].T


def _in_proj(x, g, w, layer, batch, tm, with_tail, tails=None):
    n = x.shape[0]
    nt = n // batch // tm
    first_tail = nt - MAX_WINDOW // tm
    row = lambda width: pl.BlockSpec((tm, width), lambda b, j: (b * nt + j, 0))
    out_specs = [row(N_IN_PAD)]
    out_shape = [jax.ShapeDtypeStruct((n, N_IN_PAD), F32)]
    if with_tail:
        tail = pl.BlockSpec((None, None, ATT_WIDTH, tm),
                            lambda b, j: (layer, b, 0, jnp.maximum(j - first_tail, 0)))
        out_specs += [tail, tail]
        out_shape += [jax.ShapeDtypeStruct((DEPTH, batch, ATT_WIDTH, MAX_WINDOW), F32)] * 2
    body = functools.partial(_in_proj_kernel, first_tail=first_tail)
    extra_specs, extra_args, aliases = [], [], {}
    if tails is not None:
        body = _drop_refs(body, 3, 2)
        extra_specs = [pl.BlockSpec(memory_space=pl.ANY)] * 2
        extra_args = list(tails)
        aliases = {3: 1, 4: 2}
    return pl.pallas_call(
        body,
        grid=(batch, nt),
        in_specs=[row(D_MODEL), _const_spec((1, D_MODEL), layer), _const_spec((D_MODEL, N_IN_PAD), layer)]
        + extra_specs,
        out_specs=out_specs,
        out_shape=out_shape,
        input_output_aliases=aliases,
        compiler_params=_cparams(("parallel", "arbitrary")),
        name="in_proj",
    )(x, g, w, *extra_args)


FFN_CHUNK = 256


def _post_kernel(a_ref, b_ref, c_ref, x_ref, wo_ref, go_ref, gi_ref, wgu_ref, wd_ref, gf_ref, o_ref):
    mixed = jnp.concatenate([a_ref[...], b_ref[...], c_ref[...]], axis=1).astype(BF16)
    x1 = x_ref[...] + _rms(jnp.dot(mixed, wo_ref[...], preferred_element_type=F32), go_ref[...])
    h = _rms(x1, gi_ref[...]).astype(BF16)
    acc = jnp.zeros(x1.shape, F32)
    for c in range(D_FF // FFN_CHUNK):
        lo = c * FFN_CHUNK
        g = jnp.dot(h, wgu_ref[:, lo:lo + FFN_CHUNK], preferred_element_type=F32)
        u = jnp.dot(h, wgu_ref[:, D_FF + lo:D_FF + lo + FFN_CHUNK], preferred_element_type=F32)
        act = (_silu(g) * u).astype(BF16)
        acc = acc + jnp.dot(act, wd_ref[lo:lo + FFN_CHUNK, :], preferred_element_type=F32)
    o_ref[...] = x1 + _rms(acc, gf_ref[...])


def _post(a, b, c, x, wo, go, gi, wgu, wd, gf, layer, tm):
    n = x.shape[0]
    row = lambda width: pl.BlockSpec((tm, width), lambda i: (i, 0))
    vec = _const_spec((1, D_MODEL), layer)
    return pl.pallas_call(
        _post_kernel,
        grid=(n // tm,),
        in_specs=[row(LRU_WIDTH), row(ATT_WIDTH), row(SSD_WIDTH), row(D_MODEL),
                  _const_spec((D_MODEL, D_MODEL), layer), vec, vec,
                  _const_spec((D_MODEL, 2 * D_FF), layer), _const_spec((D_FF, D_MODEL), layer), vec],
        out_specs=row(D_MODEL),
        out_shape=jax.ShapeDtypeStruct((n, D_MODEL), F32),
        compiler_params=_cparams(("parallel",)),
        name="post",
    )(a, b, c, x, wo, go, gi, wgu, wd, gf)


def _mix_a_prompt_kernel(xa_ref, ga_ref, cw_ref, cb_ref, wa_ref, ba_ref, wx_ref, bx_ref, lam_ref,
                         out_ref, hlast_ref, tail_ref, hc_ref):
    @pl.when(pl.program_id(1) == 0)
    def _():
        tail_ref[...] = jnp.zeros_like(tail_ref)
        hc_ref[...] = jnp.zeros_like(hc_ref)

    x = xa_ref[...]
    tb, c = x.shape
    xc = _conv_block_carry(x, tail_ref, cw_ref, cb_ref)
    a, u = _lru_gates(xc, wa_ref, ba_ref, wx_ref, bx_ref, lam_ref)
    gate = _gelu_tanh(ga_ref[...])
    nslab = tb // SUBLANES
    a3 = a.reshape(nslab, SUBLANES, c)
    u3 = u.reshape(nslab, SUBLANES, c)
    sub = lax.broadcasted_iota(jnp.int32, (1, SUBLANES, c), 1)
    s = 1
    while s < SUBLANES:
        keep = sub >= s
        a_sh = jnp.where(keep, pltpu.roll(a3, s, 1), 1.0)
        u_sh = jnp.where(keep, pltpu.roll(u3, s, 1), 0.0)
        u3 = a3 * u_sh + u3
        a3 = a3 * a_sh
        s *= 2
    h_row = hc_ref[0:1, :]
    for k in range(nslab):
        h = u3[k] + a3[k] * h_row
        h_row = h[SUBLANES - 1:SUBLANES, :]
        rows = slice(k * SUBLANES, (k + 1) * SUBLANES)
        out_ref[rows, :] = h * gate[rows, :]
    hc_ref[...] = jnp.broadcast_to(h_row, hc_ref.shape)
    hlast_ref[...] = h_row


def _mix_a_prompt(p, cw, cb, wa, ba, wx, bx, lam, layer, batch, tb):
    n = p.shape[0]
    nb = n // batch // tb
    blk = pl.BlockSpec((tb, LRU_WIDTH), lambda b, i: (b * nb + i, 0))
    col = lambda name: pl.BlockSpec((tb, LRU_WIDTH), lambda b, i: (b * nb + i, _pcol(name)))
    vec = _const_spec((1, LRU_WIDTH), layer)
    mat = _const_spec((LRU_WIDTH, LRU_WIDTH), layer)
    return pl.pallas_call(
        _mix_a_prompt_kernel,
        grid=(batch, nb),
        in_specs=[col("xa"), col("ga"), _const_spec((CONV_W, LRU_WIDTH), layer), vec, mat, vec, mat, vec, vec],
        out_specs=[blk, pl.BlockSpec((None, 1, LRU_WIDTH), lambda b, i: (b, 0, 0))],
        out_shape=[jax.ShapeDtypeStruct((n, LRU_WIDTH), F32),
                   jax.ShapeDtypeStruct((batch, 1, LRU_WIDTH), F32)],
        scratch_shapes=[pltpu.VMEM((SUBLANES, LRU_WIDTH), F32), pltpu.VMEM((SUBLANES, LRU_WIDTH), F32)],
        compiler_params=_cparams(("parallel", "arbitrary")),
        name="mix_a_prompt",
    )(p, p, cw, cb, wa, ba, wx, bx, lam)


def _mix_a_decode_kernel(xa_ref, ga_ref, buf_ref, h0_ref, cw_ref, cb_ref, wa_ref, ba_ref, wx_ref, bx_ref,
                         lam_ref, out_ref, h_ref):
    x = xa_ref[...]
    xc = _conv_groups(x, buf_ref[...], cw_ref, cb_ref)
    a, u = _lru_gates(xc, wa_ref, ba_ref, wx_ref, bx_ref, lam_ref)
    u = u + a * h0_ref[...]
    row = _row_index(x.shape, SUBLANES)
    _, h = _lin_scan(a, u, row, 3)
    h_ref[...] = h
    out_ref[...] = h * _gelu_tanh(ga_ref[...])


def _mix_a_decode(p, bufpad, h0pad, cw, cb, wa, ba, wx, bx, lam, layer, tm):
    n = p.shape[0]
    blk = pl.BlockSpec((tm, LRU_WIDTH), lambda i: (i, 0))
    col = lambda name: pl.BlockSpec((tm, LRU_WIDTH), lambda i: (i, _pcol(name)))
    lay = pl.BlockSpec((None, tm, LRU_WIDTH), lambda i: (layer, i, 0))
    vec = _const_spec((1, LRU_WIDTH), layer)
    mat = _const_spec((LRU_WIDTH, LRU_WIDTH), layer)
    return pl.pallas_call(
        _mix_a_decode_kernel,
        grid=(n // tm,),
        in_specs=[col("xa"), col("ga"), lay, lay, _const_spec((CONV_W, LRU_WIDTH), layer), vec, mat, vec, mat, vec,
                  vec],
        out_specs=[blk, blk],
        out_shape=[jax.ShapeDtypeStruct((n, LRU_WIDTH), F32), jax.ShapeDtypeStruct((n, LRU_WIDTH), F32)],
        compiler_params=_cparams(("parallel",)),
        name="mix_a_decode",
    )(p, p, bufpad, h0pad, cw, cb, wa, ba, wx, bx, lam)


ATT_BLK = 2048
ATT_UNIT = 128
ATT_UNROLL = 4


def _attn_scores(q_t, kp_t, kc_t, bias, lane_lo):
    zero = jnp.zeros_like(q_t)
    qs = q_t * ATT_SCALE
    qq = jnp.concatenate([jnp.where(lane_lo, qs, zero), jnp.where(lane_lo, zero, qs)], axis=0).astype(BF16)
    kk = jnp.concatenate([kp_t, kc_t], axis=0).astype(BF16)
    s = lax.dot_general(qq, kk, NT_DIMS, preferred_element_type=F32) + bias
    m = jnp.max(s, axis=1, keepdims=True)
    return jnp.exp(s - m).astype(BF16), m


def _attn_output(p, m, vp_t, vc_t, lane_lo):
    zero = jnp.zeros_like(vp_t)
    pcat = jnp.concatenate([p[:ATT_UNIT], p[ATT_UNIT:]], axis=1)
    one_lo = jnp.where(lane_lo, 1.0, 0.0)
    one_hi = 1.0 - one_lo
    w = jnp.concatenate([
        jnp.concatenate([jnp.where(lane_lo, vp_t, zero), one_lo], axis=1),
        jnp.concatenate([jnp.where(lane_lo, vc_t, zero), one_lo], axis=1),
        jnp.concatenate([jnp.where(lane_lo, zero, vp_t), one_hi], axis=1),
        jnp.concatenate([jnp.where(lane_lo, zero, vc_t), one_hi], axis=1)], axis=0).astype(BF16)
    ol = jnp.dot(pcat, w, preferred_element_type=F32)
    m_t = jnp.where(lane_lo, jnp.broadcast_to(m[:ATT_UNIT], vp_t.shape),
                    jnp.broadcast_to(m[ATT_UNIT:], vp_t.shape))
    return ol[:, :LANES], m_t, ol[:, LANES:]


DEC_T = 8
DEC_NEW_PAD = 128
DEC_KEYS = MAX_WINDOW + DEC_NEW_PAD
DEC_ROWS = ATT_HEADS * DEC_T
ATT_PHASES = len(DILATIONS)


def _decode_key_multiplicity():
    pos = np.full((DEC_KEYS,), -10 ** 9, np.int64)
    pos[:MAX_WINDOW] = np.arange(MAX_WINDOW)
    pos[MAX_WINDOW:MAX_WINDOW + DEC_T] = MAX_WINDOW + np.arange(DEC_T)
    t = np.arange(DEC_T)
    dist = (MAX_WINDOW + t)[:, None] - pos[None, :]
    cnt = np.zeros(dist.shape, np.float32)
    for d in DILATIONS:
        cnt += ((dist >= 0) & (dist % d == 0) & (dist <= ATT_SPAN * d)).astype(np.float32)
    return np.tile(cnt, (ATT_HEADS, 1))


def _attn_decode_block(q_ref, kn_ref, vn_ref, kt_ref, vt_ref, mult_ref, o_ref):
    lane = lax.broadcasted_iota(jnp.int32, (DEC_T, ATT_WIDTH), 1)
    head_masks = [jnp.logical_and(lane >= HEAD_DIM * h, lane < HEAD_DIM * (h + 1)) for h in range(ATT_HEADS)]
    mult = mult_ref[...]
    seen = mult > 0.0
    zpad = jnp.zeros((DEC_NEW_PAD - DEC_T, ATT_WIDTH), F32)
    pending = []
    for b in range(kt_ref.shape[0]):
        new_rows = slice(DEC_T * b, DEC_T * (b + 1))
        qb = q_ref[new_rows, :] * ATT_SCALE
        qbd = jnp.concatenate([jnp.where(hm, qb, 0.0) for hm in head_masks], axis=0).astype(BF16)
        kt = kt_ref[b].reshape(ATT_WIDTH, MAX_WINDOW).astype(BF16)
        kn = jnp.concatenate([kn_ref[new_rows, :], zpad], axis=0).astype(BF16)
        s = jnp.concatenate([jnp.dot(qbd, kt, preferred_element_type=F32),
                             lax.dot_general(qbd, kn, NT_DIMS, preferred_element_type=F32)], axis=1)
        s = jnp.where(seen, s, NEG_INF)
        m = jnp.max(s, axis=1, keepdims=True)
        p = mult * jnp.exp(s - m)
        pending.append((p / jnp.sum(p, axis=1, keepdims=True)).astype(BF16))
    for b, p in enumerate(pending):
        new_rows = slice(DEC_T * b, DEC_T * (b + 1))
        vt = vt_ref[b].reshape(ATT_WIDTH, MAX_WINDOW).astype(BF16)
        vn = jnp.concatenate([vn_ref[new_rows, :], zpad], axis=0).astype(BF16)
        o = (lax.dot_general(p[:, :MAX_WINDOW], vt, NT_DIMS, preferred_element_type=F32)
             + jnp.dot(p[:, MAX_WINDOW:], vn, preferred_element_type=F32))
        out = jnp.zeros((DEC_T, ATT_WIDTH), F32)
        for h, hm in enumerate(head_masks):
            out = out + jnp.where(hm, o[DEC_T * h:DEC_T * (h + 1), :], 0.0)
        o_ref[new_rows, :] = out


def _attn_kernel(q_ref, kp_ref, kc_ref, vp_ref, vc_ref, qn_ref, kn_ref, vn_ref, kt_ref, vt_ref, mult_ref,
                 o_ref, od_ref, acc_ref, m_ref, l_ref, *, dec_blocks):
    sub = pl.program_id(3)
    step = ((pl.program_id(0) * pl.num_programs(1) + pl.program_id(1)) * pl.num_programs(2)
            + pl.program_id(2)) * ATT_PHASES + sub

    @pl.when(step < dec_blocks)
    def _():
        _attn_decode_block(qn_ref, kn_ref, vn_ref, kt_ref, vt_ref, mult_ref, od_ref)

    first_block = pl.program_id(2) == 0
    lane_lo = lax.broadcasted_iota(jnp.int32, (ATT_UNIT, LANES), 1) < HEAD_DIM
    qi = lax.broadcasted_iota(jnp.int32, (2 * ATT_UNIT, 2 * ATT_UNIT), 0) & (ATT_UNIT - 1)
    ki = lax.broadcasted_iota(jnp.int32, (2 * ATT_UNIT, 2 * ATT_UNIT), 1)
    is_prev = ki < ATT_UNIT
    dist = qi - ki + ATT_UNIT
    bias = jnp.where(dist >= 0, jnp.where(dist <= ATT_SPAN, 0.0, NEG_INF), NEG_INF)
    bias_first = bias + jnp.where(is_prev, jnp.where(first_block, NEG_INF, 0.0), 0.0)

    def rows(start, d):
        if d == 1:
            return pl.ds(start, ATT_UNIT)
        return pl.ds(start, ATT_UNIT, stride=d)

    def head_scores(bi, rho):
        d = DILATIONS[bi]
        cur = rows(rho, d)
        prev = rows(rho + ATT_BLK - d * ATT_UNIT, d)
        return (cur, prev, vp_ref) + _attn_scores(q_ref[cur, :], kp_ref[prev, :], kc_ref[cur, :], bias_first, lane_lo)

    def inner_scores(bi, idx):
        d = DILATIONS[bi]
        rho = idx & (d - 1)
        j = 1 + (idx >> int(np.log2(d)))
        start = rho + d * ATT_UNIT * j
        cur = rows(start, d)
        prev = rows(start - d * ATT_UNIT, d)
        return (cur, prev, vc_ref) + _attn_scores(q_ref[cur, :], kc_ref[prev, :], kc_ref[cur, :], bias, lane_lo)

    def group(bi, pending):
        for cur, prev, vprev_ref, p, m in pending:
            o_t, m_t, l_t = _attn_output(p, m, vprev_ref[prev, :], vc_ref[cur, :], lane_lo)
            acc_ref[bi, cur, :] = o_t
            m_ref[bi, cur, :] = m_t
            l_ref[bi, cur, :] = l_t

    def chunks(scores, bi, base, lo, hi):
        def body(c, carry):
            group(bi, [scores(bi, base + ATT_UNROLL * c + u) for u in range(ATT_UNROLL)])
            return carry
        lax.fori_loop(lo, hi, body, 0)

    units = ATT_BLK // ATT_UNIT
    @pl.when(sub == 0)
    def _():
        group(0, [head_scores(0, 0)] + [inner_scores(0, idx) for idx in range(ATT_UNROLL - 1)])
        chunks(inner_scores, 0, ATT_UNROLL - 1, 0, (units - ATT_UNROLL) // ATT_UNROLL)

    @pl.when(sub == 1)
    def _():
        group(1, [head_scores(1, rho) for rho in range(DILATIONS[1])])
        chunks(inner_scores, 1, 0, 0, (units - DILATIONS[1]) // ATT_UNROLL)

    @pl.when(sub == 2)
    def _():
        chunks(head_scores, 2, 0, 0, units // ATT_UNROLL)

        def combine(c, carry):
            sl = pl.ds(pl.multiple_of(c * ATT_UNIT, ATT_UNIT), ATT_UNIT)
            ms = [m_ref[bi, sl, :] for bi in range(len(DILATIONS))]
            m = functools.reduce(jnp.maximum, ms)
            ws = [jnp.exp(mi - m) for mi in ms]
            num = sum(w * acc_ref[bi, sl, :] for bi, w in enumerate(ws))
            den = sum(w * l_ref[bi, sl, :] for bi, w in enumerate(ws))
            o_ref[sl, :] = num / den
            return carry

        lax.fori_loop(0, ATT_BLK // ATT_UNIT, combine, 0, unroll=2)


def _attn(pp, ps, cache_kt, cache_vt, layer, mult, batch):
    n = pp.shape[0]
    nb = n // batch // ATT_BLK
    npair = ATT_WIDTH // LANES
    nd = ps.shape[0]
    nseq = nd // DEC_T
    steps = batch * npair * nb * ATT_PHASES
    spb = -(-nseq // steps)
    assert nseq % spb == 0
    dec_blocks = nseq // spb

    def dec_blk(b, hp, i, s):
        return jnp.minimum(((b * npair + hp) * nb + i) * ATT_PHASES + s, dec_blocks - 1)

    out = pl.BlockSpec((ATT_BLK, LANES), lambda b, hp, i, s: (b * nb + i, hp))
    cur = lambda name: pl.BlockSpec((ATT_BLK, LANES), lambda b, hp, i, s: (b * nb + i, _pcol(name, LANES) + hp))
    prev = lambda name: pl.BlockSpec(
        (ATT_BLK, LANES), lambda b, hp, i, s: (b * nb + jnp.maximum(i - 1, 0), _pcol(name, LANES) + hp))
    new = lambda name: pl.BlockSpec((DEC_T * spb, ATT_WIDTH), lambda b, hp, i, s: (dec_blk(b, hp, i, s), _pcol(name)))
    win = pl.BlockSpec((None, spb, ATT_HEADS, HEAD_DIM, MAX_WINDOW),
                       lambda b, hp, i, s: (layer, dec_blk(b, hp, i, s), 0, 0, 0))
    scratch = pltpu.VMEM((len(DILATIONS), ATT_BLK, LANES), F32)
    return pl.pallas_call(
        functools.partial(_attn_kernel, dec_blocks=dec_blocks),
        grid=(batch, npair, nb, ATT_PHASES),
        in_specs=[cur("q"), prev("k"), cur("k"), prev("v"), cur("v"), new("q"), new("k"), new("v"), win, win,
                  _const_spec((DEC_ROWS, DEC_KEYS))],
        out_specs=[out, pl.BlockSpec((DEC_T * spb, ATT_WIDTH), lambda b, hp, i, s: (dec_blk(b, hp, i, s), 0))],
        out_shape=[jax.ShapeDtypeStruct((n, ATT_WIDTH), F32), jax.ShapeDtypeStruct((nd, ATT_WIDTH), F32)],
        scratch_shapes=[scratch, scratch, scratch],
        compiler_params=_cparams(("arbitrary", "arbitrary", "arbitrary", "arbitrary")),
        name="attn",
    )(pp, pp, pp, pp, pp, ps, ps, ps, cache_kt, cache_vt, mult)


HPG = SSD_HEADS // SSD_GROUPS
B_OFF = SSD_WIDTH
C_OFF = SSD_WIDTH + SSD_GROUPS * SSD_STATE


def _ssd_chunk_diag(xbc, dt, acum, pair_ok):
    acum_t = acum.T
    ys, xrs = [], []
    for g in range(SSD_GROUPS):
        bg = xbc[:, B_OFF + g * SSD_STATE:B_OFF + (g + 1) * SSD_STATE].astype(BF16)
        cg = xbc[:, C_OFF + g * SSD_STATE:C_OFF + (g + 1) * SSD_STATE].astype(BF16)
        cb = lax.dot_general(cg, bg, NT_DIMS, preferred_element_type=F32)
        for h in range(g * HPG, (g + 1) * HPG):
            xr = xbc[:, h * SSD_HEAD_DIM:(h + 1) * SSD_HEAD_DIM] * dt[:, h:h + 1]
            diff = acum[:, h:h + 1] - acum_t[h:h + 1, :]
            lmat = jnp.exp(jnp.where(pair_ok, diff, NEG_INF))
            ys.append(jnp.dot((cb * lmat).astype(BF16), xr.astype(BF16), preferred_element_type=F32))
            xrs.append(xr)
    return ys, xrs


def _ssd_finish(y, xs, z, dskip_ref, norm_ref):
    y = y + dskip_ref[...] * xs
    y = y * _silu(z)
    return _rms(y, norm_ref[...])


DT_COPIES = 3
SSD_PAIRS = SSD_HEADS // 2
SSD_CONV_ROWS = 128


def _ssd_prompt_kernel(xbc_ref, z_ref, dt_ref, cw_ref, cb_ref, dtb_ref, alog_ref, dskip_ref, norm_ref,
                       out_ref, state_ref, tail_ref, xc_ref, st_ref):
    tb = xbc_ref.shape[0]

    @pl.when(pl.program_id(1) == 0)
    def _():
        tail_ref[...] = jnp.zeros_like(tail_ref)
        st_ref[...] = jnp.zeros_like(st_ref)

    for r0 in range(0, tb, SSD_CONV_ROWS):
        xc_ref[r0:r0 + SSD_CONV_ROWS, :] = _silu(
            _conv_block_carry(xbc_ref[r0:r0 + SSD_CONV_ROWS, :], tail_ref, cw_ref, cb_ref))
    a_neg = -jnp.exp(alog_ref[...])
    row = _row_index((SSD_CHUNK, LANES))
    lane = lax.broadcasted_iota(jnp.int32, (SSD_CHUNK, LANES), 1)
    lane_lo = lane < SSD_HEAD_DIM
    causal = row >= lane
    zero = jnp.zeros((SSD_CHUNK, LANES), F32)

    def pair_rows(t):
        return jnp.concatenate([jnp.where(lane_lo, t, zero), jnp.where(lane_lo, zero, t)], axis=0).astype(BF16)

    def chunk(c, carry):
        r0 = pl.multiple_of(c * SSD_CHUNK, SSD_CHUNK)
        rows = pl.ds(r0, SSD_CHUNK)
        xc = xc_ref[rows, :]
        dt = _softplus(dt_ref[rows, :] + dtb_ref[...])
        acum = _cumsum_rows(dt * a_neg, row, 7)
        last = acum[SSD_CHUNK - 1:SSD_CHUNK, :]
        tot = jnp.exp(last)
        pt = jnp.where(lane < SUBLANES, acum, jnp.where(lane < 2 * SUBLANES, dt, dt * jnp.exp(last - acum))).T
        bs = [xc[:, B_OFF + g * SSD_STATE:B_OFF + (g + 1) * SSD_STATE] for g in range(SSD_GROUPS)]
        cs = [xc[:, C_OFF + g * SSD_STATE:C_OFF + (g + 1) * SSD_STATE].astype(BF16) for g in range(SSD_GROUPS)]
        bts = [b.T for b in bs]
        cbs = [lax.dot_general(cs[g], bs[g].astype(BF16), NT_DIMS, preferred_element_type=F32)
               for g in range(SSD_GROUPS)]
        gs, eacs, btws = [], [], []
        for h in range(SSD_HEADS):
            g = h // HPG
            a_col = jnp.broadcast_to(acum[:, h:h + 1], (SSD_CHUNK, SSD_CHUNK))
            lmat = jnp.exp(jnp.where(causal, a_col - pt[h:h + 1, :], NEG_INF))
            gs.append((cbs[g] * lmat * pt[SUBLANES + h:SUBLANES + h + 1, :]).astype(BF16))
            eacs.append(jnp.exp(a_col))
            btws.append((bts[g] * pt[2 * SUBLANES + h:2 * SUBLANES + h + 1, :]).astype(BF16))
        outs = []
        for k in range(SSD_PAIRS):
            h0, h1 = 2 * k, 2 * k + 1
            g0, g1 = h0 // HPG, h1 // HPG
            x2 = pair_rows(xc[:, k * LANES:(k + 1) * LANES])
            y = jnp.dot(jnp.concatenate([gs[h0], gs[h1]], axis=1), x2, preferred_element_type=F32)
            st = st_ref[k]
            if g0 == g1:
                y_off = jnp.dot(cs[g0], st.astype(BF16), preferred_element_type=F32)
            else:
                y_off = jnp.dot(jnp.concatenate([cs[g0], cs[g1]], axis=1), pair_rows(st),
                                preferred_element_type=F32)
            outs.append(y + y_off * jnp.where(lane_lo, eacs[h0], eacs[h1]))
            upd = jnp.dot(jnp.concatenate([btws[h0], btws[h1]], axis=1), x2, preferred_element_type=F32)
            st_ref[k] = st * jnp.where(lane_lo, tot[:, h0:h0 + 1], tot[:, h1:h1 + 1]) + upd
        y = jnp.concatenate(outs, axis=1)
        out_ref[rows, :] = _ssd_finish(y, xc[:, :SSD_WIDTH], z_ref[rows, :], dskip_ref, norm_ref)
        return carry

    lax.fori_loop(0, tb // SSD_CHUNK, chunk, 0)
    for k in range(SSD_PAIRS):
        t = st_ref[k].T
        state_ref[2 * k] = t[:SSD_HEAD_DIM]
        state_ref[2 * k + 1] = t[SSD_HEAD_DIM:]


def _ssd_prompt(p, cw, cb, dtb, alog, dskip, norm, layer, batch, tb):
    n = p.shape[0]
    nb = n // batch // tb
    blk = lambda width: pl.BlockSpec((tb, width), lambda b, i: (b * nb + i, 0))
    col = lambda name: pl.BlockSpec((tb, P_COLS[name][1]), lambda b, i: (b * nb + i, _pcol(name)))
    return pl.pallas_call(
        _ssd_prompt_kernel,
        grid=(batch, nb),
        in_specs=[col("xbc"), col("z"), col("dt"),
                  _const_spec((CONV_W, SSD_CONV_CH), layer), _const_spec((1, SSD_CONV_CH), layer),
                  _const_spec((1, LANES), layer), _const_spec((1, LANES), layer),
                  _const_spec((1, SSD_WIDTH), layer), _const_spec((1, SSD_WIDTH), layer)],
        out_specs=[blk(SSD_WIDTH),
                   pl.BlockSpec((None, SSD_HEADS, SSD_HEAD_DIM, SSD_STATE), lambda b, i: (b, 0, 0, 0))],
        out_shape=[jax.ShapeDtypeStruct((n, SSD_WIDTH), F32),
                   jax.ShapeDtypeStruct((batch, SSD_HEADS, SSD_HEAD_DIM, SSD_STATE), F32)],
        scratch_shapes=[pltpu.VMEM((SUBLANES, SSD_CONV_CH), F32),
                        pltpu.VMEM((tb, SSD_CONV_CH), F32),
                        pltpu.VMEM((SSD_PAIRS, SSD_STATE, LANES), F32)],
        compiler_params=_cparams(("parallel", "arbitrary")),
        name="ssd_prompt",
    )(p, p, p, cw, cb, dtb, alog, dskip, norm)


SSD_DEC_SEQ = SSD_CHUNK // DEC_T


def _ssd_decode_kernel(xbc_ref, z_ref, dt_ref, buf_ref, h0_ref, cw_ref, cb_ref, dtb_ref, alog_ref, dskip_ref,
                       norm_ref, out_ref, hnew_ref, xc_ref, xrd_ref, eac_ref, tot_ref, yoff_ref):
    xc = _silu(_conv_groups(xbc_ref[...], buf_ref[...], cw_ref, cb_ref))
    xc_ref[...] = xc
    a_neg = -jnp.exp(alog_ref[...])
    row = _row_index((SSD_CHUNK, LANES), DEC_T)
    li = lax.broadcasted_iota(jnp.int32, (SSD_CHUNK, SSD_CHUNK), 0)
    si = lax.broadcasted_iota(jnp.int32, (SSD_CHUNK, SSD_CHUNK), 1)
    same_seq_causal = jnp.logical_and(li >= si, (li - si) <= (li & (DEC_T - 1)))
    dt = _softplus(dt_ref[...] + dtb_ref[...])
    dta = dt * a_neg
    acum = _cumsum_rows(dta, row, 3)
    rest = _suffix_sum_rows(dta, row, DEC_T, 3)
    eac_ref[...] = jnp.exp(acum)
    tot_ref[...] = jnp.exp(acum + rest)
    decay = jnp.exp(rest)
    ys, xrs = _ssd_chunk_diag(xc, dt, acum, same_seq_causal)
    xrd_ref[...] = jnp.concatenate([xrs[h] * decay[:, h:h + 1] for h in range(SSD_HEADS)], axis=1).T
    seq_of_lane = lax.shift_right_logical(lax.broadcasted_iota(jnp.int32, (SSD_HEAD_DIM, SSD_CHUNK), 1), 3)
    b_all = [xc[:, B_OFF + g * SSD_STATE:B_OFF + (g + 1) * SSD_STATE].astype(BF16) for g in range(SSD_GROUPS)]

    def seq(b, carry):
        r0 = pl.multiple_of(b * DEC_T, DEC_T)
        rows = pl.ds(r0, DEC_T)
        xcb = xc_ref[rows, :]
        eac = eac_ref[rows, :]
        tot = tot_ref[rows, :]
        own = seq_of_lane == b
        outs = []
        for h in range(SSD_HEADS):
            g = h // HPG
            cg = xcb[:, C_OFF + g * SSD_STATE:C_OFF + (g + 1) * SSD_STATE].astype(BF16)
            prev = h0_ref[b, h]
            outs.append(lax.dot_general(cg, prev.astype(BF16), NT_DIMS, preferred_element_type=F32)
                        * eac[:, h:h + 1])
            lhs = jnp.where(own, xrd_ref[h * SSD_HEAD_DIM:(h + 1) * SSD_HEAD_DIM, :], 0.0).astype(BF16)
            st = jnp.dot(lhs, b_all[g], preferred_element_type=F32)
            hnew_ref[b, h] = prev * tot[0:1, h:h + 1] + st
        yoff_ref[rows, :] = jnp.concatenate(outs, axis=1)
        return carry

    lax.fori_loop(0, SSD_DEC_SEQ, seq, 0, unroll=4)
    y = jnp.concatenate(ys, axis=1) + yoff_ref[...]
    out_ref[...] = _ssd_finish(y, xc[:, :SSD_WIDTH], z_ref[...], dskip_ref, norm_ref)


def _ssd_decode(p, bufpad, state, stacked, layer, cw, cb, dtb, alog, dskip, norm):
    n = p.shape[0]
    blk = lambda width: pl.BlockSpec((SSD_CHUNK, width), lambda i: (i, 0))
    col = lambda name: pl.BlockSpec((SSD_CHUNK, P_COLS[name][1]), lambda i: (i, _pcol(name)))
    st_in = pl.BlockSpec((None, SSD_DEC_SEQ, SSD_HEADS, SSD_HEAD_DIM, SSD_STATE), lambda i: (layer, i, 0, 0, 0))
    extra_specs, extra_args, aliases = _stacked_alias(stacked, 11, 1)
    return pl.pallas_call(
        _ssd_decode_kernel if stacked is None else _drop_refs(_ssd_decode_kernel, 11),
        grid=(n // SSD_CHUNK,),
        in_specs=[col("xbc"), col("z"), col("dt"),
                  pl.BlockSpec((None, SSD_CHUNK, SSD_CONV_CH), lambda i: (layer, i, 0)), st_in,
                  _const_spec((CONV_W, SSD_CONV_CH), layer), _const_spec((1, SSD_CONV_CH), layer),
                  _const_spec((1, LANES), layer), _const_spec((1, LANES), layer),
                  _const_spec((1, SSD_WIDTH), layer), _const_spec((1, SSD_WIDTH), layer)] + extra_specs,
        out_specs=[blk(SSD_WIDTH), st_in],
        out_shape=[jax.ShapeDtypeStruct((n, SSD_WIDTH), F32), jax.ShapeDtypeStruct(state.shape, F32)],
        input_output_aliases=aliases,
        scratch_shapes=[pltpu.VMEM((SSD_CHUNK, SSD_CONV_CH), F32),
                        pltpu.VMEM((SSD_WIDTH, SSD_CHUNK), F32),
                        pltpu.VMEM((SSD_CHUNK, LANES), F32),
                        pltpu.VMEM((SSD_CHUNK, LANES), F32),
                        pltpu.VMEM((SSD_CHUNK, SSD_WIDTH), F32)],
        compiler_params=_cparams(("parallel",)),
        name="ssd_decode",
    )(p, p, p, bufpad, state, cw, cb, dtb, alog, dskip, norm, *extra_args)


DENSE_TM = 512
MIX_A_TB = 512
SSD_TB = 1024


def _pad_state_rows(buf):
    l, b, r, c = buf.shape
    return jnp.pad(buf, ((0, 0), (0, 0), (0, DEC_T - r), (0, 0))).reshape(l, b * DEC_T, c)


def kernel(x_prompt, x_sample, state_lru_h, state_lru_conv, cache_swa_k, cache_swa_v, state_ssd, state_ssd_conv,
           norm_mix_in, norm_mix_out, w_in, conv_a_w, conv_a_b, lru_wa, lru_ba, lru_wx, lru_bx, lru_lambda,
           conv_c_w, conv_c_b, dt_bias, a_log, d_skip, ssm_norm, w_out, norm_ffn_in, norm_ffn_out,
           w_gate_up, w_down):
    bp, seq, _ = x_prompt.shape
    bs, dec_t, _ = x_sample.shape
    assert dec_t == DEC_T and seq % ATT_BLK == 0 and seq >= MAX_WINDOW and bs % SSD_DEC_SEQ == 0

    def dt_lanes(p):
        slot = jnp.pad(p, [(0, 0)] * (p.ndim - 1) + [(0, SUBLANES - SSD_HEADS)])
        rep = jnp.concatenate([slot] * DT_COPIES, axis=-1)
        return jnp.pad(rep, [(0, 0)] * (p.ndim - 1) + [(0, LANES - DT_COPIES * SUBLANES)])

    def w_cols(name):
        off, width = W_IN_COLS[name]
        blk = w_in[:, :, off:off + width]
        if name == "dt":
            return dt_lanes(blk)
        return jnp.pad(blk, ((0, 0), (0, 0), (0, P_COLS[name][1] - width)))

    w_in_b = jnp.concatenate([w_cols(name) for name in P_COLS], axis=2).astype(BF16)
    w_out_b = w_out.astype(BF16)
    w_gu_b = w_gate_up.astype(BF16)
    w_dn_b = w_down.astype(BF16)
    eye = jnp.eye(LRU_BLOCKS, dtype=F32)

    def block_diag(w):
        return (w[:, :, :, None, :] * eye[None, :, None, :, None]).reshape(
            DEPTH, LRU_WIDTH, LRU_WIDTH).astype(BF16)

    wa_bd = block_diag(lru_wa)
    wx_bd = block_diag(lru_wx)
    vec = lambda p: p[:, None, :]
    dtb_p = dt_lanes(dt_bias)[:, None, :]
    alog_p = dt_lanes(a_log)[:, None, :]
    dskip_p = jnp.repeat(d_skip, SSD_HEAD_DIM, axis=1)[:, None, :]
    buf_a = _pad_state_rows(state_lru_conv)
    buf_c = _pad_state_rows(state_ssd_conv)
    h0_a = jnp.pad(state_lru_h[:, :, None, :], ((0, 0), (0, 0), (0, DEC_T - 1), (0, 0))).reshape(
        DEPTH, bs * DEC_T, LRU_WIDTH)
    mult = jnp.asarray(_decode_key_multiplicity())
    cache_kt = jnp.transpose(cache_swa_k, (0, 1, 3, 4, 2))
    cache_vt = jnp.transpose(cache_swa_v, (0, 1, 3, 4, 2))

    yp = x_prompt.reshape(bp * seq, D_MODEL)
    ys = x_sample.reshape(bs * DEC_T, D_MODEL)
    p_new = [[] for _ in range(6)]
    s_new = [[] for _ in range(6)]
    a_args = (conv_a_w, vec(conv_a_b), wa_bd, vec(lru_ba), wx_bd, vec(lru_bx), vec(lru_lambda))
    c_args = (conv_c_w, vec(conv_c_b), dtb_p, alog_p, dskip_p, vec(ssm_norm))
    post_w = (w_out_b, vec(norm_mix_out), vec(norm_ffn_in), w_gu_b, w_dn_b, vec(norm_ffn_out))
    g_in = vec(norm_mix_in)
    tails, s_ssd = None, None
    for l in range(DEPTH):
        pp, *tails = _in_proj(yp, g_in, w_in_b, l, bp, DENSE_TM, with_tail=True, tails=tails)
        (ps,) = _in_proj(ys, g_in, w_in_b, l, 1, DENSE_TM, with_tail=False)
        out_b, out_b_s = _attn(pp, ps, cache_kt, cache_vt, l, mult, bp)

        out_a, h_last = _mix_a_prompt(pp, *a_args, layer=l, batch=bp, tb=MIX_A_TB)
        out_c, ssd_state = _ssd_prompt(pp, *c_args, layer=l, batch=bp, tb=SSD_TB)
        yp = _post(out_a, out_b, out_c, yp, *post_w, layer=l, tm=DENSE_TM)
        pp3 = pp.reshape(bp, seq, N_IN_PAD)
        last = slice(seq - (CONV_W - 1), seq)
        p_new[0].append(h_last.reshape(bp, LRU_WIDTH))
        p_new[1].append(pp3[:, last, P_COLS["xa"][0]:P_COLS["xa"][0] + LRU_WIDTH])
        p_new[4].append(ssd_state)
        p_new[5].append(pp3[:, last, P_COLS["xbc"][0]:P_COLS["xbc"][0] + SSD_CONV_CH])

        out_a, h_all = _mix_a_decode(ps, buf_a, h0_a, *a_args, layer=l, tm=DENSE_TM)
        out_c, s_ssd = _ssd_decode(ps, buf_c, state_ssd, s_ssd, l, *c_args)
        ys = _post(out_a, out_b_s, out_c, ys, *post_w, layer=l, tm=DENSE_TM)
        ps3 = ps.reshape(bs, DEC_T, N_IN_PAD)
        last = slice(DEC_T - (CONV_W - 1), DEC_T)
        new_kv = lambda name: ps3[:, :, P_COLS[name][0]:P_COLS[name][0] + ATT_WIDTH].reshape(
            bs, DEC_T, ATT_HEADS, HEAD_DIM)
        s_new[0].append(h_all.reshape(bs, DEC_T, LRU_WIDTH)[:, DEC_T - 1])
        s_new[1].append(ps3[:, last, P_COLS["xa"][0]:P_COLS["xa"][0] + LRU_WIDTH])
        s_new[2].append(new_kv("k"))
        s_new[3].append(new_kv("v"))
        s_new[5].append(ps3[:, last, P_COLS["xbc"][0]:P_COLS["xbc"][0] + SSD_CONV_CH])

    stack = lambda parts: jnp.stack(parts) if parts else None
    outs_p = [stack(a) for a in p_new]
    outs_s = [stack(a) for a in s_new]
    outs_p[2], outs_p[3] = [jnp.transpose(t.reshape(DEPTH, bp, ATT_HEADS, HEAD_DIM, MAX_WINDOW), (0, 1, 4, 2, 3))
                            for t in tails]
    outs_s[4] = s_ssd
    return (yp.reshape(bp, seq, D_MODEL), ys.reshape(bs, DEC_T, D_MODEL), *outs_p, *outs_s)
```

```python
import functools

import numpy as np
import jax
import jax.numpy as jnp
from jax import lax
from jax.experimental import pallas as pl
from jax.experimental.pallas import tpu as pltpu

F32 = jnp.float32
BF16 = jnp.bfloat16

D_MODEL = 1024
DEPTH = 4
CONV_W = 4
HEAD_DIM = 64
ATT_WIDTH = 384
ATT_HEADS = 6
ATT_SPAN = 128
DILATIONS = (1, 4, 16)
MAX_WINDOW = 2048
ATT_SCALE = HEAD_DIM ** -0.5
SSD_WIDTH = 384
SSD_HEADS = 6
SSD_HEAD_DIM = 64
SSD_GROUPS = 2
SSD_STATE = 128
SSD_CHUNK = 128
SSD_CONV_CH = 896
LRU_WIDTH = 256
LRU_BLOCKS = 4
LRU_C = 8.0
D_FF = 2816
N_IN = 2950
EPS = 1e-6

LANES = 128
SUBLANES = 8
N_IN_PAD = 3072
VMEM_LIMIT = 56 * 1024 * 1024

W_IN_COLS = {"ga": (0, 256), "xa": (256, 256), "q": (512, 384), "k": (896, 384), "v": (1280, 384),
             "z": (1664, 384), "xbc": (2048, 896), "dt": (2944, 6)}
P_COLS = {"xbc": (0, 896), "dt": (896, 128), "ga": (1024, 256), "xa": (1280, 256), "q": (1536, 384),
          "k": (1920, 384), "v": (2304, 384), "z": (2688, 384)}


def _pcol(name, width=None):
    off, w = P_COLS[name]
    width = width or w
    assert off % width == 0
    return off // width

NEG_INF = float("-inf")
NT_DIMS = (((1,), (1,)), ((), ()))


def _cparams(sem):
    return pltpu.CompilerParams(dimension_semantics=sem, vmem_limit_bytes=VMEM_LIMIT)


def _const_spec(shape, layer=None):
    nd = len(shape)
    if layer is None:
        return pl.BlockSpec(shape, lambda *_: (0,) * nd, pipeline_mode=pl.Buffered(1))
    return pl.BlockSpec((None,) + tuple(shape), lambda *_: (layer,) + (0,) * nd, pipeline_mode=pl.Buffered(1))


def _drop_refs(kernel_fn, first, count=1):
    def wrapped(*refs, **kw):
        return kernel_fn(*refs[:first], *refs[first + count:], **kw)
    return wrapped


def _rms(x, g):
    ms = jnp.mean(x * x, axis=-1, keepdims=True)
    return x * lax.rsqrt(ms + EPS) * g


def _sigmoid(x):
    return jax.nn.sigmoid(x)


def _silu(x):
    return x * jax.nn.sigmoid(x)


def _softplus(x):
    return jnp.maximum(x, 0.0) + jnp.log1p(jnp.exp(-jnp.abs(x)))


def _gelu_tanh(x):
    c = np.sqrt(2.0 / np.pi).astype(np.float32)
    return 0.5 * x * (1.0 + jnp.tanh(c * (x + 0.044715 * (x * x * x))))


def _roll_rows(x, shift):
    n = x.shape[0]
    shift = shift % n
    if shift == 0:
        return x
    return pltpu.roll(x, shift, 0)


def _row_index(shape, group=None):
    r = lax.broadcasted_iota(jnp.int32, shape, 0)
    if group is not None:
        r = jnp.bitwise_and(r, group - 1)
    return r


def _lin_scan(a, u, row, steps):
    s = 1
    for _ in range(steps):
        keep = row >= s
        a_sh = jnp.where(keep, _roll_rows(a, s), 1.0)
        u_sh = jnp.where(keep, _roll_rows(u, s), 0.0)
        u = a * u_sh + u
        a = a * a_sh
        s *= 2
    return a, u


def _cumsum_rows(x, row, steps):
    s = 1
    for _ in range(steps):
        x = x + jnp.where(row >= s, _roll_rows(x, s), 0.0)
        s *= 2
    return x


def _suffix_sum_rows(x, row, group, steps):
    incl = x
    s = 1
    for _ in range(steps):
        incl = incl + jnp.where(row < group - s, _roll_rows(incl, -s), 0.0)
        s *= 2
    return incl - x


def _conv_taps(x, shifted_fn, w_ref, b_ref):
    y = b_ref[...] + w_ref[CONV_W - 1:CONV_W, :] * x
    for s in range(1, CONV_W):
        y = y + w_ref[CONV_W - 1 - s:CONV_W - s, :] * shifted_fn(s)
    return y


def _conv_block_carry(x, tail_ref, w_ref, b_ref):
    tb, c = x.shape
    x3 = x.reshape(tb // SUBLANES, SUBLANES, c)
    sub = lax.broadcasted_iota(jnp.int32, (1, SUBLANES, c), 1)
    tail = tail_ref[...]

    def shifted(s):
        r = pltpu.roll(x3, s, 1)
        before = jnp.concatenate([pltpu.roll(tail, s, 0)[None], r[:-1]], axis=0)
        return jnp.where(sub < s, before, r)

    y = _conv_taps(x3, shifted, w_ref, b_ref)
    tail_ref[...] = x[tb - SUBLANES:tb]
    return y.reshape(tb, c)


def _conv_groups(x, bufpad, w_ref, b_ref):
    row = _row_index(x.shape, SUBLANES)
    return _conv_taps(
        x, lambda s: jnp.where(row >= s, _roll_rows(x, s), _roll_rows(bufpad, s - (CONV_W - 1))), w_ref, b_ref)


def _lru_gates(xc, wa_ref, ba_ref, wx_ref, bx_ref, lam_ref):
    xb = xc.astype(BF16)
    r = _sigmoid(jnp.dot(xb, wa_ref[...], preferred_element_type=F32) + ba_ref[...])
    ig = _sigmoid(jnp.dot(xb, wx_ref[...], preferred_element_type=F32) + bx_ref[...])
    log_a = (-LRU_C) * r * _softplus(-lam_ref[...])
    a = jnp.exp(log_a)
    t = jnp.tanh(log_a)
    u = jnp.sqrt(-2.0 * t / (1.0 - t)) * (ig * xc)
    return a, u


def _in_proj_kernel(x_ref, g_ref, w_ref, p_ref, *tail_refs, first_tail):
    h = _rms(x_ref[...], g_ref[...]).astype(BF16)
    p_ref[...] = jnp.dot(h, w_ref[...], preferred_element_type=F32)
    if tail_refs:
        kt_ref, vt_ref = tail_refs

        @pl.when(pl.program_id(1) >= first_tail)
        def _():
            k_off, v_off = P_COLS["k"][0], P_COLS["v"][0]
            kt_ref[...] = p_ref[:, k_off:k_off + ATT_WIDTH].T
            vt_ref[...] = p_ref[:, v_off:v_off + ATT_WIDTH].T


def _in_proj(x, g, w, layer, batch, tm, tails=None):
    with_tail = tails is not None
    n = x.shape[0]
    nt = n // batch // tm
    first_tail = nt - MAX_WINDOW // tm
    row = lambda width: pl.BlockSpec((tm, width), lambda b, j: (b * nt + j, 0))
    out_specs = [row(N_IN_PAD)]
    out_shape = [jax.ShapeDtypeStruct((n, N_IN_PAD), F32)]
    if with_tail:
        tail = pl.BlockSpec((None, None, ATT_WIDTH, tm),
                            lambda b, j: (layer, b, 0, jnp.maximum(j - first_tail, 0)))
        out_specs += [tail, tail]
        out_shape += [jax.ShapeDtypeStruct(t.shape, F32) for t in tails]
    body = functools.partial(_in_proj_kernel, first_tail=first_tail)
    extra_specs, extra_args, aliases = [], [], {}
    if with_tail:
        body = _drop_refs(body, 3, 2)
        extra_specs = [pl.BlockSpec(memory_space=pl.ANY)] * 2
        extra_args = list(tails)
        aliases = {3: 1, 4: 2}
    return pl.pallas_call(
        body,
        grid=(batch, nt),
        in_specs=[row(D_MODEL), _const_spec((1, D_MODEL), layer), _const_spec((D_MODEL, N_IN_PAD), layer)]
        + extra_specs,
        out_specs=out_specs,
        out_shape=out_shape,
        input_output_aliases=aliases,
        compiler_params=_cparams(("parallel", "arbitrary")),
        name="in_proj",
    )(x, g, w, *extra_args)


FFN_CHUNK = 256


def _post_kernel(a_ref, b_ref, c_ref, x_ref, wo_ref, go_ref, gi_ref, wgu_ref, wd_ref, gf_ref, o_ref):
    mixed = jnp.concatenate([a_ref[...], b_ref[...], c_ref[...]], axis=1).astype(BF16)
    x1 = x_ref[...] + _rms(jnp.dot(mixed, wo_ref[...], preferred_element_type=F32), go_ref[...])
    h = _rms(x1, gi_ref[...]).astype(BF16)
    acc = jnp.zeros(x1.shape, F32)
    for c in range(D_FF // FFN_CHUNK):
        lo = c * FFN_CHUNK
        g = jnp.dot(h, wgu_ref[:, lo:lo + FFN_CHUNK], preferred_element_type=F32)
        u = jnp.dot(h, wgu_ref[:, D_FF + lo:D_FF + lo + FFN_CHUNK], preferred_element_type=F32)
        act = (_silu(g) * u).astype(BF16)
        acc = acc + jnp.dot(act, wd_ref[lo:lo + FFN_CHUNK, :], preferred_element_type=F32)
    o_ref[...] = x1 + _rms(acc, gf_ref[...])


def _post(a, b, c, x, wo, go, gi, wgu, wd, gf, layer, tm):
    n = x.shape[0]
    row = lambda width: pl.BlockSpec((tm, width), lambda i: (i, 0))
    vec = _const_spec((1, D_MODEL), layer)
    return pl.pallas_call(
        _post_kernel,
        grid=(n // tm,),
        in_specs=[row(LRU_WIDTH), row(ATT_WIDTH), row(SSD_WIDTH), row(D_MODEL),
                  _const_spec((D_MODEL, D_MODEL), layer), vec, vec,
                  _const_spec((D_MODEL, 2 * D_FF), layer), _const_spec((D_FF, D_MODEL), layer), vec],
        out_specs=row(D_MODEL),
        out_shape=jax.ShapeDtypeStruct((n, D_MODEL), F32),
        compiler_params=_cparams(("parallel",)),
        name="post",
    )(a, b, c, x, wo, go, gi, wgu, wd, gf)


def _mix_a_prompt_kernel(xa_ref, ga_ref, cw_ref, cb_ref, wa_ref, ba_ref, wx_ref, bx_ref, lam_ref,
                         out_ref, hlast_ref, tail_ref, hc_ref):
    @pl.when(pl.program_id(1) == 0)
    def _():
        tail_ref[...] = jnp.zeros_like(tail_ref)
        hc_ref[...] = jnp.zeros_like(hc_ref)

    x = xa_ref[...]
    tb, c = x.shape
    xc = _conv_block_carry(x, tail_ref, cw_ref, cb_ref)
    a, u = _lru_gates(xc, wa_ref, ba_ref, wx_ref, bx_ref, lam_ref)
    gate = _gelu_tanh(ga_ref[...])
    nslab = tb // SUBLANES
    a3 = a.reshape(nslab, SUBLANES, c)
    u3 = u.reshape(nslab, SUBLANES, c)
    sub = lax.broadcasted_iota(jnp.int32, (1, SUBLANES, c), 1)
    s = 1
    while s < SUBLANES:
        keep = sub >= s
        a_sh = jnp.where(keep, pltpu.roll(a3, s, 1), 1.0)
        u_sh = jnp.where(keep, pltpu.roll(u3, s, 1), 0.0)
        u3 = a3 * u_sh + u3
        a3 = a3 * a_sh
        s *= 2
    h_row = hc_ref[0:1, :]
    for k in range(nslab):
        h = u3[k] + a3[k] * h_row
        h_row = h[SUBLANES - 1:SUBLANES, :]
        rows = slice(k * SUBLANES, (k + 1) * SUBLANES)
        out_ref[rows, :] = h * gate[rows, :]
    hc_ref[...] = jnp.broadcast_to(h_row, hc_ref.shape)
    hlast_ref[...] = h_row


def _mix_a_prompt(p, cw, cb, wa, ba, wx, bx, lam, layer, batch, tb):
    n = p.shape[0]
    nb = n // batch // tb
    blk = pl.BlockSpec((tb, LRU_WIDTH), lambda b, i: (b * nb + i, 0))
    col = lambda name: pl.BlockSpec((tb, LRU_WIDTH), lambda b, i: (b * nb + i, _pcol(name)))
    vec = _const_spec((1, LRU_WIDTH), layer)
    mat = _const_spec((LRU_WIDTH, LRU_WIDTH), layer)
    return pl.pallas_call(
        _mix_a_prompt_kernel,
        grid=(batch, nb),
        in_specs=[col("xa"), col("ga"), _const_spec((CONV_W, LRU_WIDTH), layer), vec, mat, vec, mat, vec, vec],
        out_specs=[blk, pl.BlockSpec((None, 1, LRU_WIDTH), lambda b, i: (b, 0, 0))],
        out_shape=[jax.ShapeDtypeStruct((n, LRU_WIDTH), F32),
                   jax.ShapeDtypeStruct((batch, 1, LRU_WIDTH), F32)],
        scratch_shapes=[pltpu.VMEM((SUBLANES, LRU_WIDTH), F32), pltpu.VMEM((SUBLANES, LRU_WIDTH), F32)],
        compiler_params=_cparams(("parallel", "arbitrary")),
        name="mix_a_prompt",
    )(p, p, cw, cb, wa, ba, wx, bx, lam)


def _mix_a_decode_kernel(xa_ref, ga_ref, buf_ref, h0_ref, cw_ref, cb_ref, wa_ref, ba_ref, wx_ref, bx_ref,
                         lam_ref, out_ref, h_ref):
    x = xa_ref[...]
    xc = _conv_groups(x, buf_ref[...], cw_ref, cb_ref)
    a, u = _lru_gates(xc, wa_ref, ba_ref, wx_ref, bx_ref, lam_ref)
    u = u + a * h0_ref[...]
    row = _row_index(x.shape, SUBLANES)
    _, h = _lin_scan(a, u, row, 3)
    h_ref[...] = h
    out_ref[...] = h * _gelu_tanh(ga_ref[...])


def _mix_a_decode(p, bufpad, h0pad, cw, cb, wa, ba, wx, bx, lam, layer, tm):
    n = p.shape[0]
    blk = pl.BlockSpec((tm, LRU_WIDTH), lambda i: (i, 0))
    col = lambda name: pl.BlockSpec((tm, LRU_WIDTH), lambda i: (i, _pcol(name)))
    lay = pl.BlockSpec((None, tm, LRU_WIDTH), lambda i: (layer, i, 0))
    vec = _const_spec((1, LRU_WIDTH), layer)
    mat = _const_spec((LRU_WIDTH, LRU_WIDTH), layer)
    return pl.pallas_call(
        _mix_a_decode_kernel,
        grid=(n // tm,),
        in_specs=[col("xa"), col("ga"), lay, lay, _const_spec((CONV_W, LRU_WIDTH), layer), vec, mat, vec, mat, vec,
                  vec],
        out_specs=[blk, blk],
        out_shape=[jax.ShapeDtypeStruct((n, LRU_WIDTH), F32), jax.ShapeDtypeStruct((n, LRU_WIDTH), F32)],
        compiler_params=_cparams(("parallel",)),
        name="mix_a_decode",
    )(p, p, bufpad, h0pad, cw, cb, wa, ba, wx, bx, lam)


ATT_BLK = 2048
ATT_UNIT = 128
ATT_UNROLL = 4


def _attn_scores(q_t, kp_t, kc_t, bias, lane_lo):
    zero = jnp.zeros_like(q_t)
    qs = q_t * ATT_SCALE
    qq = jnp.concatenate([jnp.where(lane_lo, qs, zero), jnp.where(lane_lo, zero, qs)], axis=0).astype(BF16)
    kk = jnp.concatenate([kp_t, kc_t], axis=0).astype(BF16)
    s = lax.dot_general(qq, kk, NT_DIMS, preferred_element_type=F32) + bias
    m = jnp.max(s, axis=1, keepdims=True)
    return jnp.exp(s - m).astype(BF16), m


def _attn_output(p, m, vp_t, vc_t, lane_lo):
    zero = jnp.zeros_like(vp_t)
    pcat = jnp.concatenate([p[:ATT_UNIT], p[ATT_UNIT:]], axis=1)
    one_lo = jnp.where(lane_lo, 1.0, 0.0)
    one_hi = 1.0 - one_lo
    w = jnp.concatenate([
        jnp.concatenate([jnp.where(lane_lo, vp_t, zero), one_lo], axis=1),
        jnp.concatenate([jnp.where(lane_lo, vc_t, zero), one_lo], axis=1),
        jnp.concatenate([jnp.where(lane_lo, zero, vp_t), one_hi], axis=1),
        jnp.concatenate([jnp.where(lane_lo, zero, vc_t), one_hi], axis=1)], axis=0).astype(BF16)
    ol = jnp.dot(pcat, w, preferred_element_type=F32)
    m_t = jnp.where(lane_lo, jnp.broadcast_to(m[:ATT_UNIT], vp_t.shape),
                    jnp.broadcast_to(m[ATT_UNIT:], vp_t.shape))
    return ol[:, :LANES], m_t, ol[:, LANES:]


DEC_T = 8
DEC_NEW_PAD = 128
DEC_KEYS = MAX_WINDOW + DEC_NEW_PAD
DEC_ROWS = ATT_HEADS * DEC_T
ATT_PHASES = len(DILATIONS)


def _decode_key_multiplicity():
    pos = np.full((DEC_KEYS,), -10 ** 9, np.int64)
    pos[:MAX_WINDOW] = np.arange(MAX_WINDOW)
    pos[MAX_WINDOW:MAX_WINDOW + DEC_T] = MAX_WINDOW + np.arange(DEC_T)
    t = np.arange(DEC_T)
    dist = (MAX_WINDOW + t)[:, None] - pos[None, :]
    cnt = np.zeros(dist.shape, np.float32)
    for d in DILATIONS:
        cnt += ((dist >= 0) & (dist % d == 0) & (dist <= ATT_SPAN * d)).astype(np.float32)
    return np.tile(cnt, (ATT_HEADS, 1))


def _attn_decode_block(q_ref, kn_ref, vn_ref, kt_ref, vt_ref, mult_ref, o_ref):
    lane = lax.broadcasted_iota(jnp.int32, (DEC_T, ATT_WIDTH), 1)
    head_masks = [jnp.logical_and(lane >= HEAD_DIM * h, lane < HEAD_DIM * (h + 1)) for h in range(ATT_HEADS)]
    mult = mult_ref[...]
    seen = mult > 0.0
    zpad = jnp.zeros((DEC_NEW_PAD - DEC_T, ATT_WIDTH), F32)
    pending = []
    for b in range(kt_ref.shape[0]):
        new_rows = slice(DEC_T * b, DEC_T * (b + 1))
        qb = q_ref[new_rows, :] * ATT_SCALE
        qbd = jnp.concatenate([jnp.where(hm, qb, 0.0) for hm in head_masks], axis=0).astype(BF16)
        kt = kt_ref[b].reshape(ATT_WIDTH, MAX_WINDOW).astype(BF16)
        kn = jnp.concatenate([kn_ref[new_rows, :], zpad], axis=0).astype(BF16)
        s = jnp.concatenate([jnp.dot(qbd, kt, preferred_element_type=F32),
                             lax.dot_general(qbd, kn, NT_DIMS, preferred_element_type=F32)], axis=1)
        s = jnp.where(seen, s, NEG_INF)
        m = jnp.max(s, axis=1, keepdims=True)
        p = mult * jnp.exp(s - m)
        pending.append((p / jnp.sum(p, axis=1, keepdims=True)).astype(BF16))
    for b, p in enumerate(pending):
        new_rows = slice(DEC_T * b, DEC_T * (b + 1))
        vt = vt_ref[b].reshape(ATT_WIDTH, MAX_WINDOW).astype(BF16)
        vn = jnp.concatenate([vn_ref[new_rows, :], zpad], axis=0).astype(BF16)
        o = (lax.dot_general(p[:, :MAX_WINDOW], vt, NT_DIMS, preferred_element_type=F32)
             + jnp.dot(p[:, MAX_WINDOW:], vn, preferred_element_type=F32))
        out = jnp.zeros((DEC_T, ATT_WIDTH), F32)
        for h, hm in enumerate(head_masks):
            out = out + jnp.where(hm, o[DEC_T * h:DEC_T * (h + 1), :], 0.0)
        o_ref[new_rows, :] = out


def _attn_kernel(q_ref, kp_ref, kc_ref, vp_ref, vc_ref, qn_ref, kn_ref, vn_ref, kt_ref, vt_ref, mult_ref,
                 o_ref, od_ref, acc_ref, m_ref, l_ref, *, dec_blocks):
    sub = pl.program_id(3)
    step = ((pl.program_id(0) * pl.num_programs(1) + pl.program_id(1)) * pl.num_programs(2)
            + pl.program_id(2)) * ATT_PHASES + sub

    @pl.when(step < dec_blocks)
    def _():
        _attn_decode_block(qn_ref, kn_ref, vn_ref, kt_ref, vt_ref, mult_ref, od_ref)

    first_block = pl.program_id(2) == 0
    lane_lo = lax.broadcasted_iota(jnp.int32, (ATT_UNIT, LANES), 1) < HEAD_DIM
    qi = lax.broadcasted_iota(jnp.int32, (2 * ATT_UNIT, 2 * ATT_UNIT), 0) & (ATT_UNIT - 1)
    ki = lax.broadcasted_iota(jnp.int32, (2 * ATT_UNIT, 2 * ATT_UNIT), 1)
    is_prev = ki < ATT_UNIT
    dist = qi - ki + ATT_UNIT
    bias = jnp.where(dist >= 0, jnp.where(dist <= ATT_SPAN, 0.0, NEG_INF), NEG_INF)
    bias_first = bias + jnp.where(is_prev, jnp.where(first_block, NEG_INF, 0.0), 0.0)

    def rows(start, d):
        if d == 1:
            return pl.ds(start, ATT_UNIT)
        return pl.ds(start, ATT_UNIT, stride=d)

    def head_scores(bi, rho):
        d = DILATIONS[bi]
        cur = rows(rho, d)
        prev = rows(rho + ATT_BLK - d * ATT_UNIT, d)
        return (cur, prev, vp_ref) + _attn_scores(q_ref[cur, :], kp_ref[prev, :], kc_ref[cur, :], bias_first, lane_lo)

    def inner_scores(bi, idx):
        d = DILATIONS[bi]
        rho = idx & (d - 1)
        j = 1 + (idx >> int(np.log2(d)))
        start = rho + d * ATT_UNIT * j
        cur = rows(start, d)
        prev = rows(start - d * ATT_UNIT, d)
        return (cur, prev, vc_ref) + _attn_scores(q_ref[cur, :], kc_ref[prev, :], kc_ref[cur, :], bias, lane_lo)

    def group(bi, pending):
        for cur, prev, vprev_ref, p, m in pending:
            o_t, m_t, l_t = _attn_output(p, m, vprev_ref[prev, :], vc_ref[cur, :], lane_lo)
            acc_ref[bi, cur, :] = o_t
            m_ref[bi, cur, :] = m_t
            l_ref[bi, cur, :] = l_t

    def chunks(scores, bi, base, lo, hi):
        def body(c, carry):
            group(bi, [scores(bi, base + ATT_UNROLL * c + u) for u in range(ATT_UNROLL)])
            return carry
        lax.fori_loop(lo, hi, body, 0)

    units = ATT_BLK // ATT_UNIT
    @pl.when(sub == 0)
    def _():
        group(0, [head_scores(0, 0)] + [inner_scores(0, idx) for idx in range(ATT_UNROLL - 1)])
        chunks(inner_scores, 0, ATT_UNROLL - 1, 0, (units - ATT_UNROLL) // ATT_UNROLL)

    @pl.when(sub == 1)
    def _():
        group(1, [head_scores(1, rho) for rho in range(DILATIONS[1])])
        chunks(inner_scores, 1, 0, 0, (units - DILATIONS[1]) // ATT_UNROLL)

    @pl.when(sub == 2)
    def _():
        chunks(head_scores, 2, 0, 0, units // ATT_UNROLL)

        def combine(c, carry):
            sl = pl.ds(pl.multiple_of(c * ATT_UNIT, ATT_UNIT), ATT_UNIT)
            ms = [m_ref[bi, sl, :] for bi in range(len(DILATIONS))]
            m = functools.reduce(jnp.maximum, ms)
            ws = [jnp.exp(mi - m) for mi in ms]
            num = sum(w * acc_ref[bi, sl, :] for bi, w in enumerate(ws))
            den = sum(w * l_ref[bi, sl, :] for bi, w in enumerate(ws))
            o_ref[sl, :] = num / den
            return carry

        lax.fori_loop(0, ATT_BLK // ATT_UNIT, combine, 0, unroll=2)


def _attn(pp, ps, cache_kt, cache_vt, layer, mult, batch):
    n = pp.shape[0]
    nb = n // batch // ATT_BLK
    npair = ATT_WIDTH // LANES
    nd = ps.shape[0]
    nseq = nd // DEC_T
    steps = batch * npair * nb * ATT_PHASES
    spb = -(-nseq // steps)
    assert nseq % spb == 0
    dec_blocks = nseq // spb

    def dec_blk(b, hp, i, s):
        return jnp.minimum(((b * npair + hp) * nb + i) * ATT_PHASES + s, dec_blocks - 1)

    out = pl.BlockSpec((ATT_BLK, LANES), lambda b, hp, i, s: (b * nb + i, hp))
    cur = lambda name: pl.BlockSpec((ATT_BLK, LANES), lambda b, hp, i, s: (b * nb + i, _pcol(name, LANES) + hp))
    prev = lambda name: pl.BlockSpec(
        (ATT_BLK, LANES), lambda b, hp, i, s: (b * nb + jnp.maximum(i - 1, 0), _pcol(name, LANES) + hp))
    new = lambda name: pl.BlockSpec((DEC_T * spb, ATT_WIDTH), lambda b, hp, i, s: (dec_blk(b, hp, i, s), _pcol(name)))
    win = pl.BlockSpec((None, spb, ATT_HEADS, HEAD_DIM, MAX_WINDOW),
                       lambda b, hp, i, s: (layer, dec_blk(b, hp, i, s), 0, 0, 0))
    scratch = pltpu.VMEM((len(DILATIONS), ATT_BLK, LANES), F32)
    return pl.pallas_call(
        functools.partial(_attn_kernel, dec_blocks=dec_blocks),
        grid=(batch, npair, nb, ATT_PHASES),
        in_specs=[cur("q"), prev("k"), cur("k"), prev("v"), cur("v"), new("q"), new("k"), new("v"), win, win,
                  _const_spec((DEC_ROWS, DEC_KEYS))],
        out_specs=[out, pl.BlockSpec((DEC_T * spb, ATT_WIDTH), lambda b, hp, i, s: (dec_blk(b, hp, i, s), 0))],
        out_shape=[jax.ShapeDtypeStruct((n, ATT_WIDTH), F32), jax.ShapeDtypeStruct((nd, ATT_WIDTH), F32)],
        scratch_shapes=[scratch, scratch, scratch],
        compiler_params=_cparams(("arbitrary", "arbitrary", "arbitrary", "arbitrary")),
        name="attn",
    )(pp, pp, pp, pp, pp, ps, ps, ps, cache_kt, cache_vt, mult)


HPG = SSD_HEADS // SSD_GROUPS
B_OFF = SSD_WIDTH
C_OFF = SSD_WIDTH + SSD_GROUPS * SSD_STATE


def _ssd_chunk_diag(xbc, dt, acum, pair_ok):
    acum_t = acum.T
    ys, xrs = [], []
    for g in range(SSD_GROUPS):
        bg = xbc[:, B_OFF + g * SSD_STATE:B_OFF + (g + 1) * SSD_STATE].astype(BF16)
        cg = xbc[:, C_OFF + g * SSD_STATE:C_OFF + (g + 1) * SSD_STATE].astype(BF16)
        cb = lax.dot_general(cg, bg, NT_DIMS, preferred_element_type=F32)
        for h in range(g * HPG, (g + 1) * HPG):
            xr = xbc[:, h * SSD_HEAD_DIM:(h + 1) * SSD_HEAD_DIM] * dt[:, h:h + 1]
            diff = acum[:, h:h + 1] - acum_t[h:h + 1, :]
            lmat = jnp.exp(jnp.where(pair_ok, diff, NEG_INF))
            ys.append(jnp.dot((cb * lmat).astype(BF16), xr.astype(BF16), preferred_element_type=F32))
            xrs.append(xr)
    return ys, xrs


def _ssd_finish(y, xs, z, dskip_ref, norm_ref):
    y = y + dskip_ref[...] * xs
    y = y * _silu(z)
    return _rms(y, norm_ref[...])


DT_COPIES = 3
SSD_PAIRS = SSD_HEADS // 2
SSD_CONV_ROWS = 128


def _ssd_prompt_kernel(xbc_ref, z_ref, dt_ref, cw_ref, cb_ref, dtb_ref, alog_ref, dskip_ref, norm_ref,
                       out_ref, state_ref, tail_ref, xc_ref, st_ref):
    tb = xbc_ref.shape[0]

    @pl.when(pl.program_id(1) == 0)
    def _():
        tail_ref[...] = jnp.zeros_like(tail_ref)
        st_ref[...] = jnp.zeros_like(st_ref)

    for r0 in range(0, tb, SSD_CONV_ROWS):
        xc_ref[r0:r0 + SSD_CONV_ROWS, :] = _silu(
            _conv_block_carry(xbc_ref[r0:r0 + SSD_CONV_ROWS, :], tail_ref, cw_ref, cb_ref))
    a_neg = -jnp.exp(alog_ref[...])
    row = _row_index((SSD_CHUNK, LANES))
    lane = lax.broadcasted_iota(jnp.int32, (SSD_CHUNK, LANES), 1)
    lane_lo = lane < SSD_HEAD_DIM
    causal = row >= lane
    zero = jnp.zeros((SSD_CHUNK, LANES), F32)

    def pair_rows(t):
        return jnp.concatenate([jnp.where(lane_lo, t, zero), jnp.where(lane_lo, zero, t)], axis=0).astype(BF16)

    def chunk(c, carry):
        r0 = pl.multiple_of(c * SSD_CHUNK, SSD_CHUNK)
        rows = pl.ds(r0, SSD_CHUNK)
        xc = xc_ref[rows, :]
        dt = _softplus(dt_ref[rows, :] + dtb_ref[...])
        acum = _cumsum_rows(dt * a_neg, row, 7)
        last = acum[SSD_CHUNK - 1:SSD_CHUNK, :]
        tot = jnp.exp(last)
        pt = jnp.where(lane < SUBLANES, acum, jnp.where(lane < 2 * SUBLANES, dt, dt * jnp.exp(last - acum))).T
        bs = [xc[:, B_OFF + g * SSD_STATE:B_OFF + (g + 1) * SSD_STATE] for g in range(SSD_GROUPS)]
        cs = [xc[:, C_OFF + g * SSD_STATE:C_OFF + (g + 1) * SSD_STATE].astype(BF16) for g in range(SSD_GROUPS)]
        bts = [b.T for b in bs]
        cbs = [lax.dot_general(cs[g], bs[g].astype(BF16), NT_DIMS, preferred_element_type=F32)
               for g in range(SSD_GROUPS)]
        gs, eacs, btws = [], [], []
        for h in range(SSD_HEADS):
            g = h // HPG
            a_col = jnp.broadcast_to(acum[:, h:h + 1], (SSD_CHUNK, SSD_CHUNK))
            lmat = jnp.exp(jnp.where(causal, a_col - pt[h:h + 1, :], NEG_INF))
            gs.append((cbs[g] * lmat * pt[SUBLANES + h:SUBLANES + h + 1, :]).astype(BF16))
            eacs.append(jnp.exp(a_col))
            btws.append((bts[g] * pt[2 * SUBLANES + h:2 * SUBLANES + h + 1, :]).astype(BF16))
        outs = []
        for k in range(SSD_PAIRS):
            h0, h1 = 2 * k, 2 * k + 1
            g0, g1 = h0 // HPG, h1 // HPG
            x2 = pair_rows(xc[:, k * LANES:(k + 1) * LANES])
            y = jnp.dot(jnp.concatenate([gs[h0], gs[h1]], axis=1), x2, preferred_element_type=F32)
            st = st_ref[k]
            if g0 == g1:
                y_off = jnp.dot(cs[g0], st.astype(BF16), preferred_element_type=F32)
            else:
                y_off = jnp.dot(jnp.concatenate([cs[g0], cs[g1]], axis=1), pair_rows(st),
                                preferred_element_type=F32)
            outs.append(y + y_off * jnp.where(lane_lo, eacs[h0], eacs[h1]))
            upd = jnp.dot(jnp.concatenate([btws[h0], btws[h1]], axis=1), x2, preferred_element_type=F32)
            st_ref[k] = st * jnp.where(lane_lo, tot[:, h0:h0 + 1], tot[:, h1:h1 + 1]) + upd
        y = jnp.concatenate(outs, axis=1)
        out_ref[rows, :] = _ssd_finish(y, xc[:, :SSD_WIDTH], z_ref[rows, :], dskip_ref, norm_ref)
        return carry

    lax.fori_loop(0, tb // SSD_CHUNK, chunk, 0)
    for k in range(SSD_PAIRS):
        t = st_ref[k].T
        state_ref[2 * k] = t[:SSD_HEAD_DIM]
        state_ref[2 * k + 1] = t[SSD_HEAD_DIM:]


def _ssd_prompt(p, cw, cb, dtb, alog, dskip, norm, layer, batch, tb):
    n = p.shape[0]
    nb = n // batch // tb
    blk = lambda width: pl.BlockSpec((tb, width), lambda b, i: (b * nb + i, 0))
    col = lambda name: pl.BlockSpec((tb, P_COLS[name][1]), lambda b, i: (b * nb + i, _pcol(name)))
    return pl.pallas_call(
        _ssd_prompt_kernel,
        grid=(batch, nb),
        in_specs=[col("xbc"), col("z"), col("dt"),
                  _const_spec((CONV_W, SSD_CONV_CH), layer), _const_spec((1, SSD_CONV_CH), layer),
                  _const_spec((1, LANES), layer), _const_spec((1, LANES), layer),
                  _const_spec((1, SSD_WIDTH), layer), _const_spec((1, SSD_WIDTH), layer)],
        out_specs=[blk(SSD_WIDTH),
                   pl.BlockSpec((None, SSD_HEADS, SSD_HEAD_DIM, SSD_STATE), lambda b, i: (b, 0, 0, 0))],
        out_shape=[jax.ShapeDtypeStruct((n, SSD_WIDTH), F32),
                   jax.ShapeDtypeStruct((batch, SSD_HEADS, SSD_HEAD_DIM, SSD_STATE), F32)],
        scratch_shapes=[pltpu.VMEM((SUBLANES, SSD_CONV_CH), F32),
                        pltpu.VMEM((tb, SSD_CONV_CH), F32),
                        pltpu.VMEM((SSD_PAIRS, SSD_STATE, LANES), F32)],
        compiler_params=_cparams(("parallel", "arbitrary")),
        name="ssd_prompt",
    )(p, p, p, cw, cb, dtb, alog, dskip, norm)


SSD_DEC_SEQ = SSD_CHUNK // DEC_T


def _ssd_decode_kernel(xbc_ref, z_ref, dt_ref, buf_ref, h0_ref, cw_ref, cb_ref, dtb_ref, alog_ref, dskip_ref,
                       norm_ref, out_ref, hnew_ref, xc_ref, xrd_ref, eac_ref, tot_ref, yoff_ref):
    xc = _silu(_conv_groups(xbc_ref[...], buf_ref[...], cw_ref, cb_ref))
    xc_ref[...] = xc
    a_neg = -jnp.exp(alog_ref[...])
    row = _row_index((SSD_CHUNK, LANES), DEC_T)
    li = lax.broadcasted_iota(jnp.int32, (SSD_CHUNK, SSD_CHUNK), 0)
    si = lax.broadcasted_iota(jnp.int32, (SSD_CHUNK, SSD_CHUNK), 1)
    same_seq_causal = jnp.logical_and(li >= si, (li - si) <= (li & (DEC_T - 1)))
    dt = _softplus(dt_ref[...] + dtb_ref[...])
    dta = dt * a_neg
    acum = _cumsum_rows(dta, row, 3)
    rest = _suffix_sum_rows(dta, row, DEC_T, 3)
    eac_ref[...] = jnp.exp(acum)
    tot_ref[...] = jnp.exp(acum + rest)
    decay = jnp.exp(rest)
    ys, xrs = _ssd_chunk_diag(xc, dt, acum, same_seq_causal)
    xrd_ref[...] = jnp.concatenate([xrs[h] * decay[:, h:h + 1] for h in range(SSD_HEADS)], axis=1).T
    seq_of_lane = lax.shift_right_logical(lax.broadcasted_iota(jnp.int32, (SSD_HEAD_DIM, SSD_CHUNK), 1), 3)
    b_all = [xc[:, B_OFF + g * SSD_STATE:B_OFF + (g + 1) * SSD_STATE].astype(BF16) for g in range(SSD_GROUPS)]

    def seq(b, carry):
        r0 = pl.multiple_of(b * DEC_T, DEC_T)
        rows = pl.ds(r0, DEC_T)
        xcb = xc_ref[rows, :]
        eac = eac_ref[rows, :]
        tot = tot_ref[rows, :]
        own = seq_of_lane == b
        outs = []
        for h in range(SSD_HEADS):
            g = h // HPG
            cg = xcb[:, C_OFF + g * SSD_STATE:C_OFF + (g + 1) * SSD_STATE].astype(BF16)
            prev = h0_ref[b, h]
            outs.append(lax.dot_general(cg, prev.astype(BF16), NT_DIMS, preferred_element_type=F32)
                        * eac[:, h:h + 1])
            lhs = jnp.where(own, xrd_ref[h * SSD_HEAD_DIM:(h + 1) * SSD_HEAD_DIM, :], 0.0).astype(BF16)
            st = jnp.dot(lhs, b_all[g], preferred_element_type=F32)
            hnew_ref[b, h] = prev * tot[0:1, h:h + 1] + st
        yoff_ref[rows, :] = jnp.concatenate(outs, axis=1)
        return carry

    lax.fori_loop(0, SSD_DEC_SEQ, seq, 0, unroll=4)
    y = jnp.concatenate(ys, axis=1) + yoff_ref[...]
    out_ref[...] = _ssd_finish(y, xc[:, :SSD_WIDTH], z_ref[...], dskip_ref, norm_ref)


def _ssd_decode(p, bufpad, state, stacked, layer, cw, cb, dtb, alog, dskip, norm):
    n = p.shape[0]
    blk = lambda width: pl.BlockSpec((SSD_CHUNK, width), lambda i: (i, 0))
    col = lambda name: pl.BlockSpec((SSD_CHUNK, P_COLS[name][1]), lambda i: (i, _pcol(name)))
    st_in = pl.BlockSpec((None, SSD_DEC_SEQ, SSD_HEADS, SSD_HEAD_DIM, SSD_STATE), lambda i: (layer, i, 0, 0, 0))
    return pl.pallas_call(
        _drop_refs(_ssd_decode_kernel, 11),
        grid=(n // SSD_CHUNK,),
        in_specs=[col("xbc"), col("z"), col("dt"),
                  pl.BlockSpec((None, SSD_CHUNK, SSD_CONV_CH), lambda i: (layer, i, 0)), st_in,
                  _const_spec((CONV_W, SSD_CONV_CH), layer), _const_spec((1, SSD_CONV_CH), layer),
                  _const_spec((1, LANES), layer), _const_spec((1, LANES), layer),
                  _const_spec((1, SSD_WIDTH), layer), _const_spec((1, SSD_WIDTH), layer),
                  pl.BlockSpec(memory_space=pl.ANY)],
        out_specs=[blk(SSD_WIDTH), st_in],
        out_shape=[jax.ShapeDtypeStruct((n, SSD_WIDTH), F32), jax.ShapeDtypeStruct(stacked.shape, F32)],
        input_output_aliases={11: 1},
        scratch_shapes=[pltpu.VMEM((SSD_CHUNK, SSD_CONV_CH), F32),
                        pltpu.VMEM((SSD_WIDTH, SSD_CHUNK), F32),
                        pltpu.VMEM((SSD_CHUNK, LANES), F32),
                        pltpu.VMEM((SSD_CHUNK, LANES), F32),
                        pltpu.VMEM((SSD_CHUNK, SSD_WIDTH), F32)],
        compiler_params=_cparams(("parallel",)),
        name="ssd_decode",
    )(p, p, p, bufpad, state, cw, cb, dtb, alog, dskip, norm, stacked)


DENSE_TM = 512
MIX_A_TB = 512
SSD_TB = 1024


def _pad_state_rows(buf):
    l, b, r, c = buf.shape
    return jnp.pad(buf, ((0, 0), (0, 0), (0, DEC_T - r), (0, 0))).reshape(l, b * DEC_T, c)


def kernel(x_prompt, x_sample, state_lru_h, state_lru_conv, cache_swa_k, cache_swa_v, state_ssd, state_ssd_conv,
           norm_mix_in, norm_mix_out, w_in, conv_a_w, conv_a_b, lru_wa, lru_ba, lru_wx, lru_bx, lru_lambda,
           conv_c_w, conv_c_b, dt_bias, a_log, d_skip, ssm_norm, w_out, norm_ffn_in, norm_ffn_out,
           w_gate_up, w_down):
    bp, seq, _ = x_prompt.shape
    bs, dec_t, _ = x_sample.shape
    assert dec_t == DEC_T and seq % ATT_BLK == 0 and seq >= MAX_WINDOW and bs % SSD_DEC_SEQ == 0

    def dt_lanes(p):
        slot = jnp.pad(p, [(0, 0)] * (p.ndim - 1) + [(0, SUBLANES - SSD_HEADS)])
        rep = jnp.concatenate([slot] * DT_COPIES, axis=-1)
        return jnp.pad(rep, [(0, 0)] * (p.ndim - 1) + [(0, LANES - DT_COPIES * SUBLANES)])

    def w_cols(name):
        off, width = W_IN_COLS[name]
        blk = w_in[:, :, off:off + width]
        if name == "dt":
            return dt_lanes(blk)
        return jnp.pad(blk, ((0, 0), (0, 0), (0, P_COLS[name][1] - width)))

    w_in_b = jnp.concatenate([w_cols(name) for name in P_COLS], axis=2).astype(BF16)
    w_out_b = w_out.astype(BF16)
    w_gu_b = w_gate_up.astype(BF16)
    w_dn_b = w_down.astype(BF16)
    eye = jnp.eye(LRU_BLOCKS, dtype=F32)

    def block_diag(w):
        return (w[:, :, :, None, :] * eye[None, :, None, :, None]).reshape(
            DEPTH, LRU_WIDTH, LRU_WIDTH).astype(BF16)

    wa_bd = block_diag(lru_wa)
    wx_bd = block_diag(lru_wx)
    vec = lambda p: p[:, None, :]
    dtb_p = dt_lanes(dt_bias)[:, None, :]
    alog_p = dt_lanes(a_log)[:, None, :]
    dskip_p = jnp.repeat(d_skip, SSD_HEAD_DIM, axis=1)[:, None, :]
    buf_a = _pad_state_rows(state_lru_conv)
    buf_c = _pad_state_rows(state_ssd_conv)
    h0_a = jnp.pad(state_lru_h[:, :, None, :], ((0, 0), (0, 0), (0, DEC_T - 1), (0, 0))).reshape(
        DEPTH, bs * DEC_T, LRU_WIDTH)
    mult = jnp.asarray(_decode_key_multiplicity())
    cache_kt = jnp.transpose(cache_swa_k, (0, 1, 3, 4, 2))
    cache_vt = jnp.transpose(cache_swa_v, (0, 1, 3, 4, 2))

    yp = x_prompt.reshape(bp * seq, D_MODEL)
    ys = x_sample.reshape(bs * DEC_T, D_MODEL)
    p_new = [[] for _ in range(6)]
    s_new = [[] for _ in range(6)]
    a_args = (conv_a_w, vec(conv_a_b), wa_bd, vec(lru_ba), wx_bd, vec(lru_bx), vec(lru_lambda))
    c_args = (conv_c_w, vec(conv_c_b), dtb_p, alog_p, dskip_p, vec(ssm_norm))
    post_w = (w_out_b, vec(norm_mix_out), vec(norm_ffn_in), w_gu_b, w_dn_b, vec(norm_ffn_out))
    g_in = vec(norm_mix_in)
    tails = [jnp.zeros((DEPTH, bp, ATT_WIDTH, MAX_WINDOW), F32) for _ in range(2)]
    s_ssd = jnp.zeros(state_ssd.shape, F32)
    for l in range(DEPTH):
        pp, *tails = _in_proj(yp, g_in, w_in_b, l, bp, DENSE_TM, tails=tails)
        (ps,) = _in_proj(ys, g_in, w_in_b, l, 1, DENSE_TM)
        out_b, out_b_s = _attn(pp, ps, cache_kt, cache_vt, l, mult, bp)

        out_a, h_last = _mix_a_prompt(pp, *a_args, layer=l, batch=bp, tb=MIX_A_TB)
        out_c, ssd_state = _ssd_prompt(pp, *c_args, layer=l, batch=bp, tb=SSD_TB)
        yp = _post(out_a, out_b, out_c, yp, *post_w, layer=l, tm=DENSE_TM)
        pp3 = pp.reshape(bp, seq, N_IN_PAD)
        last = slice(seq - (CONV_W - 1), seq)
        p_new[0].append(h_last.reshape(bp, LRU_WIDTH))
        p_new[1].append(pp3[:, last, P_COLS["xa"][0]:P_COLS["xa"][0] + LRU_WIDTH])
        p_new[4].append(ssd_state)
        p_new[5].append(pp3[:, last, P_COLS["xbc"][0]:P_COLS["xbc"][0] + SSD_CONV_CH])

        out_a, h_all = _mix_a_decode(ps, buf_a, h0_a, *a_args, layer=l, tm=DENSE_TM)
        out_c, s_ssd = _ssd_decode(ps, buf_c, state_ssd, s_ssd, l, *c_args)
        ys = _post(out_a, out_b_s, out_c, ys, *post_w, layer=l, tm=DENSE_TM)
        ps3 = ps.reshape(bs, DEC_T, N_IN_PAD)
        last = slice(DEC_T - (CONV_W - 1), DEC_T)
        new_kv = lambda name: ps3[:, :, P_COLS[name][0]:P_COLS[name][0] + ATT_WIDTH].reshape(
            bs, DEC_T, ATT_HEADS, HEAD_DIM)
        s_new[0].append(h_all.reshape(bs, DEC_T, LRU_WIDTH)[:, DEC_T - 1])
        s_new[1].append(ps3[:, last, P_COLS["xa"][0]:P_COLS["xa"][0] + LRU_WIDTH])
        s_new[2].append(new_kv("k"))
        s_new[3].append(new_kv("v"))
        s_new[5].append(ps3[:, last, P_COLS["xbc"][0]:P_COLS["xbc"][0] + SSD_CONV_CH])

    stack = lambda parts: jnp.stack(parts) if parts else None
    outs_p = [stack(a) for a in p_new]
    outs_s = [stack(a) for a in s_new]
    outs_p[2], outs_p[3] = [jnp.transpose(t.reshape(DEPTH, bp, ATT_HEADS, HEAD_DIM, MAX_WINDOW), (0, 1, 4, 2, 3))
                            for t in tails]
    outs_s[4] = s_ssd
    return (yp.reshape(bp, seq, D_MODEL), ys.reshape(bs, DEC_T, D_MODEL), *outs_p, *outs_s)
```

```python
import functools

import numpy as np
import jax
import jax.numpy as jnp
from jax import lax
from jax.experimental import pallas as pl
from jax.experimental.pallas import tpu as pltpu

F32 = jnp.float32
BF16 = jnp.bfloat16

D_MODEL = 1024
DEPTH = 4
CONV_W = 4
HEAD_DIM = 64
ATT_WIDTH = 384
ATT_HEADS = 6
ATT_SPAN = 128
DILATIONS = (1, 4, 16)
MAX_WINDOW = 2048
ATT_SCALE = HEAD_DIM ** -0.5
SSD_WIDTH = 384
SSD_HEADS = 6
SSD_HEAD_DIM = 64
SSD_GROUPS = 2
SSD_STATE = 128
SSD_CHUNK = 128
SSD_CONV_CH = 896
LRU_WIDTH = 256
LRU_BLOCKS = 4
LRU_C = 8.0
D_FF = 2816
N_IN = 2950
EPS = 1e-6

LANES = 128
SUBLANES = 8
N_IN_PAD = 3072
VMEM_LIMIT = 56 * 1024 * 1024

W_IN_COLS = {"ga": (0, 256), "xa": (256, 256), "q": (512, 384), "k": (896, 384), "v": (1280, 384),
             "z": (1664, 384), "xbc": (2048, 896), "dt": (2944, 6)}
P_COLS = {"xbc": (0, 896), "dt": (896, 128), "ga": (1024, 256), "xa": (1280, 256), "q": (1536, 384),
          "k": (1920, 384), "v": (2304, 384), "z": (2688, 384)}


def _pcol(name, width=None):
    off, w = P_COLS[name]
    width = width or w
    assert off % width == 0
    return off // width

NEG_INF = float("-inf")
NT_DIMS = (((1,), (1,)), ((), ()))


def _cparams(sem):
    return pltpu.CompilerParams(dimension_semantics=sem, vmem_limit_bytes=VMEM_LIMIT)


def _const_spec(shape, layer=None):
    nd = len(shape)
    if layer is None:
        return pl.BlockSpec(shape, lambda *_: (0,) * nd, pipeline_mode=pl.Buffered(1))
    return pl.BlockSpec((None,) + tuple(shape), lambda *_: (layer,) + (0,) * nd, pipeline_mode=pl.Buffered(1))


def _drop_refs(kernel_fn, first, count=1):
    def wrapped(*refs, **kw):
        return kernel_fn(*refs[:first], *refs[first + count:], **kw)
    return wrapped


def _rms(x, g):
    ms = jnp.mean(x * x, axis=-1, keepdims=True)
    return x * lax.rsqrt(ms + EPS) * g


def _sigmoid(x):
    return jax.nn.sigmoid(x)


def _silu(x):
    return x * jax.nn.sigmoid(x)


def _softplus(x):
    return jnp.maximum(x, 0.0) + jnp.log1p(jnp.exp(-jnp.abs(x)))


def _gelu_tanh(x):
    c = np.sqrt(2.0 / np.pi).astype(np.float32)
    return 0.5 * x * (1.0 + jnp.tanh(c * (x + 0.044715 * (x * x * x))))


def _roll_rows(x, shift):
    n = x.shape[0]
    shift = shift % n
    if shift == 0:
        return x
    return pltpu.roll(x, shift, 0)


def _row_index(shape, group=None):
    r = lax.broadcasted_iota(jnp.int32, shape, 0)
    if group is not None:
        r = jnp.bitwise_and(r, group - 1)
    return r


def _lin_scan(a, u, row, steps):
    s = 1
    for _ in range(steps):
        keep = row >= s
        a_sh = jnp.where(keep, _roll_rows(a, s), 1.0)
        u_sh = jnp.where(keep, _roll_rows(u, s), 0.0)
        u = a * u_sh + u
        a = a * a_sh
        s *= 2
    return a, u


def _cumsum_rows(x, row, steps):
    s = 1
    for _ in range(steps):
        x = x + jnp.where(row >= s, _roll_rows(x, s), 0.0)
        s *= 2
    return x


def _suffix_sum_rows(x, row, group, steps):
    incl = x
    s = 1
    for _ in range(steps):
        incl = incl + jnp.where(row < group - s, _roll_rows(incl, -s), 0.0)
        s *= 2
    return incl - x


def _conv_taps(x, shifted_fn, w_ref, b_ref):
    y = b_ref[...] + w_ref[CONV_W - 1:CONV_W, :] * x
    for s in range(1, CONV_W):
        y = y + w_ref[CONV_W - 1 - s:CONV_W - s, :] * shifted_fn(s)
    return y


def _conv_block_carry(x, tail_ref, w_ref, b_ref):
    tb, c = x.shape
    x3 = x.reshape(tb // SUBLANES, SUBLANES, c)
    sub = lax.broadcasted_iota(jnp.int32, (1, SUBLANES, c), 1)
    tail = tail_ref[...]

    def shifted(s):
        r = pltpu.roll(x3, s, 1)
        before = jnp.concatenate([pltpu.roll(tail, s, 0)[None], r[:-1]], axis=0)
        return jnp.where(sub < s, before, r)

    y = _conv_taps(x3, shifted, w_ref, b_ref)
    tail_ref[...] = x[tb - SUBLANES:tb]
    return y.reshape(tb, c)


def _conv_groups(x, bufpad, w_ref, b_ref):
    row = _row_index(x.shape, SUBLANES)
    return _conv_taps(
        x, lambda s: jnp.where(row >= s, _roll_rows(x, s), _roll_rows(bufpad, s - (CONV_W - 1))), w_ref, b_ref)


def _lru_gates(xc, wa_ref, ba_ref, wx_ref, bx_ref, lam_ref):
    xb = xc.astype(BF16)
    r = _sigmoid(jnp.dot(xb, wa_ref[...], preferred_element_type=F32) + ba_ref[...])
    ig = _sigmoid(jnp.dot(xb, wx_ref[...], preferred_element_type=F32) + bx_ref[...])
    log_a = (-LRU_C) * r * _softplus(-lam_ref[...])
    a = jnp.exp(log_a)
    t = jnp.tanh(log_a)
    u = jnp.sqrt(-2.0 * t / (1.0 - t)) * (ig * xc)
    return a, u


def _in_proj_kernel(x_ref, g_ref, w_ref, p_ref, *tail_refs, first_tail):
    h = _rms(x_ref[...], g_ref[...]).astype(BF16)
    p_ref[...] = jnp.dot(h, w_ref[...], preferred_element_type=F32)
    if tail_refs:
        kt_ref, vt_ref = tail_refs

        @pl.when(pl.program_id(1) >= first_tail)
        def _():
            k_off, v_off = P_COLS["k"][0], P_COLS["v"][0]
            kt_ref[...] = p_ref[:, k_off:k_off + ATT_WIDTH].T
            vt_ref[...] = p_ref[:, v_off:v_off + ATT_WIDTH].T


def _in_proj(x, g, w, layer, batch, tm, tails=None):
    with_tail = tails is not None
    n = x.shape[0]
    nt = n // batch // tm
    first_tail = nt - MAX_WINDOW // tm
    row = lambda width: pl.BlockSpec((tm, width), lambda b, j: (b * nt + j, 0))
    out_specs = [row(N_IN_PAD)]
    out_shape = [jax.ShapeDtypeStruct((n, N_IN_PAD), F32)]
    if with_tail:
        tail = pl.BlockSpec((None, None, ATT_WIDTH, tm),
                            lambda b, j: (layer, b, 0, jnp.maximum(j - first_tail, 0)))
        out_specs += [tail, tail]
        out_shape += [jax.ShapeDtypeStruct(t.shape, F32) for t in tails]
    body = functools.partial(_in_proj_kernel, first_tail=first_tail)
    extra_specs, extra_args, aliases = [], [], {}
    if with_tail:
        body = _drop_refs(body, 3, 2)
        extra_specs = [pl.BlockSpec(memory_space=pl.ANY)] * 2
        extra_args = list(tails)
        aliases = {3: 1, 4: 2}
    return pl.pallas_call(
        body,
        grid=(batch, nt),
        in_specs=[row(D_MODEL), _const_spec((1, D_MODEL), layer), _const_spec((D_MODEL, N_IN_PAD), layer)]
        + extra_specs,
        out_specs=out_specs,
        out_shape=out_shape,
        input_output_aliases=aliases,
        compiler_params=_cparams(("parallel", "arbitrary")),
        name="in_proj",
    )(x, g, w, *extra_args)


FFN_CHUNK = 256


def _post_kernel(a_ref, b_ref, c_ref, x_ref, wo_ref, go_ref, gi_ref, wgu_ref, wd_ref, gf_ref, o_ref):
    mixed = jnp.concatenate([a_ref[...], b_ref[...], c_ref[...]], axis=1).astype(BF16)
    x1 = x_ref[...] + _rms(jnp.dot(mixed, wo_ref[...], preferred_element_type=F32), go_ref[...])
    h = _rms(x1, gi_ref[...]).astype(BF16)
    acc = jnp.zeros(x1.shape, F32)
    for c in range(D_FF // FFN_CHUNK):
        lo = c * FFN_CHUNK
        g = jnp.dot(h, wgu_ref[:, lo:lo + FFN_CHUNK], preferred_element_type=F32)
        u = jnp.dot(h, wgu_ref[:, D_FF + lo:D_FF + lo + FFN_CHUNK], preferred_element_type=F32)
        act = (_silu(g) * u).astype(BF16)
        acc = acc + jnp.dot(act, wd_ref[lo:lo + FFN_CHUNK, :], preferred_element_type=F32)
    o_ref[...] = x1 + _rms(acc, gf_ref[...])


def _post(a, b, c, x, wo, go, gi, wgu, wd, gf, layer, tm):
    n = x.shape[0]
    row = lambda width: pl.BlockSpec((tm, width), lambda i: (i, 0))
    vec = _const_spec((1, D_MODEL), layer)
    return pl.pallas_call(
        _post_kernel,
        grid=(n // tm,),
        in_specs=[row(LRU_WIDTH), row(ATT_WIDTH), row(SSD_WIDTH), row(D_MODEL),
                  _const_spec((D_MODEL, D_MODEL), layer), vec, vec,
                  _const_spec((D_MODEL, 2 * D_FF), layer), _const_spec((D_FF, D_MODEL), layer), vec],
        out_specs=row(D_MODEL),
        out_shape=jax.ShapeDtypeStruct((n, D_MODEL), F32),
        compiler_params=_cparams(("parallel",)),
        name="post",
    )(a, b, c, x, wo, go, gi, wgu, wd, gf)


def _mix_a_prompt_kernel(xa_ref, ga_ref, cw_ref, cb_ref, wa_ref, ba_ref, wx_ref, bx_ref, lam_ref,
                         out_ref, hlast_ref, tail_ref, hc_ref):
    @pl.when(pl.program_id(1) == 0)
    def _():
        tail_ref[...] = jnp.zeros_like(tail_ref)
        hc_ref[...] = jnp.zeros_like(hc_ref)

    x = xa_ref[...]
    tb, c = x.shape
    xc = _conv_block_carry(x, tail_ref, cw_ref, cb_ref)
    a, u = _lru_gates(xc, wa_ref, ba_ref, wx_ref, bx_ref, lam_ref)
    gate = _gelu_tanh(ga_ref[...])
    nslab = tb // SUBLANES
    a3 = a.reshape(nslab, SUBLANES, c)
    u3 = u.reshape(nslab, SUBLANES, c)
    sub = lax.broadcasted_iota(jnp.int32, (1, SUBLANES, c), 1)
    s = 1
    while s < SUBLANES:
        keep = sub >= s
        a_sh = jnp.where(keep, pltpu.roll(a3, s, 1), 1.0)
        u_sh = jnp.where(keep, pltpu.roll(u3, s, 1), 0.0)
        u3 = a3 * u_sh + u3
        a3 = a3 * a_sh
        s *= 2
    h_row = hc_ref[0:1, :]
    for k in range(nslab):
        h = u3[k] + a3[k] * h_row
        h_row = h[SUBLANES - 1:SUBLANES, :]
        rows = slice(k * SUBLANES, (k + 1) * SUBLANES)
        out_ref[rows, :] = h * gate[rows, :]
    hc_ref[...] = jnp.broadcast_to(h_row, hc_ref.shape)
    hlast_ref[...] = h_row


def _mix_a_prompt(p, cw, cb, wa, ba, wx, bx, lam, layer, batch, tb):
    n = p.shape[0]
    nb = n // batch // tb
    blk = pl.BlockSpec((tb, LRU_WIDTH), lambda b, i: (b * nb + i, 0))
    col = lambda name: pl.BlockSpec((tb, LRU_WIDTH), lambda b, i: (b * nb + i, _pcol(name)))
    vec = _const_spec((1, LRU_WIDTH), layer)
    mat = _const_spec((LRU_WIDTH, LRU_WIDTH), layer)
    return pl.pallas_call(
        _mix_a_prompt_kernel,
        grid=(batch, nb),
        in_specs=[col("xa"), col("ga"), _const_spec((CONV_W, LRU_WIDTH), layer), vec, mat, vec, mat, vec, vec],
        out_specs=[blk, pl.BlockSpec((None, 1, LRU_WIDTH), lambda b, i: (b, 0, 0))],
        out_shape=[jax.ShapeDtypeStruct((n, LRU_WIDTH), F32),
                   jax.ShapeDtypeStruct((batch, 1, LRU_WIDTH), F32)],
        scratch_shapes=[pltpu.VMEM((SUBLANES, LRU_WIDTH), F32), pltpu.VMEM((SUBLANES, LRU_WIDTH), F32)],
        compiler_params=_cparams(("parallel", "arbitrary")),
        name="mix_a_prompt",
    )(p, p, cw, cb, wa, ba, wx, bx, lam)


def _mix_a_decode_kernel(xa_ref, ga_ref, buf_ref, h0_ref, cw_ref, cb_ref, wa_ref, ba_ref, wx_ref, bx_ref,
                         lam_ref, out_ref, h_ref):
    x = xa_ref[...]
    xc = _conv_groups(x, buf_ref[...], cw_ref, cb_ref)
    a, u = _lru_gates(xc, wa_ref, ba_ref, wx_ref, bx_ref, lam_ref)
    u = u + a * h0_ref[...]
    row = _row_index(x.shape, SUBLANES)
    _, h = _lin_scan(a, u, row, 3)
    h_ref[...] = h
    out_ref[...] = h * _gelu_tanh(ga_ref[...])


def _mix_a_decode(p, bufpad, h0pad, cw, cb, wa, ba, wx, bx, lam, layer, tm):
    n = p.shape[0]
    blk = pl.BlockSpec((tm, LRU_WIDTH), lambda i: (i, 0))
    col = lambda name: pl.BlockSpec((tm, LRU_WIDTH), lambda i: (i, _pcol(name)))
    lay = pl.BlockSpec((None, tm, LRU_WIDTH), lambda i: (layer, i, 0))
    vec = _const_spec((1, LRU_WIDTH), layer)
    mat = _const_spec((LRU_WIDTH, LRU_WIDTH), layer)
    return pl.pallas_call(
        _mix_a_decode_kernel,
        grid=(n // tm,),
        in_specs=[col("xa"), col("ga"), lay, lay, _const_spec((CONV_W, LRU_WIDTH), layer), vec, mat, vec, mat, vec,
                  vec],
        out_specs=[blk, blk],
        out_shape=[jax.ShapeDtypeStruct((n, LRU_WIDTH), F32), jax.ShapeDtypeStruct((n, LRU_WIDTH), F32)],
        compiler_params=_cparams(("parallel",)),
        name="mix_a_decode",
    )(p, p, bufpad, h0pad, cw, cb, wa, ba, wx, bx, lam)


ATT_BLK = 2048
ATT_UNIT = 128
ATT_UNROLL = 4


def _attn_scores(q_t, kp_t, kc_t, bias, lane_lo):
    zero = jnp.zeros_like(q_t)
    qs = q_t * ATT_SCALE
    qq = jnp.concatenate([jnp.where(lane_lo, qs, zero), jnp.where(lane_lo, zero, qs)], axis=0).astype(BF16)
    kk = jnp.concatenate([kp_t, kc_t], axis=0).astype(BF16)
    s = lax.dot_general(qq, kk, NT_DIMS, preferred_element_type=F32) + bias
    m = jnp.max(s, axis=1, keepdims=True)
    return jnp.exp(s - m).astype(BF16), m


def _attn_output(p, m, vp_t, vc_t, lane_lo):
    zero = jnp.zeros_like(vp_t)
    pcat = jnp.concatenate([p[:ATT_UNIT], p[ATT_UNIT:]], axis=1)
    one_lo = jnp.where(lane_lo, 1.0, 0.0)
    one_hi = 1.0 - one_lo
    w = jnp.concatenate([
        jnp.concatenate([jnp.where(lane_lo, vp_t, zero), one_lo], axis=1),
        jnp.concatenate([jnp.where(lane_lo, vc_t, zero), one_lo], axis=1),
        jnp.concatenate([jnp.where(lane_lo, zero, vp_t), one_hi], axis=1),
        jnp.concatenate([jnp.where(lane_lo, zero, vc_t), one_hi], axis=1)], axis=0).astype(BF16)
    ol = jnp.dot(pcat, w, preferred_element_type=F32)
    m_t = jnp.where(lane_lo, jnp.broadcast_to(m[:ATT_UNIT], vp_t.shape),
                    jnp.broadcast_to(m[ATT_UNIT:], vp_t.shape))
    return ol[:, :LANES], m_t, ol[:, LANES:]


DEC_T = 8
DEC_NEW_PAD = 128
DEC_KEYS = MAX_WINDOW + DEC_NEW_PAD
DEC_ROWS = ATT_HEADS * DEC_T
ATT_PHASES = len(DILATIONS)


def _decode_key_multiplicity():
    pos = np.full((DEC_KEYS,), -10 ** 9, np.int64)
    pos[:MAX_WINDOW] = np.arange(MAX_WINDOW)
    pos[MAX_WINDOW:MAX_WINDOW + DEC_T] = MAX_WINDOW + np.arange(DEC_T)
    t = np.arange(DEC_T)
    dist = (MAX_WINDOW + t)[:, None] - pos[None, :]
    cnt = np.zeros(dist.shape, np.float32)
    for d in DILATIONS:
        cnt += ((dist >= 0) & (dist % d == 0) & (dist <= ATT_SPAN * d)).astype(np.float32)
    return np.tile(cnt, (ATT_HEADS, 1))


def _attn_decode_block(q_ref, kn_ref, vn_ref, kt_ref, vt_ref, mult_ref, o_ref):
    lane = lax.broadcasted_iota(jnp.int32, (DEC_T, ATT_WIDTH), 1)
    head_masks = [jnp.logical_and(lane >= HEAD_DIM * h, lane < HEAD_DIM * (h + 1)) for h in range(ATT_HEADS)]
    mult = mult_ref[...]
    seen = mult > 0.0
    zpad = jnp.zeros((DEC_NEW_PAD - DEC_T, ATT_WIDTH), F32)
    pending = []
    for b in range(kt_ref.shape[0]):
        new_rows = slice(DEC_T * b, DEC_T * (b + 1))
        qb = q_ref[new_rows, :] * ATT_SCALE
        qbd = jnp.concatenate([jnp.where(hm, qb, 0.0) for hm in head_masks], axis=0).astype(BF16)
        kt = kt_ref[b].reshape(ATT_WIDTH, MAX_WINDOW).astype(BF16)
        kn = jnp.concatenate([kn_ref[new_rows, :], zpad], axis=0).astype(BF16)
        s = jnp.concatenate([jnp.dot(qbd, kt, preferred_element_type=F32),
                             lax.dot_general(qbd, kn, NT_DIMS, preferred_element_type=F32)], axis=1)
        s = jnp.where(seen, s, NEG_INF)
        m = jnp.max(s, axis=1, keepdims=True)
        p = mult * jnp.exp(s - m)
        pending.append((p / jnp.sum(p, axis=1, keepdims=True)).astype(BF16))
    for b, p in enumerate(pending):
        new_rows = slice(DEC_T * b, DEC_T * (b + 1))
        vt = vt_ref[b].reshape(ATT_WIDTH, MAX_WINDOW).astype(BF16)
        vn = jnp.concatenate([vn_ref[new_rows, :], zpad], axis=0).astype(BF16)
        o = (lax.dot_general(p[:, :MAX_WINDOW], vt, NT_DIMS, preferred_element_type=F32)
             + jnp.dot(p[:, MAX_WINDOW:], vn, preferred_element_type=F32))
        out = jnp.zeros((DEC_T, ATT_WIDTH), F32)
        for h, hm in enumerate(head_masks):
            out = out + jnp.where(hm, o[DEC_T * h:DEC_T * (h + 1), :], 0.0)
        o_ref[new_rows, :] = out


def _attn_kernel(q_ref, kp_ref, kc_ref, vp_ref, vc_ref, qn_ref, kn_ref, vn_ref, kt_ref, vt_ref, mult_ref,
                 o_ref, od_ref, acc_ref, m_ref, l_ref, *, dec_blocks):
    sub = pl.program_id(3)
    step = ((pl.program_id(0) * pl.num_programs(1) + pl.program_id(1)) * pl.num_programs(2)
            + pl.program_id(2)) * ATT_PHASES + sub

    @pl.when(step < dec_blocks)
    def _():
        _attn_decode_block(qn_ref, kn_ref, vn_ref, kt_ref, vt_ref, mult_ref, od_ref)

    first_block = pl.program_id(2) == 0
    lane_lo = lax.broadcasted_iota(jnp.int32, (ATT_UNIT, LANES), 1) < HEAD_DIM
    qi = lax.broadcasted_iota(jnp.int32, (2 * ATT_UNIT, 2 * ATT_UNIT), 0) & (ATT_UNIT - 1)
    ki = lax.broadcasted_iota(jnp.int32, (2 * ATT_UNIT, 2 * ATT_UNIT), 1)
    is_prev = ki < ATT_UNIT
    dist = qi - ki + ATT_UNIT
    bias = jnp.where(dist >= 0, jnp.where(dist <= ATT_SPAN, 0.0, NEG_INF), NEG_INF)
    bias_first = bias + jnp.where(is_prev, jnp.where(first_block, NEG_INF, 0.0), 0.0)

    def rows(start, d):
        if d == 1:
            return pl.ds(start, ATT_UNIT)
        return pl.ds(start, ATT_UNIT, stride=d)

    def head_scores(bi, rho):
        d = DILATIONS[bi]
        cur = rows(rho, d)
        prev = rows(rho + ATT_BLK - d * ATT_UNIT, d)
        return (cur, prev, vp_ref) + _attn_scores(q_ref[cur, :], kp_ref[prev, :], kc_ref[cur, :], bias_first, lane_lo)

    def inner_scores(bi, idx):
        d = DILATIONS[bi]
        rho = idx & (d - 1)
        j = 1 + (idx >> int(np.log2(d)))
        start = rho + d * ATT_UNIT * j
        cur = rows(start, d)
        prev = rows(start - d * ATT_UNIT, d)
        return (cur, prev, vc_ref) + _attn_scores(q_ref[cur, :], kc_ref[prev, :], kc_ref[cur, :], bias, lane_lo)

    def group(bi, pending):
        for cur, prev, vprev_ref, p, m in pending:
            o_t, m_t, l_t = _attn_output(p, m, vprev_ref[prev, :], vc_ref[cur, :], lane_lo)
            acc_ref[bi, cur, :] = o_t
            m_ref[bi, cur, :] = m_t
            l_ref[bi, cur, :] = l_t

    def chunks(scores, bi, base, lo, hi):
        def body(c, carry):
            group(bi, [scores(bi, base + ATT_UNROLL * c + u) for u in range(ATT_UNROLL)])
            return carry
        lax.fori_loop(lo, hi, body, 0)

    units = ATT_BLK // ATT_UNIT
    @pl.when(sub == 0)
    def _():
        group(0, [head_scores(0, 0)] + [inner_scores(0, idx) for idx in range(ATT_UNROLL - 1)])
        chunks(inner_scores, 0, ATT_UNROLL - 1, 0, (units - ATT_UNROLL) // ATT_UNROLL)

    @pl.when(sub == 1)
    def _():
        group(1, [head_scores(1, rho) for rho in range(DILATIONS[1])])
        chunks(inner_scores, 1, 0, 0, (units - DILATIONS[1]) // ATT_UNROLL)

    @pl.when(sub == 2)
    def _():
        chunks(head_scores, 2, 0, 0, units // ATT_UNROLL)

        def combine(c, carry):
            sl = pl.ds(pl.multiple_of(c * ATT_UNIT, ATT_UNIT), ATT_UNIT)
            ms = [m_ref[bi, sl, :] for bi in range(len(DILATIONS))]
            m = functools.reduce(jnp.maximum, ms)
            ws = [jnp.exp(mi - m) for mi in ms]
            num = sum(w * acc_ref[bi, sl, :] for bi, w in enumerate(ws))
            den = sum(w * l_ref[bi, sl, :] for bi, w in enumerate(ws))
            o_ref[sl, :] = num / den
            return carry

        lax.fori_loop(0, ATT_BLK // ATT_UNIT, combine, 0, unroll=2)


def _attn(pp, ps, cache_kt, cache_vt, layer, mult, batch):
    n = pp.shape[0]
    nb = n // batch // ATT_BLK
    npair = ATT_WIDTH // LANES
    nd = ps.shape[0]
    nseq = nd // DEC_T
    steps = batch * npair * nb * ATT_PHASES
    spb = -(-nseq // steps)
    assert nseq % spb == 0
    dec_blocks = nseq // spb

    def dec_blk(b, hp, i, s):
        return jnp.minimum(((b * npair + hp) * nb + i) * ATT_PHASES + s, dec_blocks - 1)

    out = pl.BlockSpec((ATT_BLK, LANES), lambda b, hp, i, s: (b * nb + i, hp))
    cur = lambda name: pl.BlockSpec((ATT_BLK, LANES), lambda b, hp, i, s: (b * nb + i, _pcol(name, LANES) + hp))
    prev = lambda name: pl.BlockSpec(
        (ATT_BLK, LANES), lambda b, hp, i, s: (b * nb + jnp.maximum(i - 1, 0), _pcol(name, LANES) + hp))
    new = lambda name: pl.BlockSpec((DEC_T * spb, ATT_WIDTH), lambda b, hp, i, s: (dec_blk(b, hp, i, s), _pcol(name)))
    win = pl.BlockSpec((None, spb, ATT_HEADS, HEAD_DIM, MAX_WINDOW),
                       lambda b, hp, i, s: (layer, dec_blk(b, hp, i, s), 0, 0, 0))
    scratch = pltpu.VMEM((len(DILATIONS), ATT_BLK, LANES), F32)
    return pl.pallas_call(
        functools.partial(_attn_kernel, dec_blocks=dec_blocks),
        grid=(batch, npair, nb, ATT_PHASES),
        in_specs=[cur("q"), prev("k"), cur("k"), prev("v"), cur("v"), new("q"), new("k"), new("v"), win, win,
                  _const_spec((DEC_ROWS, DEC_KEYS))],
        out_specs=[out, pl.BlockSpec((DEC_T * spb, ATT_WIDTH), lambda b, hp, i, s: (dec_blk(b, hp, i, s), 0))],
        out_shape=[jax.ShapeDtypeStruct((n, ATT_WIDTH), F32), jax.ShapeDtypeStruct((nd, ATT_WIDTH), F32)],
        scratch_shapes=[scratch, scratch, scratch],
        compiler_params=_cparams(("arbitrary", "arbitrary", "arbitrary", "arbitrary")),
        name="attn",
    )(pp, pp, pp, pp, pp, ps, ps, ps, cache_kt, cache_vt, mult)


HPG = SSD_HEADS // SSD_GROUPS
B_OFF = SSD_WIDTH
C_OFF = SSD_WIDTH + SSD_GROUPS * SSD_STATE


def _ssd_chunk_diag(xbc, dt, acum, pair_ok):
    acum_t = acum.T
    ys, xrs = [], []
    for g in range(SSD_GROUPS):
        bg = xbc[:, B_OFF + g * SSD_STATE:B_OFF + (g + 1) * SSD_STATE].astype(BF16)
        cg = xbc[:, C_OFF + g * SSD_STATE:C_OFF + (g + 1) * SSD_STATE].astype(BF16)
        cb = lax.dot_general(cg, bg, NT_DIMS, preferred_element_type=F32)
        for h in range(g * HPG, (g + 1) * HPG):
            xr = xbc[:, h * SSD_HEAD_DIM:(h + 1) * SSD_HEAD_DIM] * dt[:, h:h + 1]
            diff = acum[:, h:h + 1] - acum_t[h:h + 1, :]
            lmat = jnp.exp(jnp.where(pair_ok, diff, NEG_INF))
            ys.append(jnp.dot((cb * lmat).astype(BF16), xr.astype(BF16), preferred_element_type=F32))
            xrs.append(xr)
    return ys, xrs


def _ssd_finish(y, xs, z, dskip_ref, norm_ref):
    y = y + dskip_ref[...] * xs
    y = y * _silu(z)
    return _rms(y, norm_ref[...])


DT_COPIES = 3
SSD_PAIRS = SSD_HEADS // 2
SSD_CONV_ROWS = 128


def _ssd_prompt_kernel(xbc_ref, z_ref, dt_ref, cw_ref, cb_ref, dtb_ref, alog_ref, dskip_ref, norm_ref,
                       out_ref, state_ref, tail_ref, xc_ref, st_ref):
    tb = xbc_ref.shape[0]

    @pl.when(pl.program_id(1) == 0)
    def _():
        tail_ref[...] = jnp.zeros_like(tail_ref)
        st_ref[...] = jnp.zeros_like(st_ref)

    for r0 in range(0, tb, SSD_CONV_ROWS):
        xc_ref[r0:r0 + SSD_CONV_ROWS, :] = _silu(
            _conv_block_carry(xbc_ref[r0:r0 + SSD_CONV_ROWS, :], tail_ref, cw_ref, cb_ref))
    a_neg = -jnp.exp(alog_ref[...])
    row = _row_index((SSD_CHUNK, LANES))
    lane = lax.broadcasted_iota(jnp.int32, (SSD_CHUNK, LANES), 1)
    lane_lo = lane < SSD_HEAD_DIM
    causal = row >= lane
    zero = jnp.zeros((SSD_CHUNK, LANES), F32)

    def pair_rows(t):
        return jnp.concatenate([jnp.where(lane_lo, t, zero), jnp.where(lane_lo, zero, t)], axis=0).astype(BF16)

    def chunk(c, carry):
        r0 = pl.multiple_of(c * SSD_CHUNK, SSD_CHUNK)
        rows = pl.ds(r0, SSD_CHUNK)
        xc = xc_ref[rows, :]
        dt = _softplus(dt_ref[rows, :] + dtb_ref[...])
        acum = _cumsum_rows(dt * a_neg, row, 7)
        last = acum[SSD_CHUNK - 1:SSD_CHUNK, :]
        tot = jnp.exp(last)
        pt = jnp.where(lane < SUBLANES, acum, jnp.where(lane < 2 * SUBLANES, dt, dt * jnp.exp(last - acum))).T
        bs = [xc[:, B_OFF + g * SSD_STATE:B_OFF + (g + 1) * SSD_STATE] for g in range(SSD_GROUPS)]
        cs = [xc[:, C_OFF + g * SSD_STATE:C_OFF + (g + 1) * SSD_STATE].astype(BF16) for g in range(SSD_GROUPS)]
        bts = [b.T for b in bs]
        cbs = [lax.dot_general(cs[g], bs[g].astype(BF16), NT_DIMS, preferred_element_type=F32)
               for g in range(SSD_GROUPS)]
        gs, eacs, btws = [], [], []
        for h in range(SSD_HEADS):
            g = h // HPG
            a_col = jnp.broadcast_to(acum[:, h:h + 1], (SSD_CHUNK, SSD_CHUNK))
            lmat = jnp.exp(jnp.where(causal, a_col - pt[h:h + 1, :], NEG_INF))
            gs.append((cbs[g] * lmat * pt[SUBLANES + h:SUBLANES + h + 1, :]).astype(BF16))
            eacs.append(jnp.exp(a_col))
            btws.append((bts[g] * pt[2 * SUBLANES + h:2 * SUBLANES + h + 1, :]).astype(BF16))
        outs = []
        for k in range(SSD_PAIRS):
            h0, h1 = 2 * k, 2 * k + 1
            g0, g1 = h0 // HPG, h1 // HPG
            x2 = pair_rows(xc[:, k * LANES:(k + 1) * LANES])
            y = jnp.dot(jnp.concatenate([gs[h0], gs[h1]], axis=1), x2, preferred_element_type=F32)
            st = st_ref[k]
            if g0 == g1:
                y_off = jnp.dot(cs[g0], st.astype(BF16), preferred_element_type=F32)
            else:
                y_off = jnp.dot(jnp.concatenate([cs[g0], cs[g1]], axis=1), pair_rows(st),
                                preferred_element_type=F32)
            outs.append(y + y_off * jnp.where(lane_lo, eacs[h0], eacs[h1]))
            upd = jnp.dot(jnp.concatenate([btws[h0], btws[h1]], axis=1), x2, preferred_element_type=F32)
            st_ref[k] = st * jnp.where(lane_lo, tot[:, h0:h0 + 1], tot[:, h1:h1 + 1]) + upd
        y = jnp.concatenate(outs, axis=1)
        out_ref[rows, :] = _ssd_finish(y, xc[:, :SSD_WIDTH], z_ref[rows, :], dskip_ref, norm_ref)
        return carry

    lax.fori_loop(0, tb // SSD_CHUNK, chunk, 0)
    for k in range(SSD_PAIRS):
        t = st_ref[k].T
        state_ref[2 * k] = t[:SSD_HEAD_DIM]
        state_ref[2 * k + 1] = t[SSD_HEAD_DIM:]


def _ssd_prompt(p, cw, cb, dtb, alog, dskip, norm, layer, batch, tb):
    n = p.shape[0]
    nb = n // batch // tb
    blk = lambda width: pl.BlockSpec((tb, width), lambda b, i: (b * nb + i, 0))
    col = lambda name: pl.BlockSpec((tb, P_COLS[name][1]), lambda b, i: (b * nb + i, _pcol(name)))
    return pl.pallas_call(
        _ssd_prompt_kernel,
        grid=(batch, nb),
        in_specs=[col("xbc"), col("z"), col("dt"),
                  _const_spec((CONV_W, SSD_CONV_CH), layer), _const_spec((1, SSD_CONV_CH), layer),
                  _const_spec((1, LANES), layer), _const_spec((1, LANES), layer),
                  _const_spec((1, SSD_WIDTH), layer), _const_spec((1, SSD_WIDTH), layer)],
        out_specs=[blk(SSD_WIDTH),
                   pl.BlockSpec((None, SSD_HEADS, SSD_HEAD_DIM, SSD_STATE), lambda b, i: (b, 0, 0, 0))],
        out_shape=[jax.ShapeDtypeStruct((n, SSD_WIDTH), F32),
                   jax.ShapeDtypeStruct((batch, SSD_HEADS, SSD_HEAD_DIM, SSD_STATE), F32)],
        scratch_shapes=[pltpu.VMEM((SUBLANES, SSD_CONV_CH), F32),
                        pltpu.VMEM((tb, SSD_CONV_CH), F32),
                        pltpu.VMEM((SSD_PAIRS, SSD_STATE, LANES), F32)],
        compiler_params=_cparams(("parallel", "arbitrary")),
        name="ssd_prompt",
    )(p, p, p, cw, cb, dtb, alog, dskip, norm)


SSD_DEC_SEQ = SSD_CHUNK // DEC_T


def _ssd_decode_kernel(xbc_ref, z_ref, dt_ref, buf_ref, h0_ref, cw_ref, cb_ref, dtb_ref, alog_ref, dskip_ref,
                       norm_ref, out_ref, hnew_ref, xc_ref, xrd_ref, eac_ref, tot_ref, yoff_ref):
    xc = _silu(_conv_groups(xbc_ref[...], buf_ref[...], cw_ref, cb_ref))
    xc_ref[...] = xc
    a_neg = -jnp.exp(alog_ref[...])
    row = _row_index((SSD_CHUNK, LANES), DEC_T)
    li = lax.broadcasted_iota(jnp.int32, (SSD_CHUNK, SSD_CHUNK), 0)
    si = lax.broadcasted_iota(jnp.int32, (SSD_CHUNK, SSD_CHUNK), 1)
    same_seq_causal = jnp.logical_and(li >= si, (li - si) <= (li & (DEC_T - 1)))
    dt = _softplus(dt_ref[...] + dtb_ref[...])
    dta = dt * a_neg
    acum = _cumsum_rows(dta, row, 3)
    rest = _suffix_sum_rows(dta, row, DEC_T, 3)
    eac_ref[...] = jnp.exp(acum)
    tot_ref[...] = jnp.exp(acum + rest)
    decay = jnp.exp(rest)
    ys, xrs = _ssd_chunk_diag(xc, dt, acum, same_seq_causal)
    xrd_ref[...] = jnp.concatenate([xrs[h] * decay[:, h:h + 1] for h in range(SSD_HEADS)], axis=1).T
    seq_of_lane = lax.shift_right_logical(lax.broadcasted_iota(jnp.int32, (SSD_HEAD_DIM, SSD_CHUNK), 1), 3)
    b_all = [xc[:, B_OFF + g * SSD_STATE:B_OFF + (g + 1) * SSD_STATE].astype(BF16) for g in range(SSD_GROUPS)]

    def seq(b, carry):
        r0 = pl.multiple_of(b * DEC_T, DEC_T)
        rows = pl.ds(r0, DEC_T)
        xcb = xc_ref[rows, :]
        eac = eac_ref[rows, :]
        tot = tot_ref[rows, :]
        own = seq_of_lane == b
        outs = []
        for h in range(SSD_HEADS):
            g = h // HPG
            cg = xcb[:, C_OFF + g * SSD_STATE:C_OFF + (g + 1) * SSD_STATE].astype(BF16)
            prev = h0_ref[b, h]
            outs.append(lax.dot_general(cg, prev.astype(BF16), NT_DIMS, preferred_element_type=F32)
                        * eac[:, h:h + 1])
            lhs = jnp.where(own, xrd_ref[h * SSD_HEAD_DIM:(h + 1) * SSD_HEAD_DIM, :], 0.0).astype(BF16)
            st = jnp.dot(lhs, b_all[g], preferred_element_type=F32)
            hnew_ref[b, h] = prev * tot[0:1, h:h + 1] + st
        yoff_ref[rows, :] = jnp.concatenate(outs, axis=1)
        return carry

    lax.fori_loop(0, SSD_DEC_SEQ, seq, 0, unroll=4)
    y = jnp.concatenate(ys, axis=1) + yoff_ref[...]
    out_ref[...] = _ssd_finish(y, xc[:, :SSD_WIDTH], z_ref[...], dskip_ref, norm_ref)


def _ssd_decode(p, bufpad, state, stacked, layer, cw, cb, dtb, alog, dskip, norm):
    n = p.shape[0]
    blk = lambda width: pl.BlockSpec((SSD_CHUNK, width), lambda i: (i, 0))
    col = lambda name: pl.BlockSpec((SSD_CHUNK, P_COLS[name][1]), lambda i: (i, _pcol(name)))
    st_in = pl.BlockSpec((None, SSD_DEC_SEQ, SSD_HEADS, SSD_HEAD_DIM, SSD_STATE), lambda i: (layer, i, 0, 0, 0))
    return pl.pallas_call(
        _drop_refs(_ssd_decode_kernel, 11),
        grid=(n // SSD_CHUNK,),
        in_specs=[col("xbc"), col("z"), col("dt"),
                  pl.BlockSpec((None, SSD_CHUNK, SSD_CONV_CH), lambda i: (layer, i, 0)), st_in,
                  _const_spec((CONV_W, SSD_CONV_CH), layer), _const_spec((1, SSD_CONV_CH), layer),
                  _const_spec((1, LANES), layer), _const_spec((1, LANES), layer),
                  _const_spec((1, SSD_WIDTH), layer), _const_spec((1, SSD_WIDTH), layer),
                  pl.BlockSpec(memory_space=pl.ANY)],
        out_specs=[blk(SSD_WIDTH), st_in],
        out_shape=[jax.ShapeDtypeStruct((n, SSD_WIDTH), F32), jax.ShapeDtypeStruct(stacked.shape, F32)],
        input_output_aliases={11: 1},
        scratch_shapes=[pltpu.VMEM((SSD_CHUNK, SSD_CONV_CH), F32),
                        pltpu.VMEM((SSD_WIDTH, SSD_CHUNK), F32),
                        pltpu.VMEM((SSD_CHUNK, LANES), F32),
                        pltpu.VMEM((SSD_CHUNK, LANES), F32),
                        pltpu.VMEM((SSD_CHUNK, SSD_WIDTH), F32)],
        compiler_params=_cparams(("parallel",)),
        name="ssd_decode",
    )(p, p, p, bufpad, state, cw, cb, dtb, alog, dskip, norm, stacked)


DENSE_TM = 512
IN_PROJ_TM = 1024
MIX_A_TB = 1024
SSD_TB = 2048


def _pad_state_rows(buf):
    l, b, r, c = buf.shape
    return jnp.pad(buf, ((0, 0), (0, 0), (0, DEC_T - r), (0, 0))).reshape(l, b * DEC_T, c)


def kernel(x_prompt, x_sample, state_lru_h, state_lru_conv, cache_swa_k, cache_swa_v, state_ssd, state_ssd_conv,
           norm_mix_in, norm_mix_out, w_in, conv_a_w, conv_a_b, lru_wa, lru_ba, lru_wx, lru_bx, lru_lambda,
           conv_c_w, conv_c_b, dt_bias, a_log, d_skip, ssm_norm, w_out, norm_ffn_in, norm_ffn_out,
           w_gate_up, w_down):
    bp, seq, _ = x_prompt.shape
    bs, dec_t, _ = x_sample.shape
    assert dec_t == DEC_T and seq % ATT_BLK == 0 and seq >= MAX_WINDOW and bs % SSD_DEC_SEQ == 0

    def dt_lanes(p):
        slot = jnp.pad(p, [(0, 0)] * (p.ndim - 1) + [(0, SUBLANES - SSD_HEADS)])
        rep = jnp.concatenate([slot] * DT_COPIES, axis=-1)
        return jnp.pad(rep, [(0, 0)] * (p.ndim - 1) + [(0, LANES - DT_COPIES * SUBLANES)])

    def w_cols(name):
        off, width = W_IN_COLS[name]
        blk = w_in[:, :, off:off + width]
        if name == "dt":
            return dt_lanes(blk)
        return jnp.pad(blk, ((0, 0), (0, 0), (0, P_COLS[name][1] - width)))

    w_in_b = jnp.concatenate([w_cols(name) for name in P_COLS], axis=2).astype(BF16)
    w_out_b = w_out.astype(BF16)
    w_gu_b = w_gate_up.astype(BF16)
    w_dn_b = w_down.astype(BF16)
    eye = jnp.eye(LRU_BLOCKS, dtype=F32)

    def block_diag(w):
        return (w[:, :, :, None, :] * eye[None, :, None, :, None]).reshape(
            DEPTH, LRU_WIDTH, LRU_WIDTH).astype(BF16)

    wa_bd = block_diag(lru_wa)
    wx_bd = block_diag(lru_wx)
    vec = lambda p: p[:, None, :]
    dtb_p = dt_lanes(dt_bias)[:, None, :]
    alog_p = dt_lanes(a_log)[:, None, :]
    dskip_p = jnp.repeat(d_skip, SSD_HEAD_DIM, axis=1)[:, None, :]
    buf_a = _pad_state_rows(state_lru_conv)
    buf_c = _pad_state_rows(state_ssd_conv)
    h0_a = jnp.pad(state_lru_h[:, :, None, :], ((0, 0), (0, 0), (0, DEC_T - 1), (0, 0))).reshape(
        DEPTH, bs * DEC_T, LRU_WIDTH)
    mult = jnp.asarray(_decode_key_multiplicity())
    cache_kt = jnp.transpose(cache_swa_k, (0, 1, 3, 4, 2))
    cache_vt = jnp.transpose(cache_swa_v, (0, 1, 3, 4, 2))

    yp = x_prompt.reshape(bp * seq, D_MODEL)
    ys = x_sample.reshape(bs * DEC_T, D_MODEL)
    p_new = [[] for _ in range(6)]
    s_new = [[] for _ in range(6)]
    a_args = (conv_a_w, vec(conv_a_b), wa_bd, vec(lru_ba), wx_bd, vec(lru_bx), vec(lru_lambda))
    c_args = (conv_c_w, vec(conv_c_b), dtb_p, alog_p, dskip_p, vec(ssm_norm))
    post_w = (w_out_b, vec(norm_mix_out), vec(norm_ffn_in), w_gu_b, w_dn_b, vec(norm_ffn_out))
    g_in = vec(norm_mix_in)
    tails = [jnp.zeros((DEPTH, bp, ATT_WIDTH, MAX_WINDOW), F32) for _ in range(2)]
    s_ssd = jnp.zeros(state_ssd.shape, F32)
    for l in range(DEPTH):
        pp, *tails = _in_proj(yp, g_in, w_in_b, l, bp, IN_PROJ_TM, tails=tails)
        (ps,) = _in_proj(ys, g_in, w_in_b, l, 1, IN_PROJ_TM)
        out_b, out_b_s = _attn(pp, ps, cache_kt, cache_vt, l, mult, bp)

        out_a, h_last = _mix_a_prompt(pp, *a_args, layer=l, batch=bp, tb=MIX_A_TB)
        out_c, ssd_state = _ssd_prompt(pp, *c_args, layer=l, batch=bp, tb=SSD_TB)
        yp = _post(out_a, out_b, out_c, yp, *post_w, layer=l, tm=DENSE_TM)
        pp3 = pp.reshape(bp, seq, N_IN_PAD)
        last = slice(seq - (CONV_W - 1), seq)
        p_new[0].append(h_last.reshape(bp, LRU_WIDTH))
        p_new[1].append(pp3[:, last, P_COLS["xa"][0]:P_COLS["xa"][0] + LRU_WIDTH])
        p_new[4].append(ssd_state)
        p_new[5].append(pp3[:, last, P_COLS["xbc"][0]:P_COLS["xbc"][0] + SSD_CONV_CH])

        out_a, h_all = _mix_a_decode(ps, buf_a, h0_a, *a_args, layer=l, tm=DENSE_TM)
        out_c, s_ssd = _ssd_decode(ps, buf_c, state_ssd, s_ssd, l, *c_args)
        ys = _post(out_a, out_b_s, out_c, ys, *post_w, layer=l, tm=DENSE_TM)
        ps3 = ps.reshape(bs, DEC_T, N_IN_PAD)
        last = slice(DEC_T - (CONV_W - 1), DEC_T)
        new_kv = lambda name: ps3[:, :, P_COLS[name][0]:P_COLS[name][0] + ATT_WIDTH].reshape(
            bs, DEC_T, ATT_HEADS, HEAD_DIM)
        s_new[0].append(h_all.reshape(bs, DEC_T, LRU_WIDTH)[:, DEC_T - 1])
        s_new[1].append(ps3[:, last, P_COLS["xa"][0]:P_COLS["xa"][0] + LRU_WIDTH])
        s_new[2].append(new_kv("k"))
        s_new[3].append(new_kv("v"))
        s_new[5].append(ps3[:, last, P_COLS["xbc"][0]:P_COLS["xbc"][0] + SSD_CONV_CH])

    stack = lambda parts: jnp.stack(parts) if parts else None
    outs_p = [stack(a) for a in p_new]
    outs_s = [stack(a) for a in s_new]
    outs_p[2], outs_p[3] = [jnp.transpose(t.reshape(DEPTH, bp, ATT_HEADS, HEAD_DIM, MAX_WINDOW), (0, 1, 4, 2, 3))
                            for t in tails]
    outs_s[4] = s_ssd
    return (yp.reshape(bp, seq, D_MODEL), ys.reshape(bs, DEC_T, D_MODEL), *outs_p, *outs_s)
```
